```python
import math
import jax
import jax.numpy as jnp
from jax import lax
import numpy as np

D_MODEL = 1024
BATCH = 8
SEQ = 2048
DEPTH = 4

GRID_W = 64
CTX_LEN = 256
HEAD_DIM = 64
ROPE_THETA = 10000.0
RMS_EPS = 1e-6
LN_EPS = 1e-5
NEG_INF = -1e30
Q_BLOCK = 128

MLA_HEADS = 4
MLA_Q_LORA = 256
MLA_KV_LORA = 128
MLA_NOPE = 64
MLA_ROPE = 32
MLA_V = 64

GQA_Q_HEADS = 4
GQA_KV_HEADS = 2
WINDOW = 128
W_BLOCK = 128

S5_CHANNELS = 256
S5_GROUP = 16
S5_GROUPS = S5_CHANNELS // S5_GROUP
S5_STATE = 64
S5_DT_MIN = 1e-3
S5_DT_MAX = 1e-1

GMLP_WIDTH = 256
GMLP_CHUNK = 128
GMLP_GROUPS = 4
GMLP_GROUP_DIM = GMLP_WIDTH // GMLP_GROUPS

N_BRANCH = 4
BRANCH_WIDTH = 256
D_FF = 2816
MACARON_WEIGHT = 0.5
N_MOD = 9
MOD_CTX_LAST = 5

SEG_WIDTHS = (MLA_KV_LORA, MLA_ROPE, GQA_KV_HEADS * HEAD_DIM, GQA_KV_HEADS * HEAD_DIM, S5_CHANNELS, MLA_Q_LORA, GQA_Q_HEADS * HEAD_DIM, 2 * GMLP_WIDTH, N_BRANCH * D_MODEL)
N_CTX_SEGS = 5
IN_COLS = MLA_KV_LORA + MLA_ROPE + 2 * GQA_KV_HEADS * HEAD_DIM + S5_CHANNELS + MLA_Q_LORA + GQA_Q_HEADS * HEAD_DIM + 2 * GMLP_WIDTH + N_BRANCH * D_MODEL

kernel_name = 'hybrid_prefix_dit_trunk'

F32 = jnp.float32


def split_cols(p, widths):
    parts, start = [], 0
    for w in widths:
        parts.append(p[..., start:start + w])
        start += w
    return parts


def rms_norm(x, gain):
    xf = x.astype(F32)
    y = xf * lax.rsqrt(jnp.mean(xf * xf, axis=-1, keepdims=True) + RMS_EPS)
    return (y * gain.astype(F32)).astype(x.dtype)


def layer_norm(x, gain):
    xf = x.astype(F32)
    xc = xf - jnp.mean(xf, axis=-1, keepdims=True)
    y = xc * lax.rsqrt(jnp.mean(xc * xc, axis=-1, keepdims=True) + LN_EPS)
    return (y * gain.astype(F32)).astype(x.dtype)


def axial_rope_tables(rows, rot_dim):
    axis_dim = rot_dim // 2
    inv_freq = ROPE_THETA ** (-jnp.arange(0, axis_dim, 2, dtype=F32) / axis_dim)
    row = jnp.repeat(jnp.arange(rows, dtype=F32), GRID_W)
    col = jnp.tile(jnp.arange(GRID_W, dtype=F32), rows)
    ang_r = row[:, None] * inv_freq[None, :]
    ang_c = col[:, None] * inv_freq[None, :]
    return (jnp.cos(ang_r), jnp.sin(ang_r), jnp.cos(ang_c), jnp.sin(ang_c))


def _rotate_half(x, cos, sin):
    x1, x2 = jnp.split(x, 2, axis=-1)
    cos = cos[None, :, None, :]
    sin = sin[None, :, None, :]
    return jnp.concatenate([x1 * cos - x2 * sin, x2 * cos + x1 * sin], axis=-1)


def axial_rope(x, tables):
    cos_r, sin_r, cos_c, sin_c = tables
    x_row, x_col = jnp.split(x.astype(F32), 2, axis=-1)
    out = jnp.concatenate([_rotate_half(x_row, cos_r, sin_r), _rotate_half(x_col, cos_c, sin_c)], axis=-1)
    return out.astype(x.dtype)


def blocked_attention(q, k, v, scale):
    b, lq, h, d = q.shape
    nb = lq // Q_BLOCK
    qb = q.reshape(b, nb, Q_BLOCK, h, d).transpose(1, 0, 2, 3, 4)

    def one_block(q_blk):
        s = jnp.einsum('bqhd,bkhd->bhqk', q_blk, k).astype(F32) * scale
        p = jax.nn.softmax(s, axis=-1).astype(v.dtype)
        return jnp.einsum('bhqk,bkhd->bqhd', p, v)

    o = lax.map(one_block, qb)
    return o.transpose(1, 0, 2, 3, 4).reshape(b, lq, h, v.shape[-1])


def mla_queries(q_lora, norm_g, w_uq, rope):
    b, n, _ = q_lora.shape
    q = (rms_norm(q_lora, norm_g) @ w_uq).reshape(b, n, MLA_HEADS, MLA_NOPE + MLA_ROPE)
    q_nope, q_pe = q[..., :MLA_NOPE], q[..., MLA_NOPE:]
    if rope is not None:
        q_pe = axial_rope(q_pe, rope)
    return jnp.concatenate([q_nope, q_pe], axis=-1)


def mla_keys_values(kv_lora, k_pe, norm_g, w_ukv, rope):
    b, n, _ = kv_lora.shape
    kv = (rms_norm(kv_lora, norm_g) @ w_ukv).reshape(b, n, MLA_HEADS, MLA_NOPE + MLA_V)
    k_nope, v = kv[..., :MLA_NOPE], kv[..., MLA_NOPE:]
    k_pe = k_pe[:, :, None, :]
    if rope is not None:
        k_pe = axial_rope(k_pe, rope)
    k = jnp.concatenate([k_nope, jnp.broadcast_to(k_pe, (b, n, MLA_HEADS, MLA_ROPE))], axis=-1)
    return k, v


def window_gqa_latent(q, k, v, kc, vc, sink):
    b, n, hq, d = q.shape
    hkv = k.shape[2]
    g = hq // hkv
    nb = n // W_BLOCK
    nc = kc.shape[1]
    scale = d ** -0.5

    def band(t):
        tp = jnp.pad(t, ((0, 0), (W_BLOCK, W_BLOCK), (0, 0), (0, 0)))
        tp = tp.reshape(b, nb + 2, W_BLOCK, hkv, d)
        return jnp.concatenate([tp[:, :-2], tp[:, 1:-1], tp[:, 2:]], axis=2)

    kb, vb = band(k), band(v)
    qb = q.reshape(b, nb, W_BLOCK, hkv, g, d)
    s_band = jnp.einsum('bnqkgd,bnjkd->bnkgqj', qb, kb).astype(F32) * scale
    qpos = jnp.arange(nb)[:, None] * W_BLOCK + jnp.arange(W_BLOCK)[None, :]
    kpos = jnp.arange(nb)[:, None] * W_BLOCK - W_BLOCK + jnp.arange(3 * W_BLOCK)[None, :]
    valid = ((jnp.abs(qpos[:, :, None] - kpos[:, None, :]) <= WINDOW)
             & (kpos[:, None, :] >= 0) & (kpos[:, None, :] < n))
    s_band = jnp.where(valid[None, :, None, None], s_band, NEG_INF)
    s_ctx = jnp.einsum('bnqkgd,bjkd->bnkgqj', qb, kc).astype(F32) * scale
    sink_l = jnp.broadcast_to(sink.astype(F32).reshape(1, 1, hkv, g, 1, 1), s_band.shape[:-1] + (1,))
    probs = jax.nn.softmax(jnp.concatenate([s_band, s_ctx, sink_l], axis=-1), axis=-1).astype(v.dtype)
    nbk = 3 * W_BLOCK
    o = (jnp.einsum('bnkgqj,bnjkd->bnqkgd', probs[..., :nbk], vb)
         + jnp.einsum('bnkgqj,bjkd->bnqkgd', probs[..., nbk:nbk + nc], vc))
    return o.reshape(b, n, hq, d)


def sink_gqa_context(q, k, v, sink):
    b, n, hq, d = q.shape
    hkv = k.shape[2]
    g = hq // hkv
    qg = q.reshape(b, n, hkv, g, d)
    s = jnp.einsum('bqkgd,bjkd->bkgqj', qg, k).astype(F32) * d ** -0.5
    sink_l = jnp.broadcast_to(sink.astype(F32).reshape(1, hkv, g, 1, 1), s.shape[:-1] + (1,))
    p = jax.nn.softmax(jnp.concatenate([s, sink_l], axis=-1), axis=-1)[..., :-1].astype(v.dtype)
    return jnp.einsum('bkgqj,bjkd->bqkgd', p, v).reshape(b, n, hq, d)


def s5_discretize(lam_re, lam_im, log_dt, b_re, b_im):
    lam_re = jnp.minimum(lam_re.astype(F32), -1e-4)
    lam_im = lam_im.astype(F32)
    dt = jnp.exp(log_dt.astype(F32))[:, None]
    mag = jnp.exp(lam_re * dt)
    a_re = mag * jnp.cos(lam_im * dt)
    a_im = mag * jnp.sin(lam_im * dt)
    nr, ni = a_re - 1.0, a_im
    den = lam_re * lam_re + lam_im * lam_im
    coef_re = (nr * lam_re + ni * lam_im) / den
    coef_im = (ni * lam_re - nr * lam_im) / den
    b_re, b_im = b_re.astype(F32), b_im.astype(F32)
    bb_re = coef_re[..., None] * b_re - coef_im[..., None] * b_im
    bb_im = coef_re[..., None] * b_im + coef_im[..., None] * b_re
    return a_re, a_im, bb_re, bb_im


def _complex_affine_combine(e1, e2):
    a1r, a1i, b1r, b1i = e1
    a2r, a2i, b2r, b2i = e2
    return (a2r * a1r - a2i * a1i, a2r * a1i + a2i * a1r,
            a2r * b1r - a2i * b1i + b2r, a2r * b1i + a2i * b1r + b2i)


def s5_states(u, disc, s0, reverse):
    a_re, a_im, bb_re, bb_im = disc
    bu_re = jnp.einsum('blgh,gph->blgp', u, bb_re)
    bu_im = jnp.einsum('blgh,gph->blgp', u, bb_im)
    if s0 is not None:
        s0_re, s0_im = s0
        first = -1 if reverse else 0
        bu_re = bu_re.at[:, first].add(a_re * s0_re - a_im * s0_im)
        bu_im = bu_im.at[:, first].add(a_re * s0_im + a_im * s0_re)
    elems = (jnp.broadcast_to(a_re, bu_re.shape), jnp.broadcast_to(a_im, bu_re.shape), bu_re, bu_im)
    _, _, s_re, s_im = lax.associative_scan(_complex_affine_combine, elems, reverse=reverse, axis=1)
    return s_re, s_im


def s5_readout(states, c_re, c_im):
    s_re, s_im = states
    return jnp.einsum('blgp,ghp->blgh', s_re, c_re) - jnp.einsum('blgp,ghp->blgh', s_im, c_im)


def s5_output(u, fwd, bwd, c_re, c_im, d_skip, w_glu, b_glu, out_dtype):
    c_re, c_im = c_re.astype(F32), c_im.astype(F32)
    y = (s5_readout(fwd, c_re[0], c_im[0]) + s5_readout(bwd, c_re[1], c_im[1])
         + d_skip.astype(F32).reshape(S5_GROUPS, S5_GROUP) * u)
    y = jax.nn.gelu(y.reshape(u.shape[0], u.shape[1], S5_CHANNELS))
    a, g = jnp.split(y @ w_glu.astype(F32) + b_glu.astype(F32), 2, axis=-1)
    return (a * jax.nn.sigmoid(g)).astype(out_dtype)


def chunk_gmlp(z, norm_g, w_s, b_s):
    b, n, _ = z.shape
    u, v = jnp.split(jax.nn.gelu(z), 2, axis=-1)
    v = layer_norm(v, norm_g).reshape(b, n // GMLP_CHUNK, GMLP_CHUNK, GMLP_GROUPS, GMLP_GROUP_DIM)
    mixed = jnp.einsum('gij,bcjgd->bcigd', w_s, v) + b_s.T[:, :, None]
    return u * mixed.reshape(b, n, GMLP_WIDTH)


def merge_branches(branches, gate_logits, w_branch, w_out):
    gates = jax.nn.sigmoid(gate_logits.astype(F32)).astype(gate_logits.dtype)
    gates = gates.reshape(gate_logits.shape[:-1] + (N_BRANCH, D_MODEL))
    merged = sum(gates[..., i, :] * (y @ w_branch[i]) for i, y in enumerate(branches))
    return merged @ w_out


def token_mix(hl, hc, tab_mla, tab_gqa, lp, need_ctx):
    b, n, _ = hl.shape
    nc = hc.shape[1]
    kvl_l, kpe_l, gk_l, gv_l, u_l, ql_l, gq_l, z_l, gate_l = split_cols(hl @ lp['w_in'], SEG_WIDTHS)
    ctx_widths = SEG_WIDTHS if need_ctx else SEG_WIDTHS[:N_CTX_SEGS]
    parts_c = split_cols(hc @ lp['w_in'][:, :sum(ctx_widths)], ctx_widths)
    kvl_c, kpe_c, gk_c, gv_c, u_c = parts_c[:N_CTX_SEGS]

    mla_scale = (MLA_NOPE + MLA_ROPE) ** -0.5
    k_lat, v_lat = mla_keys_values(kvl_l, kpe_l, lp['mla_kv_norm'], lp['mla_w_ukv'], tab_mla)
    k_ctx, v_ctx = mla_keys_values(kvl_c, kpe_c, lp['mla_kv_norm'], lp['mla_w_ukv'], None)
    q_lat = mla_queries(ql_l, lp['mla_q_norm'], lp['mla_w_uq'], tab_mla)
    a_l = blocked_attention(q_lat, jnp.concatenate([k_lat, k_ctx], axis=1),
                            jnp.concatenate([v_lat, v_ctx], axis=1), mla_scale).reshape(b, n, BRANCH_WIDTH)

    gq_lat = axial_rope(gq_l.reshape(b, n, GQA_Q_HEADS, HEAD_DIM), tab_gqa)
    gk_lat = axial_rope(gk_l.reshape(b, n, GQA_KV_HEADS, HEAD_DIM), tab_gqa)
    gv_lat = gv_l.reshape(b, n, GQA_KV_HEADS, HEAD_DIM)
    gk_ctx = gk_c.reshape(b, nc, GQA_KV_HEADS, HEAD_DIM)
    gv_ctx = gv_c.reshape(b, nc, GQA_KV_HEADS, HEAD_DIM)
    b_l = window_gqa_latent(gq_lat, gk_lat, gv_lat, gk_ctx, gv_ctx, lp['gqa_sink']).reshape(b, n, BRANCH_WIDTH)

    disc_f = s5_discretize(lp['s5_lam_re'][0], lp['s5_lam_im'][0], lp['s5_log_dt'][0], lp['s5_b_re'][0], lp['s5_b_im'][0])
    disc_b = s5_discretize(lp['s5_lam_re'][1], lp['s5_lam_im'][1], lp['s5_log_dt'][1], lp['s5_b_re'][1], lp['s5_b_im'][1])
    uc = u_c.astype(F32).reshape(b, nc, S5_GROUPS, S5_GROUP)
    ul = u_l.astype(F32).reshape(b, n, S5_GROUPS, S5_GROUP)
    fwd_c = s5_states(uc, disc_f, None, False)
    bwd_c = s5_states(uc, disc_b, None, True)
    fwd_l = s5_states(ul, disc_f, (fwd_c[0][:, -1], fwd_c[1][:, -1]), False)
    bwd_l = s5_states(ul, disc_b, (bwd_c[0][:, 0], bwd_c[1][:, 0]), True)
    c_l = s5_output(ul, fwd_l, bwd_l, lp['s5_c_re'], lp['s5_c_im'], lp['s5_d'], lp['s5_w_glu'], lp['s5_b_glu'], hl.dtype)

    d_l = chunk_gmlp(z_l, lp['gmlp_norm'], lp['gmlp_w_s'], lp['gmlp_b_s'])

    yl = merge_branches((a_l, b_l, c_l, d_l), gate_l, lp['w_branch'], lp['w_out'])
    if not need_ctx:
        return yl, None

    ql_c, gq_c, z_c, gate_c = parts_c[N_CTX_SEGS:]
    q_ctx = mla_queries(ql_c, lp['mla_q_norm'], lp['mla_w_uq'], None)
    a_c = blocked_attention(q_ctx, k_ctx, v_ctx, mla_scale).reshape(b, nc, BRANCH_WIDTH)
    b_c = sink_gqa_context(gq_c.reshape(b, nc, GQA_Q_HEADS, HEAD_DIM), gk_ctx, gv_ctx, lp['gqa_sink']).reshape(b, nc, BRANCH_WIDTH)
    c_c = s5_output(uc, fwd_c, bwd_c, lp['s5_c_re'], lp['s5_c_im'], lp['s5_d'], lp['s5_w_glu'], lp['s5_b_glu'], hc.dtype)
    d_c = chunk_gmlp(z_c, lp['gmlp_norm'], lp['gmlp_w_s'], lp['gmlp_b_s'])
    yc = merge_branches((a_c, b_c, c_c, d_c), gate_c, lp['w_branch'], lp['w_out'])
    return yl, yc


def mod_vec(mod, i):
    return mod[:, i][:, None, :]


def modulated_norm(xs, gain, mod, s):
    return rms_norm(xs, gain) * (1.0 + mod_vec(mod, 3 * s + 1)) + mod_vec(mod, 3 * s)


def gated_residual(xs, y, gain, mod, s, weight):
    return xs + weight * mod_vec(mod, 3 * s + 2) * rms_norm(y, gain)


def ffn_sublayer(xs, mod, s, g_pre, g_post, w_in_f, w_out_f):
    a, g = jnp.split(modulated_norm(xs, g_pre, mod, s) @ w_in_f, 2, axis=-1)
    return gated_residual(xs, (jax.nn.silu(a) * g) @ w_out_f, g_post, mod, s, MACARON_WEIGHT)


def setup_inputs(seed: int = 0) -> dict:
    key = jax.random.key(seed)
    keys = iter(jax.random.split(key, 40))

    def nrm(shape, scale):
        return scale * jax.random.normal(next(keys), shape, F32)

    def gain(shape):
        return 1.0 + nrm(shape, 0.05)

    hg, p, g = S5_GROUP, S5_STATE, S5_GROUPS
    lam_im0 = jnp.broadcast_to(jnp.pi * jnp.arange(p, dtype=F32), (DEPTH, 2, g, p))
    return {
        'x': nrm((BATCH, SEQ, D_MODEL), 1.0),
        'c': nrm((BATCH, D_MODEL), 1.0),
        'ctx': nrm((BATCH, CTX_LEN, D_MODEL), 1.0),
        'c_ctx': nrm((D_MODEL,), 1.0),
        'w_ada': nrm((DEPTH, D_MODEL, N_MOD * D_MODEL), D_MODEL ** -0.5),
        'b_ada': nrm((DEPTH, N_MOD * D_MODEL), 0.01),
        'norm_pre': gain((DEPTH, 3, D_MODEL)),
        'norm_post': gain((DEPTH, 3, D_MODEL)),
        'w_ffn_in': nrm((DEPTH, 2, D_MODEL, 2 * D_FF), D_MODEL ** -0.5),
        'w_ffn_out': nrm((DEPTH, 2, D_FF, D_MODEL), D_FF ** -0.5),
        'w_in': nrm((DEPTH, D_MODEL, IN_COLS), D_MODEL ** -0.5),
        'mla_q_norm': gain((DEPTH, MLA_Q_LORA)),
        'mla_w_uq': nrm((DEPTH, MLA_Q_LORA, MLA_HEADS * (MLA_NOPE + MLA_ROPE)), MLA_Q_LORA ** -0.5),
        'mla_kv_norm': gain((DEPTH, MLA_KV_LORA)),
        'mla_w_ukv': nrm((DEPTH, MLA_KV_LORA, MLA_HEADS * (MLA_NOPE + MLA_V)), MLA_KV_LORA ** -0.5),
        'gqa_sink': nrm((DEPTH, GQA_Q_HEADS), 0.5),
        's5_lam_re': -0.5 + nrm((DEPTH, 2, g, p), 0.01),
        's5_lam_im': lam_im0 + nrm((DEPTH, 2, g, p), 0.01),
        's5_log_dt': jax.random.uniform(next(keys), (DEPTH, 2, g), F32, math.log(S5_DT_MIN), math.log(S5_DT_MAX)),
        's5_b_re': nrm((DEPTH, 2, g, p, hg), (2 * hg) ** -0.5),
        's5_b_im': nrm((DEPTH, 2, g, p, hg), (2 * hg) ** -0.5),
        's5_c_re': nrm((DEPTH, 2, g, hg, p), p ** -0.5),
        's5_c_im': nrm((DEPTH, 2, g, hg, p), p ** -0.5),
        's5_d': nrm((DEPTH, S5_CHANNELS), 1.0),
        's5_w_glu': nrm((DEPTH, S5_CHANNELS, 2 * S5_CHANNELS), S5_CHANNELS ** -0.5),
        's5_b_glu': nrm((DEPTH, 2 * S5_CHANNELS), 0.01),
        'gmlp_norm': gain((DEPTH, GMLP_WIDTH)),
        'gmlp_w_s': nrm((DEPTH, GMLP_GROUPS, GMLP_CHUNK, GMLP_CHUNK), 0.5 * GMLP_CHUNK ** -0.5),
        'gmlp_b_s': 1.0 + nrm((DEPTH, GMLP_GROUPS, GMLP_CHUNK), 0.01),
        'w_branch': nrm((DEPTH, N_BRANCH, BRANCH_WIDTH, D_MODEL), BRANCH_WIDTH ** -0.5),
        'w_out': nrm((DEPTH, D_MODEL, D_MODEL), D_MODEL ** -0.5),
    }


def reference(x, c, ctx, c_ctx, w_ada, b_ada, norm_pre, norm_post, w_ffn_in, w_ffn_out, w_in,
              mla_q_norm, mla_w_uq, mla_kv_norm, mla_w_ukv, gqa_sink,
              s5_lam_re, s5_lam_im, s5_log_dt, s5_b_re, s5_b_im, s5_c_re, s5_c_im, s5_d, s5_w_glu, s5_b_glu,
              gmlp_norm, gmlp_w_s, gmlp_b_s, w_branch, w_out):
    b, n, _ = x.shape
    rows = n // GRID_W
    tab_mla = axial_rope_tables(rows, MLA_ROPE)
    tab_gqa = axial_rope_tables(rows, HEAD_DIM)
    silu_c = jax.nn.silu(c)
    silu_cc = jax.nn.silu(c_ctx)[None, :]
    xl, xc = x, ctx
    for l in range(DEPTH):
        last = l == DEPTH - 1
        mod_l = (silu_c @ w_ada[l] + b_ada[l]).reshape(b, N_MOD, D_MODEL)
        n_mod_c = MOD_CTX_LAST if last else N_MOD
        mod_c = (silu_cc @ w_ada[l][:, :n_mod_c * D_MODEL] + b_ada[l][:n_mod_c * D_MODEL]).reshape(1, n_mod_c, D_MODEL)
        lp = dict(w_in=w_in[l], mla_q_norm=mla_q_norm[l], mla_w_uq=mla_w_uq[l],
                  mla_kv_norm=mla_kv_norm[l], mla_w_ukv=mla_w_ukv[l], gqa_sink=gqa_sink[l],
                  s5_lam_re=s5_lam_re[l], s5_lam_im=s5_lam_im[l], s5_log_dt=s5_log_dt[l],
                  s5_b_re=s5_b_re[l], s5_b_im=s5_b_im[l], s5_c_re=s5_c_re[l], s5_c_im=s5_c_im[l],
                  s5_d=s5_d[l], s5_w_glu=s5_w_glu[l], s5_b_glu=s5_b_glu[l],
                  gmlp_norm=gmlp_norm[l], gmlp_w_s=gmlp_w_s[l], gmlp_b_s=gmlp_b_s[l],
                  w_branch=w_branch[l], w_out=w_out[l])
        xl = ffn_sublayer(xl, mod_l, 0, norm_pre[l, 0], norm_post[l, 0], w_ffn_in[l, 0], w_ffn_out[l, 0])
        xc = ffn_sublayer(xc, mod_c, 0, norm_pre[l, 0], norm_post[l, 0], w_ffn_in[l, 0], w_ffn_out[l, 0])
        hl = modulated_norm(xl, norm_pre[l, 1], mod_l, 1)
        hc = modulated_norm(xc, norm_pre[l, 1], mod_c, 1)
        yl, yc = token_mix(hl, hc, tab_mla, tab_gqa, lp, not last)
        xl = gated_residual(xl, yl, norm_post[l, 1], mod_l, 1, 1.0)
        xl = ffn_sublayer(xl, mod_l, 2, norm_pre[l, 2], norm_post[l, 2], w_ffn_in[l, 1], w_ffn_out[l, 1])
        if not last:
            xc = gated_residual(xc, yc, norm_post[l, 1], mod_c, 1, 1.0)
            xc = ffn_sublayer(xc, mod_c, 2, norm_pre[l, 2], norm_post[l, 2], w_ffn_in[l, 1], w_ffn_out[l, 1])
    return xl
```

```python
import functools
import math

import jax
import jax.numpy as jnp
from jax import lax
from jax.experimental import pallas as pl
from jax.experimental.pallas import tpu as pltpu

F32 = jnp.float32
BF16 = jnp.bfloat16

GRID_W = 64
HEAD_DIM = 64
ROPE_THETA = 10000.0
RMS_EPS = 1e-6
LN_EPS = 1e-5
NEG_INF = -1e30
MLA_HEADS = 4
MLA_Q_LORA = 256
MLA_KV_LORA = 128
MLA_NOPE = 64
MLA_ROPE = 32
MLA_V = 64
GQA_Q_HEADS = 4
GQA_KV_HEADS = 2
WINDOW = 128
W_BLOCK = 128
S5_CHANNELS = 256
S5_GROUP = 16
S5_GROUPS = S5_CHANNELS // S5_GROUP
S5_STATE = 64
GMLP_WIDTH = 256
GMLP_CHUNK = 128
GMLP_GROUPS = 4
N_BRANCH = 4
BRANCH_WIDTH = 256
N_MOD = 9

LANES = 128
SUBLANES = 8
VMEM_LIMIT_BYTES = 56 * 1024 * 1024

ROW_TILE = 256
MLA_Q_TILE = 256
S5_CHUNK = 64
MOD_ROWS = 16
MLA_SLOT = LANES
S5_LANES = S5_GROUPS * S5_STATE

PC_Z = 0
PC_A = PC_Z + 2 * GMLP_WIDTH
PC_U = PC_A + 256
PC_QL = PC_U + S5_CHANNELS
PC_GQ = PC_QL + MLA_Q_LORA
PC_GQR = PC_GQ + GQA_Q_HEADS * HEAD_DIM
PC_GK = PC_GQR + GQA_Q_HEADS * HEAD_DIM
PC_GKR = PC_GK + GQA_KV_HEADS * HEAD_DIM
PC_GV = PC_GKR + GQA_KV_HEADS * HEAD_DIM
PC_END = PC_GV + GQA_KV_HEADS * HEAD_DIM

TC_GC, TC_GS, TC_MQC, TC_MQS, TC_MKT, TC_END = 0, 128, 256, 384, 512, 640

MLA_K_COLS = MLA_HEADS * MLA_SLOT
MLA_KV_COLS = MLA_K_COLS + MLA_HEADS * MLA_V


def _cparams(sem):
    return pltpu.CompilerParams(dimension_semantics=sem, vmem_limit_bytes=VMEM_LIMIT_BYTES)


def _const_spec(shape, index_map):
    return pl.BlockSpec(shape, index_map, pipeline_mode=pl.Buffered(1))


def _rms(x, gain):
    return x * lax.rsqrt(jnp.mean(x * x, axis=-1, keepdims=True) + RMS_EPS) * gain


def _dot(a, b):
    return jnp.dot(a, b, preferred_element_type=F32)


def _dot_nt(a, b):
    return lax.dot_general(a, b, (((1,), (1,)), ((), ())), preferred_element_type=F32)


def _ada_kernel(cs_ref, w_ref, b_ref, o_ref):
    cs = cs_ref[...]
    s = (cs * jax.nn.sigmoid(cs)).astype(BF16)
    o_ref[...] = _dot(s, w_ref[...].astype(BF16)) + b_ref[...]


def _ada_call(cs, w_ada, b_ada3):
    depth, d, n = w_ada.shape
    tn = 1024
    return pl.pallas_call(
        _ada_kernel,
        grid=(depth, n // tn),
        in_specs=[
            pl.BlockSpec((MOD_ROWS, d), lambda l, j: (0, 0)),
            pl.BlockSpec((None, d, tn), lambda l, j: (l, 0, j)),
            pl.BlockSpec((None, 1, tn), lambda l, j: (l, 0, j)),
        ],
        out_specs=pl.BlockSpec((None, MOD_ROWS, tn), lambda l, j: (l, 0, j)),
        out_shape=jax.ShapeDtypeStruct((depth, MOD_ROWS, n), F32),
        compiler_params=_cparams(("parallel", "parallel")),
        name="ada_mod",
    )(cs, w_ada, b_ada3)


def _s5_disc_kernel(lre_ref, lim_ref, ldt_ref, bre_ref, bim_ref, are_ref, aim_ref, bbre_ref, bbim_ref):
    lam_re = jnp.minimum(lre_ref[...], -1e-4)
    lam_im = lim_ref[...]
    dt = jnp.exp(ldt_ref[...])
    mag = jnp.exp(lam_re * dt)
    a_re = mag * jnp.cos(lam_im * dt)
    a_im = mag * jnp.sin(lam_im * dt)
    nr, ni = a_re - 1.0, a_im
    den = lam_re * lam_re + lam_im * lam_im
    coef_re = (nr * lam_re + ni * lam_im) / den
    coef_im = (ni * lam_re - nr * lam_im) / den
    b_re, b_im = bre_ref[...], bim_ref[...]
    are_ref[...] = a_re
    aim_ref[...] = a_im
    bbre_ref[...] = coef_re * b_re - coef_im * b_im
    bbim_ref[...] = coef_re * b_im + coef_im * b_re


def _s5_disc_call(lre, lim, ldt, bre, bim):
    shp = jax.ShapeDtypeStruct(lre.shape, F32)
    return pl.pallas_call(_s5_disc_kernel, out_shape=(shp, shp, shp, shp), name="s5_disc")(lre, lim, ldt, bre, bim)


class _Rows:
    def __init__(self, b, l, c, tm):
        assert l % tm == 0 and c % tm == 0
        self.b, self.l, self.c, self.tm = b, l, c, tm
        self.bpl = l // tm
        self.bpc = c // tm
        self.nb_lat = b * self.bpl
        self.nb_all = self.nb_lat + b * self.bpc

    def is_ctx(self, i):
        return i >= self.nb_lat

    def batch(self, i):
        return jnp.where(self.is_ctx(i), (i - self.nb_lat) // self.bpc, i // self.bpl)

    def tblock(self, i):
        return jnp.where(self.is_ctx(i), (i - self.nb_lat) % self.bpc, i % self.bpl)

    def mod_row(self, i):
        return jnp.where(self.is_ctx(i), self.b, i // self.bpl)

    def table_block(self, i):
        return jnp.where(self.is_ctx(i), self.bpl, i % self.bpl)

    def tb_row_block(self, i):
        return jnp.where(self.is_ctx(i), self.tblock(i), self.bpc + i % self.bpl)


def _ffn_kernel(x_ref, mod_ref, gpre_ref, gpost_ref, wi_ref, wo_ref, o_ref, *, s, d_ff, fc):
    x = x_ref[...]
    m = mod_ref[...]
    shift, scale, gate = m[3 * s:3 * s + 1], m[3 * s + 1:3 * s + 2], m[3 * s + 2:3 * s + 3]
    h = (_rms(x, gpre_ref[s:s + 1]) * (1.0 + scale) + shift).astype(BF16)
    y = None
    for j in range(d_ff // fc):
        a = _dot(h, wi_ref[:, j * fc:(j + 1) * fc])
        g = _dot(h, wi_ref[:, d_ff + j * fc:d_ff + (j + 1) * fc])
        act = (a * jax.nn.sigmoid(a) * g).astype(BF16)
        part = _dot(act, wo_ref[j * fc:(j + 1) * fc, :])
        y = part if y is None else y + part
    o_ref[...] = x + 0.5 * gate * _rms(y, gpost_ref[s:s + 1])


def _ffn_call(x, mods, norm_pre, norm_post, w_in_bf, w_out_bf, rows, *, layer, s, which, n_blocks):
    d = x.shape[1]
    d_ff = w_out_bf.shape[2]
    tm = rows.tm
    fc = d_ff // 2 if (d_ff // 2) % LANES == 0 else d_ff
    kern = functools.partial(_ffn_kernel, s=s, d_ff=d_ff, fc=fc)
    return pl.pallas_call(
        kern,
        grid=(n_blocks,),
        in_specs=[
            pl.BlockSpec((tm, d), lambda i: (i, 0)),
            pl.BlockSpec((None, None, N_MOD, d), lambda i: (layer, rows.mod_row(i), 0, 0)),
            _const_spec((None, 3, d), lambda i: (layer, 0, 0)),
            _const_spec((None, 3, d), lambda i: (layer, 0, 0)),
            _const_spec((None, None, d, 2 * d_ff), lambda i: (layer, which, 0, 0)),
            _const_spec((None, None, d_ff, d), lambda i: (layer, which, 0, 0)),
        ],
        out_specs=pl.BlockSpec((tm, d), lambda i: (i, 0)),
        out_shape=jax.ShapeDtypeStruct((n_blocks * tm, d), F32),
        compiler_params=_cparams(("parallel",)),
        name="ffn",
    )(x, mods, norm_pre, norm_post, w_in_bf, w_out_bf)


def _proj_kernel(x_ref, mod_ref, gpre_ref, wp_ref, tab_ref, gkvn_ref, wkv_ref, gqn_ref, wq_ref,
                 gmn_ref, ws_ref, bs_ref,
                 mq_ref, mkv_ref, gq_ref, gkv_ref, u_ref, d_ref):
    x = x_ref[...]
    m = mod_ref[...]
    h = (_rms(x, gpre_ref[1:2]) * (1.0 + m[4:5]) + m[3:4]).astype(BF16)
    p = _dot(h, wp_ref[...])
    tab = tab_ref[...]
    gc, gs = tab[:, TC_GC:TC_GS], tab[:, TC_GS:TC_MQC]
    mqc, mqs, mkt = tab[:, TC_MQC:TC_MQS], tab[:, TC_MQS:TC_MKT], tab[:, TC_MKT:TC_END]

    kvl = p[:, PC_A:PC_A + MLA_KV_LORA]
    kvn = _rms(kvl, gkvn_ref[...])
    pe = p[:, PC_A + MLA_KV_LORA:PC_U] * mkt
    a2 = jnp.concatenate([kvn, pe], axis=-1).astype(BF16)
    mkv_ref[...] = _dot(a2, wkv_ref[...]).astype(BF16)

    qn = _rms(p[:, PC_QL:PC_GQ], gqn_ref[...]).astype(BF16)
    q = _dot(qn, wq_ref[...])
    qs = [q[:, hh * MLA_SLOT:(hh + 1) * MLA_SLOT] * mqc
          + q[:, MLA_K_COLS + hh * MLA_SLOT:MLA_K_COLS + (hh + 1) * MLA_SLOT] * mqs
          for hh in range(MLA_HEADS)]
    mq_ref[...] = jnp.concatenate(qs, axis=-1).astype(BF16)

    gq = [(p[:, PC_GQ + j * LANES:PC_GQ + (j + 1) * LANES] * gc
           + p[:, PC_GQR + j * LANES:PC_GQR + (j + 1) * LANES] * gs) * (HEAD_DIM ** -0.5)
          for j in range(GQA_Q_HEADS * HEAD_DIM // LANES)]
    gq_ref[...] = jnp.concatenate(gq, axis=-1).astype(BF16)
    gk = p[:, PC_GK:PC_GKR] * gc + p[:, PC_GKR:PC_GV] * gs
    gkv_ref[...] = jnp.concatenate([gk, p[:, PC_GV:PC_END]], axis=-1).astype(BF16)

    u_ref[...] = p[:, PC_U:PC_QL]

    zz = jax.nn.gelu(p[:, PC_Z:PC_A])
    ug, v = zz[:, :GMLP_WIDTH], zz[:, GMLP_WIDTH:]
    vc = v - jnp.mean(v, axis=-1, keepdims=True)
    vn = vc * lax.rsqrt(jnp.mean(vc * vc, axis=-1, keepdims=True) + LN_EPS) * gmn_ref[...]
    group_of_lane = lax.broadcasted_iota(jnp.int32, (GMLP_CHUNK, GMLP_WIDTH), 1) // (GMLP_WIDTH // GMLP_GROUPS)
    outs = []
    for ci in range(x.shape[0] // GMLP_CHUNK):
        r0 = ci * GMLP_CHUNK
        vck = vn[r0:r0 + GMLP_CHUNK]
        vbd = jnp.concatenate([jnp.where(group_of_lane == g, vck, 0.0) for g in range(GMLP_GROUPS)], axis=0)
        mixed = _dot(ws_ref[...], vbd.astype(BF16)) + bs_ref[...]
        outs.append(ug[r0:r0 + GMLP_CHUNK] * mixed)
    d_ref[...] = jnp.concatenate(outs, axis=0).astype(BF16)


def _proj_call(x, mods, norm_pre, wp_bf, tab, mla_kv_norm3, wkv_bf, mla_q_norm3, wq_bf, gmlp_norm3, ws_bf, bs_f,
               rows, *, layer):
    d = x.shape[1]
    tm = rows.tm
    t_all = rows.nb_all * tm
    b = rows.b

    def row(i):
        return (i, 0)

    def lyr3(i):
        return (layer, 0, 0)

    return pl.pallas_call(
        _proj_kernel,
        grid=(rows.nb_all,),
        in_specs=[
            pl.BlockSpec((tm, d), row),
            pl.BlockSpec((None, None, N_MOD, d), lambda i: (layer, rows.mod_row(i), 0, 0)),
            _const_spec((None, 3, d), lyr3),
            _const_spec((None, d, PC_END), lyr3),
            pl.BlockSpec((tm, TC_END), lambda i: (rows.table_block(i), 0)),
            _const_spec((None, 1, MLA_KV_LORA), lyr3),
            _const_spec((None, 256, MLA_KV_COLS), lyr3),
            _const_spec((None, 1, MLA_Q_LORA), lyr3),
            _const_spec((None, MLA_Q_LORA, 2 * MLA_K_COLS), lyr3),
            _const_spec((None, 1, GMLP_WIDTH), lyr3),
            _const_spec((None, GMLP_CHUNK, GMLP_GROUPS * GMLP_CHUNK), lyr3),
            _const_spec((None, GMLP_CHUNK, GMLP_WIDTH), lyr3),
        ],
        out_specs=[
            pl.BlockSpec((tm, MLA_K_COLS), row),
            pl.BlockSpec((tm, MLA_KV_COLS), row),
            pl.BlockSpec((tm, GQA_Q_HEADS * HEAD_DIM), row),
            pl.BlockSpec((tm, 2 * GQA_KV_HEADS * HEAD_DIM), row),
            pl.BlockSpec((tm, S5_CHANNELS), lambda i: (rows.tb_row_block(i), rows.batch(i))),
            pl.BlockSpec((tm, GMLP_WIDTH), row),
        ],
        out_shape=[
            jax.ShapeDtypeStruct((t_all, MLA_K_COLS), BF16),
            jax.ShapeDtypeStruct((t_all, MLA_KV_COLS), BF16),
            jax.ShapeDtypeStruct((t_all, GQA_Q_HEADS * HEAD_DIM), BF16),
            jax.ShapeDtypeStruct((t_all, 2 * GQA_KV_HEADS * HEAD_DIM), BF16),
            jax.ShapeDtypeStruct((rows.c + rows.l, b * S5_CHANNELS), F32),
            jax.ShapeDtypeStruct((t_all, GMLP_WIDTH), BF16),
        ],
        compiler_params=_cparams(("parallel",)),
        name="in_proj",
    )(x, mods, norm_pre, wp_bf, tab, mla_kv_norm3, wkv_bf, mla_q_norm3, wq_bf, gmlp_norm3, ws_bf, bs_f)


def _mla_kernel(*refs, has_lat):
    if has_lat:
        q_ref, kvl_ref, kvc_ref, o_ref = refs
    else:
        q_ref, kvc_ref, o_ref = refs
    tq = q_ref.shape[0]
    head_of_lane = lax.broadcasted_iota(jnp.int32, (tq, MLA_HEADS * MLA_V), 1) // MLA_V
    acc = jnp.zeros((tq, MLA_HEADS * MLA_V), F32)
    vc = kvc_ref[:, MLA_K_COLS:]
    if has_lat:
        vl = kvl_ref[:, MLA_K_COLS:]
    for hh in range(MLA_HEADS):
        sl = slice(hh * MLA_SLOT, (hh + 1) * MLA_SLOT)
        qh = q_ref[:, sl]
        sc = _dot_nt(qh, kvc_ref[:, sl])
        mx = jnp.max(sc, axis=-1, keepdims=True)
        if has_lat:
            s_lat = _dot_nt(qh, kvl_ref[:, sl])
            mx = jnp.maximum(mx, jnp.max(s_lat, axis=-1, keepdims=True))
        pc = jnp.exp(sc - mx)
        den = jnp.sum(pc, axis=-1, keepdims=True)
        o = _dot(pc.astype(BF16), vc)
        if has_lat:
            p_lat = jnp.exp(s_lat - mx)
            den = den + jnp.sum(p_lat, axis=-1, keepdims=True)
            o = o + _dot(p_lat.astype(BF16), vl)
        acc = jnp.where(head_of_lane == hh, o / den, acc)
    o_ref[...] = acc.astype(BF16)


def _mla_lat_call(mq, mkv, rows):
    b, l, c = rows.b, rows.l, rows.c
    tq = MLA_Q_TILE
    nq = l // tq
    ctx0 = (b * l) // c
    return pl.pallas_call(
        functools.partial(_mla_kernel, has_lat=True),
        grid=(b, nq),
        in_specs=[
            pl.BlockSpec((tq, MLA_K_COLS), lambda bi, j: (bi * nq + j, 0)),
            pl.BlockSpec((l, MLA_KV_COLS), lambda bi, j: (bi, 0)),
            pl.BlockSpec((c, MLA_KV_COLS), lambda bi, j: (ctx0 + bi, 0)),
        ],
        out_specs=pl.BlockSpec((tq, MLA_HEADS * MLA_V), lambda bi, j: (bi * nq + j, 0)),
        out_shape=jax.ShapeDtypeStruct((b * l, MLA_HEADS * MLA_V), BF16),
        compiler_params=_cparams(("parallel", "parallel")),
        name="mla_latent",
    )(mq, mkv, mkv)


def _mla_ctx_call(mq, mkv, rows):
    b, l, c = rows.b, rows.l, rows.c
    ctx0 = (b * l) // c
    return pl.pallas_call(
        functools.partial(_mla_kernel, has_lat=False),
        grid=(b,),
        in_specs=[
            pl.BlockSpec((c, MLA_K_COLS), lambda bi: (ctx0 + bi, 0)),
            pl.BlockSpec((c, MLA_KV_COLS), lambda bi: (ctx0 + bi, 0)),
        ],
        out_specs=pl.BlockSpec((c, MLA_HEADS * MLA_V), lambda bi: (bi, 0)),
        out_shape=jax.ShapeDtypeStruct((b * c, MLA_HEADS * MLA_V), BF16),
        compiler_params=_cparams(("parallel",)),
        name="mla_context",
    )(mq, mkv)


def _gqa_kernel(*refs, has_band, seq_len):
    if has_band:
        q_ref, kvl_ref, kvc_ref, sink_ref, o_ref = refs
    else:
        q_ref, kvc_ref, sink_ref, o_ref = refs
    tq = q_ref.shape[0]
    kw = GQA_KV_HEADS * HEAD_DIM
    lo = lax.broadcasted_iota(jnp.int32, (tq, kw), 1) < HEAD_DIM
    q0, q1 = q_ref[:, :kw], q_ref[:, kw:]
    zero = jnp.zeros_like(q0)
    qst = jnp.concatenate([jnp.where(lo, q0, zero), jnp.where(lo, zero, q0),
                           jnp.where(lo, q1, zero), jnp.where(lo, zero, q1)], axis=0)
    sk = jnp.concatenate([jnp.broadcast_to(sink_ref[r:r + 1, 0:1], (tq, 1)) for r in range(GQA_Q_HEADS)], axis=0)
    sc = _dot_nt(qst, kvc_ref[:, :kw])
    mx = jnp.maximum(jnp.max(sc, axis=-1, keepdims=True), sk)
    if has_band:
        n = pl.program_id(1)
        nbk = 3 * W_BLOCK
        start = pl.multiple_of(jnp.clip((n - 1) * W_BLOCK, 0, seq_len - nbk), W_BLOCK)
        kb = kvl_ref[pl.ds(start, nbk), :kw]
        vb = kvl_ref[pl.ds(start, nbk), kw:]
        qpos = n * W_BLOCK + lax.broadcasted_iota(jnp.int32, (tq, nbk), 0)
        kpos = start + lax.broadcasted_iota(jnp.int32, (tq, nbk), 1)
        valid = jnp.abs(qpos - kpos) <= WINDOW
        valid4 = jnp.concatenate([valid] * GQA_Q_HEADS, axis=0)
        sb = jnp.where(valid4, _dot_nt(qst, kb), NEG_INF)
        mx = jnp.maximum(mx, jnp.max(sb, axis=-1, keepdims=True))
    pc = jnp.exp(sc - mx)
    den = jnp.sum(pc, axis=-1, keepdims=True) + jnp.exp(sk - mx)
    o = _dot(pc.astype(BF16), kvc_ref[:, kw:])
    if has_band:
        pb = jnp.exp(sb - mx)
        den = den + jnp.sum(pb, axis=-1, keepdims=True)
        o = o + _dot(pb.astype(BF16), vb)
    o = o / den
    c0 = jnp.where(lo, o[0:tq], o[tq:2 * tq])
    c1 = jnp.where(lo, o[2 * tq:3 * tq], o[3 * tq:4 * tq])
    o_ref[...] = jnp.concatenate([c0, c1], axis=-1).astype(BF16)


def _gqa_lat_call(gq, gkv, sink_rows, rows, *, layer):
    b, l, c = rows.b, rows.l, rows.c
    nq = l // W_BLOCK
    ctx0 = (b * l) // c
    w = GQA_Q_HEADS * HEAD_DIM
    return pl.pallas_call(
        functools.partial(_gqa_kernel, has_band=True, seq_len=l),
        grid=(b, nq),
        in_specs=[
            pl.BlockSpec((W_BLOCK, w), lambda bi, j: (bi * nq + j, 0)),
            pl.BlockSpec((l, w), lambda bi, j: (bi, 0)),
            pl.BlockSpec((c, w), lambda bi, j: (ctx0 + bi, 0)),
            pl.BlockSpec((None, SUBLANES, LANES), lambda bi, j: (layer, 0, 0)),
        ],
        out_specs=pl.BlockSpec((W_BLOCK, w), lambda bi, j: (bi * nq + j, 0)),
        out_shape=jax.ShapeDtypeStruct((b * l, w), BF16),
        compiler_params=_cparams(("parallel", "parallel")),
        name="gqa_latent",
    )(gq, gkv, gkv, sink_rows)


def _gqa_ctx_call(gq, gkv, sink_rows, rows, *, layer):
    b, l, c = rows.b, rows.l, rows.c
    ctx0 = (b * l) // c
    w = GQA_Q_HEADS * HEAD_DIM
    return pl.pallas_call(
        functools.partial(_gqa_kernel, has_band=False, seq_len=l),
        grid=(b,),
        in_specs=[
            pl.BlockSpec((c, w), lambda bi: (ctx0 + bi, 0)),
            pl.BlockSpec((c, w), lambda bi: (ctx0 + bi, 0)),
            pl.BlockSpec((None, SUBLANES, LANES), lambda bi: (layer, 0, 0)),
        ],
        out_specs=pl.BlockSpec((c, w), lambda bi: (bi, 0)),
        out_shape=jax.ShapeDtypeStruct((b * c, w), BF16),
        compiler_params=_cparams(("parallel",)),
        name="gqa_context",
    )(gq, gkv, sink_rows)


def _s5_kernel(*refs, reverse, chunk, batch):
    if reverse:
        (u_ref, are_ref, aim_ref, bre_ref, bim_ref, cre_ref, ncim_ref,
         o_ref, sre, sim, st_re, st_im) = refs
    else:
        (u_ref, rb_ref, are_ref, aim_ref, bre_ref, bim_ref, cre_ref, ncim_ref, dsk_ref, wglu_ref, bglu_ref,
         o_ref, sre, sim, st_re, st_im) = refs

    @pl.when(pl.program_id(0) == 0)
    def _():
        st_re[...] = jnp.zeros_like(st_re)
        st_im[...] = jnp.zeros_like(st_im)

    u = u_ref[...]
    ub = u.astype(BF16)
    sre[...] = _dot(ub, bre_ref[...])
    sim[...] = _dot(ub, bim_ref[...])
    a_r, a_i = are_ref[...], aim_ref[...]

    def step(k, carry):
        sr, si = carry
        t = (chunk - 1 - k) if reverse else k
        off = pl.multiple_of(t * batch, batch)
        nr = a_r * sr - a_i * si + sre[pl.ds(off, batch), :]
        ni = a_r * si + a_i * sr + sim[pl.ds(off, batch), :]
        sre[pl.ds(off, batch), :] = nr
        sim[pl.ds(off, batch), :] = ni
        return nr, ni

    sr, si = lax.fori_loop(0, chunk, step, (st_re[...], st_im[...]), unroll=4)
    st_re[...] = sr
    st_im[...] = si
    r = _dot(sre[...].astype(BF16), cre_ref[...]) + _dot(sim[...].astype(BF16), ncim_ref[...])
    if reverse:
        o_ref[...] = r
    else:
        y = jax.nn.gelu(r + rb_ref[...] + dsk_ref[...] * u)
        z = _dot(y.astype(BF16), wglu_ref[...]) + bglu_ref[...]
        o_ref[...] = (z[:, :S5_CHANNELS] * jax.nn.sigmoid(z[:, S5_CHANNELS:])).astype(BF16)


def _s5_call(u_tb, rb, a_re, a_im, b_re, b_im, c_re, nc_im, d_skip, w_glu, b_glu, rows, *, layer, reverse):
    batch = rows.b
    chunk = S5_CHUNK
    rws = chunk * batch
    n_c, n_l = rows.c // chunk, rows.l // chunk
    n_all = n_c + n_l
    dirn = 1 if reverse else 0

    def blk(s):
        if reverse:
            return jnp.where(s < n_c, n_c - 1 - s, n_c + n_all - 1 - s)
        return s

    def row(s):
        return (blk(s), 0)

    def par4(s):
        return (layer, dirn, 0, 0)

    def lyr3(s):
        return (layer, 0, 0)

    in_specs = [pl.BlockSpec((rws, S5_CHANNELS), row)]
    args = [u_tb]
    if not reverse:
        in_specs.append(pl.BlockSpec((rws, S5_CHANNELS), row))
        args.append(rb)
    in_specs += [
        _const_spec((None, None, batch, S5_LANES), par4),
        _const_spec((None, None, batch, S5_LANES), par4),
        _const_spec((None, None, S5_CHANNELS, S5_LANES), par4),
        _const_spec((None, None, S5_CHANNELS, S5_LANES), par4),
        _const_spec((None, None, S5_LANES, S5_CHANNELS), par4),
        _const_spec((None, None, S5_LANES, S5_CHANNELS), par4),
    ]
    args += [a_re, a_im, b_re, b_im, c_re, nc_im]
    if not reverse:
        in_specs += [
            _const_spec((None, 1, S5_CHANNELS), lyr3),
            _const_spec((None, S5_CHANNELS, 2 * S5_CHANNELS), lyr3),
            _const_spec((None, 1, 2 * S5_CHANNELS), lyr3),
        ]
        args += [d_skip, w_glu, b_glu]
    return pl.pallas_call(
        functools.partial(_s5_kernel, reverse=reverse, chunk=chunk, batch=batch),
        grid=(n_all,),
        in_specs=in_specs,
        out_specs=pl.BlockSpec((rws, S5_CHANNELS), row),
        out_shape=jax.ShapeDtypeStruct((n_all * rws, S5_CHANNELS), F32 if reverse else BF16),
        scratch_shapes=[
            pltpu.VMEM((rws, S5_LANES), F32),
            pltpu.VMEM((rws, S5_LANES), F32),
            pltpu.VMEM((batch, S5_LANES), F32),
            pltpu.VMEM((batch, S5_LANES), F32),
        ],
        compiler_params=_cparams(("arbitrary",)),
        name="s5_bwd" if reverse else "s5_fwd",
    )(*args)


def _merge_kernel(x_ref, mod_ref, gpre_ref, gpost_ref, a_ref, b_ref, c_ref, d_ref, wg_ref, wb_ref, wo_ref, o_ref):
    x = x_ref[...]
    m = mod_ref[...]
    d = x.shape[1]
    h = (_rms(x, gpre_ref[1:2]) * (1.0 + m[4:5]) + m[3:4]).astype(BF16)
    merged = None
    for i, br in enumerate((a_ref, b_ref, c_ref, d_ref)):
        gate = jax.nn.sigmoid(_dot(h, wg_ref[:, i * d:(i + 1) * d]))
        term = gate * _dot(br[...], wb_ref[i])
        merged = term if merged is None else merged + term
    y = _dot(merged.astype(BF16), wo_ref[...])
    o_ref[...] = x + m[5:6] * _rms(y, gpost_ref[1:2])


def _merge_call(x, mods, norm_pre, norm_post, a, bq, c_tb, dd, wg_bf, wb_bf, wo_bf, rows, *, layer, n_blocks):
    d = x.shape[1]
    tm = rows.tm

    def row(i):
        return (i, 0)

    def lyr3(i):
        return (layer, 0, 0)

    return pl.pallas_call(
        _merge_kernel,
        grid=(n_blocks,),
        in_specs=[
            pl.BlockSpec((tm, d), row),
            pl.BlockSpec((None, None, N_MOD, d), lambda i: (layer, rows.mod_row(i), 0, 0)),
            _const_spec((None, 3, d), lyr3),
            _const_spec((None, 3, d), lyr3),
            pl.BlockSpec((tm, BRANCH_WIDTH), row),
            pl.BlockSpec((tm, BRANCH_WIDTH), row),
            pl.BlockSpec((tm, BRANCH_WIDTH), lambda i: (rows.tb_row_block(i), rows.batch(i))),
            pl.BlockSpec((tm, BRANCH_WIDTH), row),
            _const_spec((None, d, N_BRANCH * d), lyr3),
            _const_spec((None, N_BRANCH, BRANCH_WIDTH, d), lambda i: (layer, 0, 0, 0)),
            _const_spec((None, d, d), lyr3),
        ],
        out_specs=pl.BlockSpec((tm, d), row),
        out_shape=jax.ShapeDtypeStruct((n_blocks * tm, d), F32),
        compiler_params=_cparams(("parallel",)),
        name="merge",
    )(x, mods, norm_pre, norm_post, a, bq, c_tb, dd, wg_bf, wb_bf, wo_bf)


def _rot_cols(w):
    q = w.shape[-1] // 4
    return jnp.concatenate([-w[..., q:2 * q], w[..., 0:q], -w[..., 3 * q:4 * q], w[..., 2 * q:3 * q]], axis=-1)


def _rope_full(rows_n, rot_dim):
    axis_dim = rot_dim // 2
    inv_freq = ROPE_THETA ** (-jnp.arange(0, axis_dim, 2, dtype=F32) / axis_dim)
    row = jnp.repeat(jnp.arange(rows_n, dtype=F32), GRID_W)
    col = jnp.tile(jnp.arange(GRID_W, dtype=F32), rows_n)
    ang_r = row[:, None] * inv_freq[None, :]
    ang_c = col[:, None] * inv_freq[None, :]
    cos = jnp.concatenate([jnp.cos(ang_r), jnp.cos(ang_r), jnp.cos(ang_c), jnp.cos(ang_c)], axis=-1)
    sin = jnp.concatenate([jnp.sin(ang_r), jnp.sin(ang_r), jnp.sin(ang_c), jnp.sin(ang_c)], axis=-1)
    return cos, sin


def _rope_table(l, tm):
    cg, sg = _rope_full(l // GRID_W, HEAD_DIM)
    cm, sm = _rope_full(l // GRID_W, MLA_ROPE)
    scale = (MLA_NOPE + MLA_ROPE) ** -0.5
    pad = MLA_SLOT - MLA_NOPE - MLA_ROPE

    def build(cg, sg, cm, sm):
        n = cg.shape[0]
        ones = jnp.ones((n, MLA_NOPE), F32)
        zeros = jnp.zeros((n, MLA_NOPE), F32)
        return jnp.concatenate([
            cg, cg, sg, sg,
            scale * ones, scale * cm, zeros[:, :pad],
            zeros, scale * sm, zeros[:, :pad],
            cm, sm, zeros,
        ], axis=-1)

    lat = build(cg, sg, cm, sm)
    one_g, zero_g = jnp.ones((tm, HEAD_DIM), F32), jnp.zeros((tm, HEAD_DIM), F32)
    ctx = build(one_g, zero_g, one_g[:, :MLA_ROPE], zero_g[:, :MLA_ROPE])
    return jnp.concatenate([lat, ctx], axis=0)


def _proj_weight(w_in):
    depth, d, _ = w_in.shape
    o = 0
    kvl = w_in[..., o:o + MLA_KV_LORA]; o += MLA_KV_LORA
    kpe = w_in[..., o:o + MLA_ROPE]; o += MLA_ROPE
    gk = w_in[..., o:o + GQA_KV_HEADS * HEAD_DIM]; o += GQA_KV_HEADS * HEAD_DIM
    gv = w_in[..., o:o + GQA_KV_HEADS * HEAD_DIM]; o += GQA_KV_HEADS * HEAD_DIM
    u = w_in[..., o:o + S5_CHANNELS]; o += S5_CHANNELS
    ql = w_in[..., o:o + MLA_Q_LORA]; o += MLA_Q_LORA
    gq = w_in[..., o:o + GQA_Q_HEADS * HEAD_DIM]; o += GQA_Q_HEADS * HEAD_DIM
    z = w_in[..., o:o + 2 * GMLP_WIDTH]; o += 2 * GMLP_WIDTH
    gate = w_in[..., o:]
    a_blk = jnp.concatenate([kvl, kpe, _rot_cols(kpe), jnp.zeros((depth, d, 256 - MLA_KV_LORA - 2 * MLA_ROPE), F32)],
                            axis=-1)
    gqh = gq.reshape(depth, d, GQA_Q_HEADS, HEAD_DIM)
    gqh_rot = _rot_cols(gqh)
    order = jnp.array([0, 2, 1, 3])
    gq_m = gqh[:, :, order].reshape(depth, d, -1)
    gqr_m = gqh_rot[:, :, order].reshape(depth, d, -1)
    gkh = gk.reshape(depth, d, GQA_KV_HEADS, HEAD_DIM)
    gkr = _rot_cols(gkh).reshape(depth, d, -1)
    wp = jnp.concatenate([z, a_blk, u, ql, gq_m, gqr_m, gk, gkr, gv], axis=-1)
    return wp.astype(BF16), gate.astype(BF16)


def _mla_kv_weight(w_ukv):
    depth = w_ukv.shape[0]
    wh = w_ukv.reshape(depth, MLA_KV_LORA, MLA_HEADS, MLA_NOPE + MLA_V)
    k_nope, v = wh[..., :MLA_NOPE], wh[..., MLA_NOPE:]
    kslot = jnp.concatenate([k_nope, jnp.zeros((depth, MLA_KV_LORA, MLA_HEADS, MLA_SLOT - MLA_NOPE), F32)], axis=-1)
    top = jnp.concatenate([kslot.reshape(depth, MLA_KV_LORA, -1), v.reshape(depth, MLA_KV_LORA, -1)], axis=-1)
    eye = jnp.eye(MLA_ROPE, dtype=F32)
    pe_slot = jnp.concatenate([jnp.zeros((MLA_ROPE, MLA_NOPE), F32), eye,
                               jnp.zeros((MLA_ROPE, MLA_SLOT - MLA_NOPE - MLA_ROPE), F32)], axis=-1)
    pe_rows = jnp.concatenate([jnp.tile(pe_slot, (1, MLA_HEADS)), jnp.zeros((MLA_ROPE, MLA_HEADS * MLA_V), F32)],
                              axis=-1)
    pe_rows = jnp.broadcast_to(pe_rows, (depth,) + pe_rows.shape)
    tail = jnp.zeros((depth, 256 - MLA_KV_LORA - 2 * MLA_ROPE, MLA_KV_COLS), F32)
    return jnp.concatenate([top, pe_rows, pe_rows, tail], axis=1).astype(BF16)


def _mla_q_weight(w_uq):
    depth = w_uq.shape[0]
    wh = w_uq.reshape(depth, MLA_Q_LORA, MLA_HEADS, MLA_NOPE + MLA_ROPE)
    nope, pe = wh[..., :MLA_NOPE], wh[..., MLA_NOPE:]
    pad = jnp.zeros((depth, MLA_Q_LORA, MLA_HEADS, MLA_SLOT - MLA_NOPE - MLA_ROPE), F32)
    full = jnp.concatenate([nope, pe, pad], axis=-1).reshape(depth, MLA_Q_LORA, -1)
    rot = jnp.concatenate([jnp.zeros_like(nope), _rot_cols(pe), pad], axis=-1).reshape(depth, MLA_Q_LORA, -1)
    return jnp.concatenate([full, rot], axis=-1).astype(BF16)


def _block_diag(w):
    g = w.shape[-3]
    eye = jnp.eye(g, dtype=w.dtype)
    bd = w[..., :, :, None, :] * eye[:, None, :, None]
    return bd.reshape(w.shape[:-3] + (g * w.shape[-2], g * w.shape[-1]))


def kernel(x, c, ctx, c_ctx, w_ada, b_ada, norm_pre, norm_post, w_ffn_in, w_ffn_out, w_in, mla_q_norm, mla_w_uq,
           mla_kv_norm, mla_w_ukv, gqa_sink, s5_lam_re, s5_lam_im, s5_log_dt, s5_b_re, s5_b_im, s5_c_re, s5_c_im,
           s5_d, s5_w_glu, s5_b_glu, gmlp_norm, gmlp_w_s, gmlp_b_s, w_branch, w_out):
    b, l, d = x.shape
    cl = ctx.shape[1]
    depth = w_ada.shape[0]
    assert b == SUBLANES and b < MOD_ROWS
    assert l % GRID_W == 0 and l % MLA_Q_TILE == 0 and l >= 3 * W_BLOCK
    assert l % S5_CHUNK == 0 and cl % S5_CHUNK == 0
    rows = _Rows(b, l, cl, ROW_TILE)

    wffn_in_bf = w_ffn_in.astype(BF16)
    wffn_out_bf = w_ffn_out.astype(BF16)
    wp_bf, wg_bf = _proj_weight(w_in)
    wkv_bf = _mla_kv_weight(mla_w_ukv)
    wq_bf = _mla_q_weight(mla_w_uq)
    head_order = jnp.array([0, 2, 1, 3])
    wb1 = w_branch[:, 1].reshape(depth, GQA_Q_HEADS, HEAD_DIM, d)[:, head_order].reshape(depth, BRANCH_WIDTH, d)
    wb_bf = jnp.concatenate([w_branch[:, :1], wb1[:, None], w_branch[:, 2:]], axis=1).astype(BF16)
    wo_bf = w_out.astype(BF16)
    sink_rows = jnp.broadcast_to(
        jnp.concatenate([gqa_sink[:, head_order], jnp.zeros((depth, SUBLANES - GQA_Q_HEADS), F32)], axis=1)[:, :, None],
        (depth, SUBLANES, LANES))
    ws_bf = jnp.transpose(gmlp_w_s, (0, 2, 1, 3)).reshape(depth, GMLP_CHUNK, GMLP_GROUPS * GMLP_CHUNK).astype(BF16)
    bs_f = jnp.repeat(jnp.transpose(gmlp_b_s, (0, 2, 1)), GMLP_WIDTH // GMLP_GROUPS, axis=2)
    tab = _rope_table(l, ROW_TILE)
    mla_kv_norm3 = mla_kv_norm[:, None, :]
    mla_q_norm3 = mla_q_norm[:, None, :]
    gmlp_norm3 = gmlp_norm[:, None, :]
    s5_d3 = s5_d[:, None, :]
    s5_b_glu3 = s5_b_glu[:, None, :]
    wglu_bf = s5_w_glu.astype(BF16)

    hg, p, g = S5_GROUP, S5_STATE, S5_GROUPS
    n_par = depth * 2 * g
    rep = lambda t: jnp.repeat(t.reshape(n_par, p), hg, axis=1)
    ldt = jnp.broadcast_to(s5_log_dt[..., None], (depth, 2, g, p))
    a_re_x, a_im_x, bb_re, bb_im = _s5_disc_call(
        rep(s5_lam_re), rep(s5_lam_im), rep(ldt), s5_b_re.reshape(n_par, p * hg), s5_b_im.reshape(n_par, p * hg))
    a_re = jnp.broadcast_to(a_re_x[:, ::hg].reshape(depth, 2, 1, g * p), (depth, 2, b, g * p))
    a_im = jnp.broadcast_to(a_im_x[:, ::hg].reshape(depth, 2, 1, g * p), (depth, 2, b, g * p))
    bb_re = jnp.swapaxes(bb_re.reshape(depth, 2, g, p, hg), -1, -2)
    bb_im = jnp.swapaxes(bb_im.reshape(depth, 2, g, p, hg), -1, -2)
    s5_bre_bd = _block_diag(bb_re).astype(BF16)
    s5_bim_bd = _block_diag(bb_im).astype(BF16)
    s5_cre_bd = _block_diag(jnp.swapaxes(s5_c_re, -1, -2)).astype(BF16)
    s5_ncim_bd = _block_diag(jnp.swapaxes(-s5_c_im, -1, -2)).astype(BF16)

    cs = jnp.concatenate([c, c_ctx[None, :], jnp.zeros((MOD_ROWS - b - 1, d), F32)], axis=0)
    mods = _ada_call(cs, w_ada, b_ada[:, None, :]).reshape(depth, MOD_ROWS, N_MOD, d)

    xs = jnp.concatenate([x.reshape(b * l, d), ctx.reshape(b * cl, d)], axis=0)
    for layer in range(depth):
        last = layer == depth - 1
        xs = _ffn_call(xs, mods, norm_pre, norm_post, wffn_in_bf, wffn_out_bf, rows,
                       layer=layer, s=0, which=0, n_blocks=rows.nb_all)
        mq, mkv, gq, gkv, u_tb, dd = _proj_call(
            xs, mods, norm_pre, wp_bf, tab, mla_kv_norm3, wkv_bf, mla_q_norm3, wq_bf, gmlp_norm3, ws_bf, bs_f,
            rows, layer=layer)
        a_l = _mla_lat_call(mq, mkv, rows)
        b_l = _gqa_lat_call(gq, gkv, sink_rows, rows, layer=layer)
        u_rows = u_tb.reshape((cl + l) * b, S5_CHANNELS)
        rb = _s5_call(u_rows, None, a_re, a_im, s5_bre_bd, s5_bim_bd, s5_cre_bd, s5_ncim_bd, None, None, None,
                      rows, layer=layer, reverse=True)
        c_rows = _s5_call(u_rows, rb, a_re, a_im, s5_bre_bd, s5_bim_bd, s5_cre_bd, s5_ncim_bd, s5_d3, wglu_bf,
                          s5_b_glu3, rows, layer=layer, reverse=False)
        c_tb = c_rows.reshape(cl + l, b * S5_CHANNELS)
        if last:
            a_all, b_all, n_blocks = a_l, b_l, rows.nb_lat
        else:
            a_all = jnp.concatenate([a_l, _mla_ctx_call(mq, mkv, rows)], axis=0)
            b_all = jnp.concatenate([b_l, _gqa_ctx_call(gq, gkv, sink_rows, rows, layer=layer)], axis=0)
            n_blocks = rows.nb_all
        xs = _merge_call(xs, mods, norm_pre, norm_post, a_all, b_all, c_tb, dd, wg_bf, wb_bf, wo_bf, rows,
                         layer=layer, n_blocks=n_blocks)
        xs = _ffn_call(xs, mods, norm_pre, norm_post, wffn_in_bf, wffn_out_bf, rows,
                       layer=layer, s=2, which=1, n_blocks=n_blocks)
    return xs.reshape(b, l, d)
```

```python
import functools
import math

import jax
import jax.numpy as jnp
from jax import lax
from jax.experimental import pallas as pl
from jax.experimental.pallas import tpu as pltpu

F32 = jnp.float32
BF16 = jnp.bfloat16

GRID_W = 64
HEAD_DIM = 64
ROPE_THETA = 10000.0
RMS_EPS = 1e-6
LN_EPS = 1e-5
NEG_INF = -1e30
MLA_HEADS = 4
MLA_Q_LORA = 256
MLA_KV_LORA = 128
MLA_NOPE = 64
MLA_ROPE = 32
MLA_V = 64
GQA_Q_HEADS = 4
GQA_KV_HEADS = 2
WINDOW = 128
W_BLOCK = 128
S5_CHANNELS = 256
S5_GROUP = 16
S5_GROUPS = S5_CHANNELS // S5_GROUP
S5_STATE = 64
GMLP_WIDTH = 256
GMLP_CHUNK = 128
GMLP_GROUPS = 4
N_BRANCH = 4
BRANCH_WIDTH = 256
N_MOD = 9

LANES = 128
SUBLANES = 8
MXU_TILE = 256
VMEM_LIMIT_BYTES = 56 * 1024 * 1024

PROJ_TILE = 256
WIDE_TILE = 768
LAT_TILE = 512
MLA_Q_TILE = 256
GQA_Q_TILE = 512
S5_CHUNK = 64
MOD_ROWS = 16
MLA_SLOT = LANES
S5_LANES = S5_GROUPS * S5_STATE

PC_Z = 0
PC_A = PC_Z + 2 * GMLP_WIDTH
PC_U = PC_A + 256
PC_QL = PC_U + S5_CHANNELS
PC_GQ = PC_QL + MLA_Q_LORA
PC_GQR = PC_GQ + GQA_Q_HEADS * HEAD_DIM
PC_GK = PC_GQR + GQA_Q_HEADS * HEAD_DIM
PC_GKR = PC_GK + GQA_KV_HEADS * HEAD_DIM
PC_GV = PC_GKR + GQA_KV_HEADS * HEAD_DIM
PC_END = PC_GV + GQA_KV_HEADS * HEAD_DIM

TC_GC, TC_GS, TC_MQC, TC_MQS, TC_MKT, TC_END = 0, 128, 256, 384, 512, 640

MLA_K_COLS = MLA_HEADS * MLA_SLOT
MLA_KV_COLS = MLA_K_COLS + MLA_HEADS * MLA_V


def _cparams(sem):
    return pltpu.CompilerParams(dimension_semantics=sem, vmem_limit_bytes=VMEM_LIMIT_BYTES)


def _const_spec(shape, index_map):
    return pl.BlockSpec(shape, index_map, pipeline_mode=pl.Buffered(1))


def _rms(x, gain):
    return x * lax.rsqrt(jnp.mean(x * x, axis=-1, keepdims=True) + RMS_EPS) * gain


def _dot(a, b):
    return jnp.dot(a, b, preferred_element_type=F32)


def _dot_nt(a, b):
    return lax.dot_general(a, b, (((1,), (1,)), ((), ())), preferred_element_type=F32)


def _ada_kernel(cs_ref, w_ref, b_ref, o_ref):
    cs = cs_ref[...]
    s = (cs * jax.nn.sigmoid(cs)).astype(BF16)
    o_ref[...] = _dot(s, w_ref[...].astype(BF16)) + b_ref[...]


def _ada_call(cs, w_ada, b_ada3):
    depth, d, n = w_ada.shape
    tn = 1024
    return pl.pallas_call(
        _ada_kernel,
        grid=(depth, n // tn),
        in_specs=[
            pl.BlockSpec((MOD_ROWS, d), lambda l, j: (0, 0)),
            pl.BlockSpec((None, d, tn), lambda l, j: (l, 0, j)),
            pl.BlockSpec((None, 1, tn), lambda l, j: (l, 0, j)),
        ],
        out_specs=pl.BlockSpec((None, MOD_ROWS, tn), lambda l, j: (l, 0, j)),
        out_shape=jax.ShapeDtypeStruct((depth, MOD_ROWS, n), F32),
        compiler_params=_cparams(("parallel", "parallel")),
        name="ada_mod",
    )(cs, w_ada, b_ada3)


def _s5_disc_kernel(lre_ref, lim_ref, ldt_ref, bre_ref, bim_ref, are_ref, aim_ref, bbre_ref, bbim_ref):
    lam_re = jnp.minimum(lre_ref[...], -1e-4)
    lam_im = lim_ref[...]
    dt = jnp.exp(ldt_ref[...])
    mag = jnp.exp(lam_re * dt)
    a_re = mag * jnp.cos(lam_im * dt)
    a_im = mag * jnp.sin(lam_im * dt)
    nr, ni = a_re - 1.0, a_im
    den = lam_re * lam_re + lam_im * lam_im
    coef_re = (nr * lam_re + ni * lam_im) / den
    coef_im = (ni * lam_re - nr * lam_im) / den
    b_re, b_im = bre_ref[...], bim_ref[...]
    are_ref[...] = a_re
    aim_ref[...] = a_im
    bbre_ref[...] = coef_re * b_re - coef_im * b_im
    bbim_ref[...] = coef_re * b_im + coef_im * b_re


def _s5_disc_call(lre, lim, ldt, bre, bim):
    shp = jax.ShapeDtypeStruct(lre.shape, F32)
    return pl.pallas_call(_s5_disc_kernel, out_shape=(shp, shp, shp, shp), name="s5_disc")(lre, lim, ldt, bre, bim)


def _mod_pieces(tm, tail):
    return ((0, tm - tail, 0), (tm - tail, tm, 1)) if tail else ((0, tm, 0),)


def _wide_specs(b, tm, d, n_blk, layer, ctx_tail):
    def tail_row(bi, j):
        return jnp.where(j == n_blk - 1, b, bi) if ctx_tail else bi
    return [
        pl.BlockSpec((None, tm, d), lambda bi, j: (bi, j, 0)),
        pl.BlockSpec((None, None, N_MOD, d), lambda bi, j: (layer, bi, 0, 0)),
        pl.BlockSpec((None, None, N_MOD, d), lambda bi, j: (layer, tail_row(bi, j), 0, 0)),
    ]


def _ffn_kernel(x_ref, mod_ref, modt_ref, gpre_ref, gpost_ref, wi_ref, wo_ref, o_ref, *, s, d_ff, chunks, tail):
    tm = x_ref.shape[0]
    ms = (mod_ref[...], modt_ref[...])
    pieces = _mod_pieces(tm, tail)
    h = jnp.concatenate(
        [(_rms(x_ref[r0:r1], gpre_ref[s:s + 1]) * (1.0 + ms[k][3 * s + 1:3 * s + 2]) + ms[k][3 * s:3 * s + 1])
         .astype(BF16) for r0, r1, k in pieces], axis=0)
    y = None
    for c0, c1 in chunks:
        a = _dot(h, wi_ref[:, c0:c1])
        g = _dot(h, wi_ref[:, d_ff + c0:d_ff + c1])
        act = (a * jax.nn.sigmoid(a) * g).astype(BF16)
        part = _dot(act, wo_ref[c0:c1, :])
        y = part if y is None else y + part
    for r0, r1, k in pieces:
        o_ref[r0:r1] = x_ref[r0:r1] + 0.5 * ms[k][3 * s + 2:3 * s + 3] * _rms(y[r0:r1], gpost_ref[s:s + 1])


def _ffn_call(x3, mods, norm_pre, norm_post, w_in_bf, w_out_bf, *, layer, s, which, tm, n_blk, ctx_rows):
    b, _, d = x3.shape
    d_ff = w_out_bf.shape[2]
    half = (d_ff // MXU_TILE + 1) // 2 * MXU_TILE
    chunks = ((0, half), (half, d_ff)) if 0 < half < d_ff else ((0, d_ff),)
    kern = functools.partial(_ffn_kernel, s=s, d_ff=d_ff, chunks=chunks, tail=ctx_rows)
    return pl.pallas_call(
        kern,
        grid=(b, n_blk),
        in_specs=_wide_specs(b, tm, d, n_blk, layer, ctx_rows > 0) + [
            _const_spec((None, 3, d), lambda bi, j: (layer, 0, 0)),
            _const_spec((None, 3, d), lambda bi, j: (layer, 0, 0)),
            _const_spec((None, None, d, 2 * d_ff), lambda bi, j: (layer, which, 0, 0)),
            _const_spec((None, None, d_ff, d), lambda bi, j: (layer, which, 0, 0)),
        ],
        out_specs=pl.BlockSpec((None, tm, d), lambda bi, j: (bi, j, 0)),
        out_shape=jax.ShapeDtypeStruct((b, n_blk * tm, d), F32),
        compiler_params=_cparams(("parallel", "parallel")),
        name="ffn",
    )(x3, mods, mods, norm_pre, norm_post, w_in_bf, w_out_bf)


def _proj_kernel(x_ref, mod_ref, gpre_ref, wp_ref, tab_ref, gkvn_ref, wkv_ref, gqn_ref, wq_ref,
                 gmn_ref, ws_ref, bs_ref,
                 mq_ref, mkv_ref, gq_ref, gkv_ref, u_ref, d_ref):
    x = x_ref[...]
    m = mod_ref[...]
    h = (_rms(x, gpre_ref[1:2]) * (1.0 + m[4:5]) + m[3:4]).astype(BF16)
    p = _dot(h, wp_ref[...])
    tab = tab_ref[...]
    gc, gs = tab[:, TC_GC:TC_GS], tab[:, TC_GS:TC_MQC]
    mqc, mqs, mkt = tab[:, TC_MQC:TC_MQS], tab[:, TC_MQS:TC_MKT], tab[:, TC_MKT:TC_END]

    kvl = p[:, PC_A:PC_A + MLA_KV_LORA]
    kvn = _rms(kvl, gkvn_ref[...])
    pe = p[:, PC_A + MLA_KV_LORA:PC_U] * mkt
    a2 = jnp.concatenate([kvn, pe], axis=-1).astype(BF16)
    mkv_ref[...] = _dot(a2, wkv_ref[...]).astype(BF16)

    qn = _rms(p[:, PC_QL:PC_GQ], gqn_ref[...]).astype(BF16)
    q = _dot(qn, wq_ref[...])
    qs = [q[:, hh * MLA_SLOT:(hh + 1) * MLA_SLOT] * mqc
          + q[:, MLA_K_COLS + hh * MLA_SLOT:MLA_K_COLS + (hh + 1) * MLA_SLOT] * mqs
          for hh in range(MLA_HEADS)]
    mq_ref[...] = jnp.concatenate(qs, axis=-1).astype(BF16)

    gq = [(p[:, PC_GQ + j * LANES:PC_GQ + (j + 1) * LANES] * gc
           + p[:, PC_GQR + j * LANES:PC_GQR + (j + 1) * LANES] * gs) * (HEAD_DIM ** -0.5)
          for j in range(GQA_Q_HEADS * HEAD_DIM // LANES)]
    gq_ref[...] = jnp.concatenate(gq, axis=-1).astype(BF16)
    gk = p[:, PC_GK:PC_GKR] * gc + p[:, PC_GKR:PC_GV] * gs
    gkv_ref[...] = jnp.concatenate([gk, p[:, PC_GV:PC_END]], axis=-1).astype(BF16)

    u_ref[...] = p[:, PC_U:PC_QL].astype(BF16)

    zz = jax.nn.gelu(p[:, PC_Z:PC_A])
    ug, v = zz[:, :GMLP_WIDTH], zz[:, GMLP_WIDTH:]
    vc = v - jnp.mean(v, axis=-1, keepdims=True)
    vn = vc * lax.rsqrt(jnp.mean(vc * vc, axis=-1, keepdims=True) + LN_EPS) * gmn_ref[...]
    group_of_lane = lax.broadcasted_iota(jnp.int32, (GMLP_CHUNK, GMLP_WIDTH), 1) // (GMLP_WIDTH // GMLP_GROUPS)
    outs = []
    for ci in range(x.shape[0] // GMLP_CHUNK):
        r0 = ci * GMLP_CHUNK
        vck = vn[r0:r0 + GMLP_CHUNK]
        vbd = jnp.concatenate([jnp.where(group_of_lane == g, vck, 0.0) for g in range(GMLP_GROUPS)], axis=0)
        mixed = _dot(ws_ref[...], vbd.astype(BF16)) + bs_ref[...]
        outs.append(ug[r0:r0 + GMLP_CHUNK] * mixed)
    d_ref[...] = jnp.concatenate(outs, axis=0).astype(BF16)


def _proj_call(x3, mods, norm_pre, wp_bf, tab, mla_kv_norm3, wkv_bf, mla_q_norm3, wq_bf, gmlp_norm3, ws_bf, bs_f,
               *, layer, lat_rows):
    b, r, d = x3.shape
    tm = PROJ_TILE
    n_lat = lat_rows // tm

    def row(bi, j):
        return (bi, j, 0)

    def lyr3(bi, j):
        return (layer, 0, 0)

    widths = (MLA_K_COLS, MLA_KV_COLS, GQA_Q_HEADS * HEAD_DIM, 2 * GQA_KV_HEADS * HEAD_DIM, S5_CHANNELS, GMLP_WIDTH)
    return pl.pallas_call(
        _proj_kernel,
        grid=(b, r // tm),
        in_specs=[
            pl.BlockSpec((None, tm, d), row),
            pl.BlockSpec((None, None, N_MOD, d), lambda bi, j: (layer, jnp.where(j >= n_lat, b, bi), 0, 0)),
            _const_spec((None, 3, d), lyr3),
            _const_spec((None, d, PC_END), lyr3),
            pl.BlockSpec((tm, TC_END), lambda bi, j: (jnp.minimum(j, n_lat), 0)),
            _const_spec((None, 1, MLA_KV_LORA), lyr3),
            _const_spec((None, 256, MLA_KV_COLS), lyr3),
            _const_spec((None, 1, MLA_Q_LORA), lyr3),
            _const_spec((None, MLA_Q_LORA, 2 * MLA_K_COLS), lyr3),
            _const_spec((None, 1, GMLP_WIDTH), lyr3),
            _const_spec((None, GMLP_CHUNK, GMLP_GROUPS * GMLP_CHUNK), lyr3),
            _const_spec((None, GMLP_CHUNK, GMLP_WIDTH), lyr3),
        ],
        out_specs=[pl.BlockSpec((None, tm, w), row) for w in widths],
        out_shape=[jax.ShapeDtypeStruct((b, r, w), BF16) for w in widths],
        compiler_params=_cparams(("parallel", "parallel")),
        name="in_proj",
    )(x3, mods, norm_pre, wp_bf, tab, mla_kv_norm3, wkv_bf, mla_q_norm3, wq_bf, gmlp_norm3, ws_bf, bs_f)


def _mla_kernel(q_ref, kv_ref, *rest):
    o_ref = rest[-1]
    tq = q_ref.shape[0]
    head_of_lane = lax.broadcasted_iota(jnp.int32, (tq, MLA_HEADS * MLA_V), 1) // MLA_V
    acc = jnp.zeros((tq, MLA_HEADS * MLA_V), F32)
    v = kv_ref[:, MLA_K_COLS:]
    sls = [slice(hh * MLA_SLOT, (hh + 1) * MLA_SLOT) for hh in range(MLA_HEADS)]
    scs = [_dot_nt(q_ref[:, sl], kv_ref[:, sl]) for sl in sls]
    ps = [jnp.exp(sc - jnp.max(sc, axis=-1, keepdims=True)) for sc in scs]
    dens = [jnp.sum(p, axis=-1, keepdims=True) for p in ps]
    os_ = [_dot(p.astype(BF16), v) for p in ps]
    for hh in range(MLA_HEADS):
        acc = jnp.where(head_of_lane == hh, os_[hh] / dens[hh], acc)
    o_ref[...] = acc.astype(BF16)


def _mla_lat_call(mq, mkv, *, lat_rows):
    b, r, _ = mq.shape
    tq = MLA_Q_TILE
    w = MLA_HEADS * MLA_V
    return pl.pallas_call(
        _mla_kernel,
        grid=(b, lat_rows // tq),
        in_specs=[
            pl.BlockSpec((None, tq, MLA_K_COLS), lambda bi, j: (bi, j, 0)),
            pl.BlockSpec((None, r, MLA_KV_COLS), lambda bi, j: (bi, 0, 0)),
        ],
        out_specs=pl.BlockSpec((None, tq, w), lambda bi, j: (bi, j, 0)),
        out_shape=jax.ShapeDtypeStruct((b, r, w), BF16),
        compiler_params=_cparams(("parallel", "parallel")),
        name="mla_latent",
    )(mq, mkv)


def _mla_ctx_call(mq, mkv, a3, *, lat_rows):
    b, r, _ = mq.shape
    c = r - lat_rows
    blk = lat_rows // c
    w = MLA_HEADS * MLA_V
    return pl.pallas_call(
        _mla_kernel,
        grid=(b,),
        in_specs=[
            pl.BlockSpec((None, c, MLA_K_COLS), lambda bi: (bi, blk, 0)),
            pl.BlockSpec((None, c, MLA_KV_COLS), lambda bi: (bi, blk, 0)),
            pl.BlockSpec(memory_space=pl.ANY),
        ],
        out_specs=pl.BlockSpec((None, c, w), lambda bi: (bi, blk, 0)),
        out_shape=jax.ShapeDtypeStruct((b, r, w), BF16),
        input_output_aliases={2: 0},
        compiler_params=_cparams(("parallel",)),
        name="mla_context",
    )(mq, mkv, a3)


def _gqa_blocks(qs, kc, vc, sink_ref, bands):
    tq = qs[0].shape[0]
    kw = GQA_KV_HEADS * HEAD_DIM
    lo = lax.broadcasted_iota(jnp.int32, (tq, kw), 1) < HEAD_DIM
    sk = jnp.concatenate([jnp.broadcast_to(sink_ref[r:r + 1, 0:1], (tq, 1)) for r in range(GQA_Q_HEADS)], axis=0)
    qsts = []
    for q in qs:
        q0, q1 = q[:, :kw], q[:, kw:]
        zero = jnp.zeros_like(q0)
        qsts.append(jnp.concatenate([jnp.where(lo, q0, zero), jnp.where(lo, zero, q0),
                                     jnp.where(lo, q1, zero), jnp.where(lo, zero, q1)], axis=0))
    scs = [_dot_nt(qst, kc) for qst in qsts]
    mxs = [jnp.maximum(jnp.max(sc, axis=-1, keepdims=True), sk) for sc in scs]
    if bands is not None:
        sbs = [jnp.where(jnp.concatenate([valid] * GQA_Q_HEADS, axis=0), _dot_nt(qst, kb), NEG_INF)
               for qst, (kb, _, valid) in zip(qsts, bands)]
        mxs = [jnp.maximum(mx, jnp.max(sb, axis=-1, keepdims=True)) for mx, sb in zip(mxs, sbs)]
    pcs = [jnp.exp(sc - mx) for sc, mx in zip(scs, mxs)]
    dens = [jnp.sum(pc, axis=-1, keepdims=True) + jnp.exp(sk - mx) for pc, mx in zip(pcs, mxs)]
    os_ = [_dot(pc.astype(BF16), vc) for pc in pcs]
    if bands is not None:
        pbs = [jnp.exp(sb - mx) for sb, mx in zip(sbs, mxs)]
        dens = [den + jnp.sum(pb, axis=-1, keepdims=True) for den, pb in zip(dens, pbs)]
        os_ = [o + _dot(pb.astype(BF16), vb) for o, pb, (_, vb, _) in zip(os_, pbs, bands)]
    outs = []
    for o, den in zip(os_, dens):
        o = o / den
        c0 = jnp.where(lo, o[0:tq], o[tq:2 * tq])
        c1 = jnp.where(lo, o[2 * tq:3 * tq], o[3 * tq:4 * tq])
        outs.append(jnp.concatenate([c0, c1], axis=-1))
    return outs


def _gqa_lat_kernel(q_ref, kv_ref, sink_ref, o_ref, *, lat_rows):
    kw = GQA_KV_HEADS * HEAD_DIM
    kc, vc = kv_ref[lat_rows:, :kw], kv_ref[lat_rows:, kw:]
    nbk = 3 * W_BLOCK
    n_in = q_ref.shape[0] // W_BLOCK
    qs, bands = [], []
    for i in range(n_in):
        n = pl.program_id(1) * n_in + i
        start = pl.multiple_of(jnp.clip((n - 1) * W_BLOCK, 0, lat_rows - nbk), W_BLOCK)
        kb = kv_ref[pl.ds(start, nbk), :kw]
        vb = kv_ref[pl.ds(start, nbk), kw:]
        qpos = n * W_BLOCK + lax.broadcasted_iota(jnp.int32, (W_BLOCK, nbk), 0)
        kpos = start + lax.broadcasted_iota(jnp.int32, (W_BLOCK, nbk), 1)
        bands.append((kb, vb, jnp.abs(qpos - kpos) <= WINDOW))
        qs.append(q_ref[i * W_BLOCK:(i + 1) * W_BLOCK])
    outs = _gqa_blocks(qs, kc, vc, sink_ref, bands)
    o_ref[...] = jnp.concatenate(outs, axis=0).astype(BF16)


def _gqa_ctx_kernel(q_ref, kv_ref, sink_ref, b3_ref, o_ref):
    kw = GQA_KV_HEADS * HEAD_DIM
    tq = q_ref.shape[0] // 2
    outs = _gqa_blocks([q_ref[:tq], q_ref[tq:]], kv_ref[:, :kw], kv_ref[:, kw:], sink_ref, None)
    o_ref[...] = jnp.concatenate(outs, axis=0).astype(BF16)


def _gqa_lat_call(gq, gkv, sink_rows, *, layer, lat_rows):
    b, r, w = gq.shape
    tq = GQA_Q_TILE
    return pl.pallas_call(
        functools.partial(_gqa_lat_kernel, lat_rows=lat_rows),
        grid=(b, lat_rows // tq),
        in_specs=[
            pl.BlockSpec((None, tq, w), lambda bi, j: (bi, j, 0)),
            pl.BlockSpec((None, r, w), lambda bi, j: (bi, 0, 0)),
            pl.BlockSpec((None, SUBLANES, LANES), lambda bi, j: (layer, 0, 0)),
        ],
        out_specs=pl.BlockSpec((None, tq, w), lambda bi, j: (bi, j, 0)),
        out_shape=jax.ShapeDtypeStruct((b, r, w), BF16),
        compiler_params=_cparams(("parallel", "parallel")),
        name="gqa_latent",
    )(gq, gkv, sink_rows)


def _gqa_ctx_call(gq, gkv, sink_rows, b3, *, layer, lat_rows):
    b, r, w = gq.shape
    c = r - lat_rows
    blk = lat_rows // c
    return pl.pallas_call(
        _gqa_ctx_kernel,
        grid=(b,),
        in_specs=[
            pl.BlockSpec((None, c, w), lambda bi: (bi, blk, 0)),
            pl.BlockSpec((None, c, w), lambda bi: (bi, blk, 0)),
            pl.BlockSpec((None, SUBLANES, LANES), lambda bi: (layer, 0, 0)),
            pl.BlockSpec(memory_space=pl.ANY),
        ],
        out_specs=pl.BlockSpec((None, c, w), lambda bi: (bi, blk, 0)),
        out_shape=jax.ShapeDtypeStruct((b, r, w), BF16),
        input_output_aliases={3: 0},
        compiler_params=_cparams(("parallel",)),
        name="gqa_context",
    )(gq, gkv, sink_rows, b3)


def _s5_kernel(*refs, reverse, chunk, batch):
    if reverse:
        (u_ref, pm_ref, are_ref, aim_ref, bre_ref, bim_ref, cre_ref, ncim_ref,
         o_ref, sre, sim, st_re, st_im) = refs
    else:
        (u_ref, rb_ref, pm_ref, pmt_ref, are_ref, aim_ref, bre_ref, bim_ref, cre_ref, ncim_ref,
         dsk_ref, wglu_ref, bglu_ref, o_ref, sre, sim, st_re, st_im) = refs
    rws = chunk * batch

    @pl.when(pl.program_id(0) == 0)
    def _():
        st_re[...] = jnp.zeros_like(st_re)
        st_im[...] = jnp.zeros_like(st_im)

    u = _dot(pm_ref[...], u_ref[...].reshape(rws, S5_CHANNELS))
    ub = u.astype(BF16)
    sre[...] = _dot(ub, bre_ref[...])
    sim[...] = _dot(ub, bim_ref[...])
    a_r, a_i = are_ref[...], aim_ref[...]

    def step(k, carry):
        sr, si = carry
        t = (chunk - 1 - k) if reverse else k
        off = pl.multiple_of(t * batch, batch)
        nr = a_r * sr - a_i * si + sre[pl.ds(off, batch), :]
        ni = a_r * si + a_i * sr + sim[pl.ds(off, batch), :]
        sre[pl.ds(off, batch), :] = nr
        sim[pl.ds(off, batch), :] = ni
        return nr, ni

    sr, si = lax.fori_loop(0, chunk, step, (st_re[...], st_im[...]), unroll=4)
    st_re[...] = sr
    st_im[...] = si
    r = _dot(sre[...].astype(BF16), cre_ref[...]) + _dot(sim[...].astype(BF16), ncim_ref[...])
    if reverse:
        o_ref[...] = r
    else:
        y = jax.nn.gelu(r + rb_ref[...] + dsk_ref[...] * u)
        z = _dot(y.astype(BF16), wglu_ref[...]) + bglu_ref[...]
        o_tb = (z[:, :S5_CHANNELS] * jax.nn.sigmoid(z[:, S5_CHANNELS:])).astype(BF16)
        o_ref[...] = _dot(pmt_ref[...], o_tb).astype(BF16).reshape(batch, chunk, S5_CHANNELS)


def _s5_call(u4, rb, pm, pmt, a_re, a_im, b_re, b_im, c_re, nc_im, d_skip, w_glu, b_glu, *, layer, reverse, lat_rows):
    batch, n_all, chunk, _ = u4.shape
    rws = chunk * batch
    n_l = lat_rows // chunk
    n_c = n_all - n_l
    dirn = 1 if reverse else 0

    def blk(s):
        if reverse:
            return n_all - 1 - s
        return jnp.where(s < n_c, n_l + s, s - n_c)

    def par4(s):
        return (layer, dirn, 0, 0)

    def lyr3(s):
        return (layer, 0, 0)

    in_specs = [pl.BlockSpec((batch, None, chunk, S5_CHANNELS), lambda s: (0, blk(s), 0, 0))]
    args = [u4]
    if not reverse:
        in_specs.append(pl.BlockSpec((rws, S5_CHANNELS), lambda s: (blk(s), 0)))
        args.append(rb)
    in_specs.append(_const_spec((rws, rws), lambda s: (0, 0)))
    args.append(pm)
    if not reverse:
        in_specs.append(_const_spec((rws, rws), lambda s: (0, 0)))
        args.append(pmt)
    in_specs += [
        _const_spec((None, None, batch, S5_LANES), par4),
        _const_spec((None, None, batch, S5_LANES), par4),
        _const_spec((None, None, S5_CHANNELS, S5_LANES), par4),
        _const_spec((None, None, S5_CHANNELS, S5_LANES), par4),
        _const_spec((None, None, S5_LANES, S5_CHANNELS), par4),
        _const_spec((None, None, S5_LANES, S5_CHANNELS), par4),
    ]
    args += [a_re, a_im, b_re, b_im, c_re, nc_im]
    if reverse:
        out_spec = pl.BlockSpec((rws, S5_CHANNELS), lambda s: (blk(s), 0))
        out_shape = jax.ShapeDtypeStruct((n_all * rws, S5_CHANNELS), F32)
    else:
        in_specs += [
            _const_spec((None, 1, S5_CHANNELS), lyr3),
            _const_spec((None, S5_CHANNELS, 2 * S5_CHANNELS), lyr3),
            _const_spec((None, 1, 2 * S5_CHANNELS), lyr3),
        ]
        args += [d_skip, w_glu, b_glu]
        out_spec = pl.BlockSpec((batch, None, chunk, S5_CHANNELS), lambda s: (0, blk(s), 0, 0))
        out_shape = jax.ShapeDtypeStruct(u4.shape, BF16)
    return pl.pallas_call(
        functools.partial(_s5_kernel, reverse=reverse, chunk=chunk, batch=batch),
        grid=(n_all,),
        in_specs=in_specs,
        out_specs=out_spec,
        out_shape=out_shape,
        scratch_shapes=[
            pltpu.VMEM((rws, S5_LANES), F32),
            pltpu.VMEM((rws, S5_LANES), F32),
            pltpu.VMEM((batch, S5_LANES), F32),
            pltpu.VMEM((batch, S5_LANES), F32),
        ],
        compiler_params=_cparams(("arbitrary",)),
        name="s5_bwd" if reverse else "s5_fwd",
    )(*args)


def _merge_kernel(x_ref, mod_ref, modt_ref, gpre_ref, gpost_ref, a_ref, b_ref, c_ref, d_ref, wg_ref, wb_ref, wo_ref,
                  o_ref, *, tail):
    tm, d = x_ref.shape
    ms = (mod_ref[...], modt_ref[...])
    pieces = _mod_pieces(tm, tail)
    h = jnp.concatenate(
        [(_rms(x_ref[r0:r1], gpre_ref[1:2]) * (1.0 + ms[k][4:5]) + ms[k][3:4]).astype(BF16) for r0, r1, k in pieces],
        axis=0)
    merged = None
    for i, br in enumerate((a_ref, b_ref, c_ref, d_ref)):
        gate = jax.nn.sigmoid(_dot(h, wg_ref[:, i * d:(i + 1) * d]))
        term = gate * _dot(br[...], wb_ref[i])
        merged = term if merged is None else merged + term
    y = _dot(merged.astype(BF16), wo_ref[...])
    for r0, r1, k in pieces:
        o_ref[r0:r1] = x_ref[r0:r1] + ms[k][5:6] * _rms(y[r0:r1], gpost_ref[1:2])


def _merge_call(x3, mods, norm_pre, norm_post, a3, b3, c3, d3, wg_bf, wb_bf, wo_bf, *, layer, tm, n_blk, ctx_rows):
    b, _, d = x3.shape

    def row(bi, j):
        return (bi, j, 0)

    def lyr3(bi, j):
        return (layer, 0, 0)

    return pl.pallas_call(
        functools.partial(_merge_kernel, tail=ctx_rows),
        grid=(b, n_blk),
        in_specs=_wide_specs(b, tm, d, n_blk, layer, ctx_rows > 0) + [
            _const_spec((None, 3, d), lyr3),
            _const_spec((None, 3, d), lyr3),
            pl.BlockSpec((None, tm, BRANCH_WIDTH), row),
            pl.BlockSpec((None, tm, BRANCH_WIDTH), row),
            pl.BlockSpec((None, tm, BRANCH_WIDTH), row),
            pl.BlockSpec((None, tm, BRANCH_WIDTH), row),
            _const_spec((None, d, N_BRANCH * d), lyr3),
            _const_spec((None, N_BRANCH, BRANCH_WIDTH, d), lambda bi, j: (layer, 0, 0, 0)),
            _const_spec((None, d, d), lyr3),
        ],
        out_specs=pl.BlockSpec((None, tm, d), row),
        out_shape=jax.ShapeDtypeStruct((b, n_blk * tm, d), F32),
        compiler_params=_cparams(("parallel", "parallel")),
        name="merge",
    )(x3, mods, mods, norm_pre, norm_post, a3, b3, c3, d3, wg_bf, wb_bf, wo_bf)


def _rot_cols(w):
    q = w.shape[-1] // 4
    return jnp.concatenate([-w[..., q:2 * q], w[..., 0:q], -w[..., 3 * q:4 * q], w[..., 2 * q:3 * q]], axis=-1)


def _rope_full(rows_n, rot_dim):
    axis_dim = rot_dim // 2
    inv_freq = ROPE_THETA ** (-jnp.arange(0, axis_dim, 2, dtype=F32) / axis_dim)
    row = jnp.repeat(jnp.arange(rows_n, dtype=F32), GRID_W)
    col = jnp.tile(jnp.arange(GRID_W, dtype=F32), rows_n)
    ang_r = row[:, None] * inv_freq[None, :]
    ang_c = col[:, None] * inv_freq[None, :]
    cos = jnp.concatenate([jnp.cos(ang_r), jnp.cos(ang_r), jnp.cos(ang_c), jnp.cos(ang_c)], axis=-1)
    sin = jnp.concatenate([jnp.sin(ang_r), jnp.sin(ang_r), jnp.sin(ang_c), jnp.sin(ang_c)], axis=-1)
    return cos, sin


def _rope_table(l, tm):
    cg, sg = _rope_full(l // GRID_W, HEAD_DIM)
    cm, sm = _rope_full(l // GRID_W, MLA_ROPE)
    scale = (MLA_NOPE + MLA_ROPE) ** -0.5
    pad = MLA_SLOT - MLA_NOPE - MLA_ROPE

    def build(cg, sg, cm, sm):
        n = cg.shape[0]
        ones = jnp.ones((n, MLA_NOPE), F32)
        zeros = jnp.zeros((n, MLA_NOPE), F32)
        return jnp.concatenate([
            cg, cg, sg, sg,
            scale * ones, scale * cm, zeros[:, :pad],
            zeros, scale * sm, zeros[:, :pad],
            cm, sm, zeros,
        ], axis=-1)

    lat = build(cg, sg, cm, sm)
    one_g, zero_g = jnp.ones((tm, HEAD_DIM), F32), jnp.zeros((tm, HEAD_DIM), F32)
    ctx = build(one_g, zero_g, one_g[:, :MLA_ROPE], zero_g[:, :MLA_ROPE])
    return jnp.concatenate([lat, ctx], axis=0)


def _proj_weight(w_in):
    depth, d, _ = w_in.shape
    o = 0
    kvl = w_in[..., o:o + MLA_KV_LORA]; o += MLA_KV_LORA
    kpe = w_in[..., o:o + MLA_ROPE]; o += MLA_ROPE
    gk = w_in[..., o:o + GQA_KV_HEADS * HEAD_DIM]; o += GQA_KV_HEADS * HEAD_DIM
    gv = w_in[..., o:o + GQA_KV_HEADS * HEAD_DIM]; o += GQA_KV_HEADS * HEAD_DIM
    u = w_in[..., o:o + S5_CHANNELS]; o += S5_CHANNELS
    ql = w_in[..., o:o + MLA_Q_LORA]; o += MLA_Q_LORA
    gq = w_in[..., o:o + GQA_Q_HEADS * HEAD_DIM]; o += GQA_Q_HEADS * HEAD_DIM
    z = w_in[..., o:o + 2 * GMLP_WIDTH]; o += 2 * GMLP_WIDTH
    gate = w_in[..., o:]
    a_blk = jnp.concatenate([kvl, kpe, _rot_cols(kpe), jnp.zeros((depth, d, 256 - MLA_KV_LORA - 2 * MLA_ROPE), F32)],
                            axis=-1)
    gqh = gq.reshape(depth, d, GQA_Q_HEADS, HEAD_DIM)
    gqh_rot = _rot_cols(gqh)
    order = jnp.array([0, 2, 1, 3])
    gq_m = gqh[:, :, order].reshape(depth, d, -1)
    gqr_m = gqh_rot[:, :, order].reshape(depth, d, -1)
    gkh = gk.reshape(depth, d, GQA_KV_HEADS, HEAD_DIM)
    gkr = _rot_cols(gkh).reshape(depth, d, -1)
    wp = jnp.concatenate([z, a_blk, u, ql, gq_m, gqr_m, gk, gkr, gv], axis=-1)
    return wp.astype(BF16), gate.astype(BF16)


def _mla_kv_weight(w_ukv):
    depth = w_ukv.shape[0]
    wh = w_ukv.reshape(depth, MLA_KV_LORA, MLA_HEADS, MLA_NOPE + MLA_V)
    k_nope, v = wh[..., :MLA_NOPE], wh[..., MLA_NOPE:]
    kslot = jnp.concatenate([k_nope, jnp.zeros((depth, MLA_KV_LORA, MLA_HEADS, MLA_SLOT - MLA_NOPE), F32)], axis=-1)
    top = jnp.concatenate([kslot.reshape(depth, MLA_KV_LORA, -1), v.reshape(depth, MLA_KV_LORA, -1)], axis=-1)
    eye = jnp.eye(MLA_ROPE, dtype=F32)
    pe_slot = jnp.concatenate([jnp.zeros((MLA_ROPE, MLA_NOPE), F32), eye,
                               jnp.zeros((MLA_ROPE, MLA_SLOT - MLA_NOPE - MLA_ROPE), F32)], axis=-1)
    pe_rows = jnp.concatenate([jnp.tile(pe_slot, (1, MLA_HEADS)), jnp.zeros((MLA_ROPE, MLA_HEADS * MLA_V), F32)],
                              axis=-1)
    pe_rows = jnp.broadcast_to(pe_rows, (depth,) + pe_rows.shape)
    tail = jnp.zeros((depth, 256 - MLA_KV_LORA - 2 * MLA_ROPE, MLA_KV_COLS), F32)
    return jnp.concatenate([top, pe_rows, pe_rows, tail], axis=1).astype(BF16)


def _mla_q_weight(w_uq):
    depth = w_uq.shape[0]
    wh = w_uq.reshape(depth, MLA_Q_LORA, MLA_HEADS, MLA_NOPE + MLA_ROPE)
    nope, pe = wh[..., :MLA_NOPE], wh[..., MLA_NOPE:]
    pad = jnp.zeros((depth, MLA_Q_LORA, MLA_HEADS, MLA_SLOT - MLA_NOPE - MLA_ROPE), F32)
    full = jnp.concatenate([nope, pe, pad], axis=-1).reshape(depth, MLA_Q_LORA, -1)
    rot = jnp.concatenate([jnp.zeros_like(nope), _rot_cols(pe), pad], axis=-1).reshape(depth, MLA_Q_LORA, -1)
    return jnp.concatenate([full, rot], axis=-1).astype(BF16)


def _block_diag(w):
    g = w.shape[-3]
    eye = jnp.eye(g, dtype=w.dtype)
    bd = w[..., :, :, None, :] * eye[:, None, :, None]
    return bd.reshape(w.shape[:-3] + (g * w.shape[-2], g * w.shape[-1]))


def kernel(x, c, ctx, c_ctx, w_ada, b_ada, norm_pre, norm_post, w_ffn_in, w_ffn_out, w_in, mla_q_norm, mla_w_uq,
           mla_kv_norm, mla_w_ukv, gqa_sink, s5_lam_re, s5_lam_im, s5_log_dt, s5_b_re, s5_b_im, s5_c_re, s5_c_im,
           s5_d, s5_w_glu, s5_b_glu, gmlp_norm, gmlp_w_s, gmlp_b_s, w_branch, w_out):
    b, l, d = x.shape
    cl = ctx.shape[1]
    r = l + cl
    depth = w_ada.shape[0]
    assert b == SUBLANES and b < MOD_ROWS
    assert l % GRID_W == 0 and l >= 3 * W_BLOCK and l % cl == 0
    assert l % PROJ_TILE == 0 and cl % PROJ_TILE == 0 and l % MLA_Q_TILE == 0 and l % GQA_Q_TILE == 0
    assert l % S5_CHUNK == 0 and cl % S5_CHUNK == 0 and l % LAT_TILE == 0
    assert r % WIDE_TILE == 0 and cl <= WIDE_TILE

    wffn_in_bf = w_ffn_in.astype(BF16)
    wffn_out_bf = w_ffn_out.astype(BF16)
    wp_bf, wg_bf = _proj_weight(w_in)
    wkv_bf = _mla_kv_weight(mla_w_ukv)
    wq_bf = _mla_q_weight(mla_w_uq)
    head_order = jnp.array([0, 2, 1, 3])
    wb1 = w_branch[:, 1].reshape(depth, GQA_Q_HEADS, HEAD_DIM, d)[:, head_order].reshape(depth, BRANCH_WIDTH, d)
    wb_bf = jnp.concatenate([w_branch[:, :1], wb1[:, None], w_branch[:, 2:]], axis=1).astype(BF16)
    wo_bf = w_out.astype(BF16)
    sink_rows = jnp.broadcast_to(
        jnp.concatenate([gqa_sink[:, head_order], jnp.zeros((depth, SUBLANES - GQA_Q_HEADS), F32)], axis=1)[:, :, None],
        (depth, SUBLANES, LANES))
    ws_bf = jnp.transpose(gmlp_w_s, (0, 2, 1, 3)).reshape(depth, GMLP_CHUNK, GMLP_GROUPS * GMLP_CHUNK).astype(BF16)
    bs_f = jnp.repeat(jnp.transpose(gmlp_b_s, (0, 2, 1)), GMLP_WIDTH // GMLP_GROUPS, axis=2)
    tab = _rope_table(l, PROJ_TILE)
    mla_kv_norm3 = mla_kv_norm[:, None, :]
    mla_q_norm3 = mla_q_norm[:, None, :]
    gmlp_norm3 = gmlp_norm[:, None, :]
    s5_d3 = s5_d[:, None, :]
    s5_b_glu3 = s5_b_glu[:, None, :]
    wglu_bf = s5_w_glu.astype(BF16)
    tb = jnp.arange(S5_CHUNK * b)
    pm = (tb[None, :] == ((tb % b) * S5_CHUNK + tb // b)[:, None]).astype(BF16)
    pmt = pm.T

    hg, p, g = S5_GROUP, S5_STATE, S5_GROUPS
    n_par = depth * 2 * g
    rep = lambda t: jnp.repeat(t.reshape(n_par, p), hg, axis=1)
    ldt = jnp.broadcast_to(s5_log_dt[..., None], (depth, 2, g, p))
    a_re_x, a_im_x, bb_re, bb_im = _s5_disc_call(
        rep(s5_lam_re), rep(s5_lam_im), rep(ldt), s5_b_re.reshape(n_par, p * hg), s5_b_im.reshape(n_par, p * hg))
    a_re = jnp.broadcast_to(a_re_x[:, ::hg].reshape(depth, 2, 1, g * p), (depth, 2, b, g * p))
    a_im = jnp.broadcast_to(a_im_x[:, ::hg].reshape(depth, 2, 1, g * p), (depth, 2, b, g * p))
    bb_re = jnp.swapaxes(bb_re.reshape(depth, 2, g, p, hg), -1, -2)
    bb_im = jnp.swapaxes(bb_im.reshape(depth, 2, g, p, hg), -1, -2)
    s5_bre_bd = _block_diag(bb_re).astype(BF16)
    s5_bim_bd = _block_diag(bb_im).astype(BF16)
    s5_cre_bd = _block_diag(jnp.swapaxes(s5_c_re, -1, -2)).astype(BF16)
    s5_ncim_bd = _block_diag(jnp.swapaxes(-s5_c_im, -1, -2)).astype(BF16)

    cs = jnp.concatenate([c, c_ctx[None, :], jnp.zeros((MOD_ROWS - b - 1, d), F32)], axis=0)
    mods = _ada_call(cs, w_ada, b_ada[:, None, :]).reshape(depth, MOD_ROWS, N_MOD, d)

    xs = jnp.concatenate([x, ctx], axis=1)
    n_wide = r // WIDE_TILE
    for layer in range(depth):
        last = layer == depth - 1
        xs = _ffn_call(xs, mods, norm_pre, norm_post, wffn_in_bf, wffn_out_bf,
                       layer=layer, s=0, which=0, tm=WIDE_TILE, n_blk=n_wide, ctx_rows=cl)
        mq, mkv, gq, gkv, u3, d3 = _proj_call(
            xs, mods, norm_pre, wp_bf, tab, mla_kv_norm3, wkv_bf, mla_q_norm3, wq_bf, gmlp_norm3, ws_bf, bs_f,
            layer=layer, lat_rows=l)
        a3 = _mla_lat_call(mq, mkv, lat_rows=l)
        b3 = _gqa_lat_call(gq, gkv, sink_rows, layer=layer, lat_rows=l)
        u4 = u3.reshape(b, r // S5_CHUNK, S5_CHUNK, S5_CHANNELS)
        rb = _s5_call(u4, None, pm, None, a_re, a_im, s5_bre_bd, s5_bim_bd, s5_cre_bd, s5_ncim_bd, None, None, None,
                      layer=layer, reverse=True, lat_rows=l)
        c3 = _s5_call(u4, rb, pm, pmt, a_re, a_im, s5_bre_bd, s5_bim_bd, s5_cre_bd, s5_ncim_bd, s5_d3, wglu_bf,
                      s5_b_glu3, layer=layer, reverse=False, lat_rows=l).reshape(b, r, S5_CHANNELS)
        if last:
            tm, n_blk, ctx_rows = LAT_TILE, l // LAT_TILE, 0
        else:
            a3 = _mla_ctx_call(mq, mkv, a3, lat_rows=l)
            b3 = _gqa_ctx_call(gq, gkv, sink_rows, b3, layer=layer, lat_rows=l)
            tm, n_blk, ctx_rows = WIDE_TILE, n_wide, cl
        xs = _merge_call(xs, mods, norm_pre, norm_post, a3, b3, c3, d3, wg_bf, wb_bf, wo_bf,
                         layer=layer, tm=tm, n_blk=n_blk, ctx_rows=ctx_rows)
        xs = _ffn_call(xs, mods, norm_pre, norm_post, wffn_in_bf, wffn_out_bf,
                       layer=layer, s=2, which=1, tm=tm, n_blk=n_blk, ctx_rows=ctx_rows)
    return xs
```

```python
import functools

import jax
import jax.numpy as jnp
import numpy as np
from jax import lax
from jax.experimental import pallas as pl
from jax.experimental.pallas import tpu as pltpu

F32 = jnp.float32
BF16 = jnp.bfloat16

GRID_W = 64
HEAD_DIM = 64
ROPE_THETA = 10000.0
RMS_EPS = 1e-6
LN_EPS = 1e-5
NEG_INF = -1e30
MLA_HEADS = 4
MLA_Q_LORA = 256
MLA_KV_LORA = 128
MLA_NOPE = 64
MLA_ROPE = 32
MLA_V = 64
GQA_Q_HEADS = 4
GQA_KV_HEADS = 2
WINDOW = 128
W_BLOCK = 128
S5_CHANNELS = 256
S5_GROUP = 16
S5_GROUPS = S5_CHANNELS // S5_GROUP
S5_STATE = 64
GMLP_WIDTH = 256
GMLP_CHUNK = 128
GMLP_GROUPS = 4
N_BRANCH = 4
BRANCH_WIDTH = 256
N_MOD = 9

LANES = 128
SUBLANES = 8
MXU_TILE = 256
VMEM_LIMIT_BYTES = 56 * 1024 * 1024

PROJ_TILE = 256
WIDE_TILE = 768
LAT_TILE = 512
MLA_Q_TILE = 256
GQA_Q_TILE = 512
S5_CHUNK = 64
MOD_ROWS = 16
MLA_SLOT = LANES
S5_LANES = S5_GROUPS * S5_STATE

PC_Z = 0
PC_A = PC_Z + 2 * GMLP_WIDTH
PC_U = PC_A + 256
PC_QL = PC_U + S5_CHANNELS
PC_GQ = PC_QL + MLA_Q_LORA
PC_GQR = PC_GQ + GQA_Q_HEADS * HEAD_DIM
PC_GK = PC_GQR + GQA_Q_HEADS * HEAD_DIM
PC_GKR = PC_GK + GQA_KV_HEADS * HEAD_DIM
PC_GV = PC_GKR + GQA_KV_HEADS * HEAD_DIM
PC_END = PC_GV + GQA_KV_HEADS * HEAD_DIM

TC_GC, TC_GS, TC_MQC, TC_MQS, TC_MKT, TC_END = 0, 128, 256, 384, 512, 640

MLA_K_COLS = MLA_HEADS * MLA_SLOT
MLA_KV_COLS = MLA_K_COLS + MLA_HEADS * MLA_V


def _cparams(sem):
    return pltpu.CompilerParams(dimension_semantics=sem, vmem_limit_bytes=VMEM_LIMIT_BYTES)


def _const_spec(shape, index_map):
    return pl.BlockSpec(shape, index_map, pipeline_mode=pl.Buffered(1))


def _rms(x, gain):
    return x * lax.rsqrt(jnp.mean(x * x, axis=-1, keepdims=True) + RMS_EPS) * gain


def _dot(a, b):
    return jnp.dot(a, b, preferred_element_type=F32)


def _dot_nt(a, b):
    return lax.dot_general(a, b, (((1,), (1,)), ((), ())), preferred_element_type=F32)


def _ada_kernel(cs_ref, w_ref, b_ref, o_ref):
    cs = cs_ref[...]
    s = (cs * jax.nn.sigmoid(cs)).astype(BF16)
    o_ref[...] = _dot(s, w_ref[...].astype(BF16)) + b_ref[...]


def _ada_call(cs, w_ada, b_ada3):
    depth, d, n = w_ada.shape
    tn = 1024
    return pl.pallas_call(
        _ada_kernel,
        grid=(depth, n // tn),
        in_specs=[
            pl.BlockSpec((MOD_ROWS, d), lambda l, j: (0, 0)),
            pl.BlockSpec((None, d, tn), lambda l, j: (l, 0, j)),
            pl.BlockSpec((None, 1, tn), lambda l, j: (l, 0, j)),
        ],
        out_specs=pl.BlockSpec((None, MOD_ROWS, tn), lambda l, j: (l, 0, j)),
        out_shape=jax.ShapeDtypeStruct((depth, MOD_ROWS, n), F32),
        compiler_params=_cparams(("parallel", "parallel")),
        name="ada_mod",
    )(cs, w_ada, b_ada3)


def _s5_disc_kernel(lre_ref, lim_ref, ldt_ref, bre_ref, bim_ref, are_ref, aim_ref, bbre_ref, bbim_ref):
    lam_re = jnp.minimum(lre_ref[...], -1e-4)
    lam_im = lim_ref[...]
    dt = jnp.exp(ldt_ref[...])
    mag = jnp.exp(lam_re * dt)
    a_re = mag * jnp.cos(lam_im * dt)
    a_im = mag * jnp.sin(lam_im * dt)
    nr, ni = a_re - 1.0, a_im
    den = lam_re * lam_re + lam_im * lam_im
    coef_re = (nr * lam_re + ni * lam_im) / den
    coef_im = (ni * lam_re - nr * lam_im) / den
    b_re, b_im = bre_ref[...], bim_ref[...]
    are_ref[...] = a_re
    aim_ref[...] = a_im
    bbre_ref[...] = coef_re * b_re - coef_im * b_im
    bbim_ref[...] = coef_re * b_im + coef_im * b_re


def _s5_disc_call(lre, lim, ldt, bre, bim):
    shp = jax.ShapeDtypeStruct(lre.shape, F32)
    return pl.pallas_call(_s5_disc_kernel, out_shape=(shp, shp, shp, shp), name="s5_disc")(lre, lim, ldt, bre, bim)


def _mod_pieces(tm, tail):
    return ((0, tm - tail, 0), (tm - tail, tm, 1)) if tail else ((0, tm, 0),)


def _wide_specs(b, tm, d, n_blk, layer, ctx_tail):
    def tail_row(bi, j):
        return jnp.where(j == n_blk - 1, b, bi) if ctx_tail else bi
    return [
        pl.BlockSpec((None, tm, d), lambda bi, j: (bi, j, 0)),
        pl.BlockSpec((None, None, N_MOD, d), lambda bi, j: (layer, bi, 0, 0)),
        pl.BlockSpec((None, None, N_MOD, d), lambda bi, j: (layer, tail_row(bi, j), 0, 0)),
    ]


def _ffn_kernel(x_ref, mod_ref, modt_ref, gpre_ref, gpost_ref, wi_ref, wo_ref, o_ref, *, s, d_ff, chunks, tail):
    tm = x_ref.shape[0]
    ms = (mod_ref[...], modt_ref[...])
    pieces = _mod_pieces(tm, tail)
    h = jnp.concatenate(
        [(_rms(x_ref[r0:r1], gpre_ref[s:s + 1]) * (1.0 + ms[k][3 * s + 1:3 * s + 2]) + ms[k][3 * s:3 * s + 1])
         .astype(BF16) for r0, r1, k in pieces], axis=0)
    y = None
    for c0, c1 in chunks:
        a = _dot(h, wi_ref[:, c0:c1])
        g = _dot(h, wi_ref[:, d_ff + c0:d_ff + c1])
        act = (a * jax.nn.sigmoid(a) * g).astype(BF16)
        part = _dot(act, wo_ref[c0:c1, :])
        y = part if y is None else y + part
    for r0, r1, k in pieces:
        o_ref[r0:r1] = x_ref[r0:r1] + 0.5 * ms[k][3 * s + 2:3 * s + 3] * _rms(y[r0:r1], gpost_ref[s:s + 1])


def _ffn_call(x3, mods, norm_pre, norm_post, w_in_bf, w_out_bf, *, layer, s, which, tm, n_blk, ctx_rows):
    b, _, d = x3.shape
    d_ff = w_out_bf.shape[2]
    half = (d_ff // MXU_TILE + 1) // 2 * MXU_TILE
    chunks = ((0, half), (half, d_ff)) if 0 < half < d_ff else ((0, d_ff),)
    kern = functools.partial(_ffn_kernel, s=s, d_ff=d_ff, chunks=chunks, tail=ctx_rows)
    return pl.pallas_call(
        kern,
        grid=(b, n_blk),
        in_specs=_wide_specs(b, tm, d, n_blk, layer, ctx_rows > 0) + [
            _const_spec((None, 3, d), lambda bi, j: (layer, 0, 0)),
            _const_spec((None, 3, d), lambda bi, j: (layer, 0, 0)),
            _const_spec((None, None, d, 2 * d_ff), lambda bi, j: (layer, which, 0, 0)),
            _const_spec((None, None, d_ff, d), lambda bi, j: (layer, which, 0, 0)),
        ],
        out_specs=pl.BlockSpec((None, tm, d), lambda bi, j: (bi, j, 0)),
        out_shape=jax.ShapeDtypeStruct((b, n_blk * tm, d), F32),
        compiler_params=_cparams(("parallel", "parallel")),
        name="ffn",
    )(x3, mods, mods, norm_pre, norm_post, w_in_bf, w_out_bf)


def _proj_kernel(x_ref, mod_ref, gpre_ref, wp_ref, tab_ref, gkvn_ref, wkv_ref, gqn_ref, wq_ref,
                 gmn_ref, ws_ref, bs_ref,
                 mq_ref, mkv_ref, gq_ref, gkv_ref, u_ref, d_ref):
    x = x_ref[...]
    m = mod_ref[...]
    h = (_rms(x, gpre_ref[1:2]) * (1.0 + m[4:5]) + m[3:4]).astype(BF16)
    p = _dot(h, wp_ref[...])
    tab = tab_ref[...]
    gc, gs = tab[:, TC_GC:TC_GS], tab[:, TC_GS:TC_MQC]
    mqc, mqs, mkt = tab[:, TC_MQC:TC_MQS], tab[:, TC_MQS:TC_MKT], tab[:, TC_MKT:TC_END]

    kvl = p[:, PC_A:PC_A + MLA_KV_LORA]
    kvn = _rms(kvl, gkvn_ref[...])
    pe = p[:, PC_A + MLA_KV_LORA:PC_U] * mkt
    a2 = jnp.concatenate([kvn, pe], axis=-1).astype(BF16)
    mkv_ref[...] = _dot(a2, wkv_ref[...]).astype(BF16)

    qn = _rms(p[:, PC_QL:PC_GQ], gqn_ref[...]).astype(BF16)
    q = _dot(qn, wq_ref[...])
    qs = [q[:, hh * MLA_SLOT:(hh + 1) * MLA_SLOT] * mqc
          + q[:, MLA_K_COLS + hh * MLA_SLOT:MLA_K_COLS + (hh + 1) * MLA_SLOT] * mqs
          for hh in range(MLA_HEADS)]
    mq_ref[...] = jnp.concatenate(qs, axis=-1).astype(BF16)

    gq = [(p[:, PC_GQ + j * LANES:PC_GQ + (j + 1) * LANES] * gc
           + p[:, PC_GQR + j * LANES:PC_GQR + (j + 1) * LANES] * gs) * (HEAD_DIM ** -0.5)
          for j in range(GQA_Q_HEADS * HEAD_DIM // LANES)]
    gq_ref[...] = jnp.concatenate(gq, axis=-1).astype(BF16)
    gk = p[:, PC_GK:PC_GKR] * gc + p[:, PC_GKR:PC_GV] * gs
    gkv_ref[...] = jnp.concatenate([gk, p[:, PC_GV:PC_END]], axis=-1).astype(BF16)

    u_ref[...] = p[:, PC_U:PC_QL].astype(BF16)

    zz = jax.nn.gelu(p[:, PC_Z:PC_A])
    ug, v = zz[:, :GMLP_WIDTH], zz[:, GMLP_WIDTH:]
    vc = v - jnp.mean(v, axis=-1, keepdims=True)
    vn = vc * lax.rsqrt(jnp.mean(vc * vc, axis=-1, keepdims=True) + LN_EPS) * gmn_ref[...]
    group_of_lane = lax.broadcasted_iota(jnp.int32, (GMLP_CHUNK, GMLP_WIDTH), 1) // (GMLP_WIDTH // GMLP_GROUPS)
    outs = []
    for ci in range(x.shape[0] // GMLP_CHUNK):
        r0 = ci * GMLP_CHUNK
        vck = vn[r0:r0 + GMLP_CHUNK]
        vbd = jnp.concatenate([jnp.where(group_of_lane == g, vck, 0.0) for g in range(GMLP_GROUPS)], axis=0)
        mixed = _dot(ws_ref[...], vbd.astype(BF16)) + bs_ref[...]
        outs.append(ug[r0:r0 + GMLP_CHUNK] * mixed)
    d_ref[...] = jnp.concatenate(outs, axis=0).astype(BF16)


def _proj_call(x3, mods, norm_pre, wp_bf, tab, mla_kv_norm3, wkv_bf, mla_q_norm3, wq_bf, gmlp_norm3, ws_bf, bs_f,
               *, layer, lat_rows):
    b, r, d = x3.shape
    tm = PROJ_TILE
    n_lat = lat_rows // tm

    def row(bi, j):
        return (bi, j, 0)

    def lyr3(bi, j):
        return (layer, 0, 0)

    widths = (MLA_K_COLS, MLA_KV_COLS, GQA_Q_HEADS * HEAD_DIM, 2 * GQA_KV_HEADS * HEAD_DIM, S5_CHANNELS, GMLP_WIDTH)
    return pl.pallas_call(
        _proj_kernel,
        grid=(b, r // tm),
        in_specs=[
            pl.BlockSpec((None, tm, d), row),
            pl.BlockSpec((None, None, N_MOD, d), lambda bi, j: (layer, jnp.where(j >= n_lat, b, bi), 0, 0)),
            _const_spec((None, 3, d), lyr3),
            _const_spec((None, d, PC_END), lyr3),
            pl.BlockSpec((tm, TC_END), lambda bi, j: (jnp.minimum(j, n_lat), 0)),
            _const_spec((None, 1, MLA_KV_LORA), lyr3),
            _const_spec((None, 256, MLA_KV_COLS), lyr3),
            _const_spec((None, 1, MLA_Q_LORA), lyr3),
            _const_spec((None, MLA_Q_LORA, 2 * MLA_K_COLS), lyr3),
            _const_spec((None, 1, GMLP_WIDTH), lyr3),
            _const_spec((None, GMLP_CHUNK, GMLP_GROUPS * GMLP_CHUNK), lyr3),
            _const_spec((None, GMLP_CHUNK, GMLP_WIDTH), lyr3),
        ],
        out_specs=[pl.BlockSpec((None, tm, w), row) for w in widths],
        out_shape=[jax.ShapeDtypeStruct((b, r, w), BF16) for w in widths],
        compiler_params=_cparams(("parallel", "parallel")),
        name="in_proj",
    )(x3, mods, norm_pre, wp_bf, tab, mla_kv_norm3, wkv_bf, mla_q_norm3, wq_bf, gmlp_norm3, ws_bf, bs_f)


def _mla_kernel(q_ref, kv_ref, *rest):
    o_ref = rest[-1]
    tq = q_ref.shape[0]
    head_of_lane = lax.broadcasted_iota(jnp.int32, (tq, MLA_HEADS * MLA_V), 1) // MLA_V
    acc = jnp.zeros((tq, MLA_HEADS * MLA_V), F32)
    v = kv_ref[:, MLA_K_COLS:]
    sls = [slice(hh * MLA_SLOT, (hh + 1) * MLA_SLOT) for hh in range(MLA_HEADS)]
    scs = [_dot_nt(q_ref[:, sl], kv_ref[:, sl]) for sl in sls]
    ps = [jnp.exp(sc - jnp.max(sc, axis=-1, keepdims=True)) for sc in scs]
    dens = [jnp.sum(p, axis=-1, keepdims=True) for p in ps]
    os_ = [_dot(p.astype(BF16), v) for p in ps]
    for hh in range(MLA_HEADS):
        acc = jnp.where(head_of_lane == hh, os_[hh] / dens[hh], acc)
    o_ref[...] = acc.astype(BF16)


def _mla_lat_call(mq, mkv, *, lat_rows):
    b, r, _ = mq.shape
    tq = MLA_Q_TILE
    w = MLA_HEADS * MLA_V
    return pl.pallas_call(
        _mla_kernel,
        grid=(b, lat_rows // tq),
        in_specs=[
            pl.BlockSpec((None, tq, MLA_K_COLS), lambda bi, j: (bi, j, 0)),
            pl.BlockSpec((None, r, MLA_KV_COLS), lambda bi, j: (bi, 0, 0)),
        ],
        out_specs=pl.BlockSpec((None, tq, w), lambda bi, j: (bi, j, 0)),
        out_shape=jax.ShapeDtypeStruct((b, r, w), BF16),
        compiler_params=_cparams(("parallel", "parallel")),
        name="mla_latent",
    )(mq, mkv)


def _mla_ctx_call(mq, mkv, a3, *, lat_rows):
    b, r, _ = mq.shape
    c = r - lat_rows
    blk = lat_rows // c
    w = MLA_HEADS * MLA_V
    return pl.pallas_call(
        _mla_kernel,
        grid=(b,),
        in_specs=[
            pl.BlockSpec((None, c, MLA_K_COLS), lambda bi: (bi, blk, 0)),
            pl.BlockSpec((None, c, MLA_KV_COLS), lambda bi: (bi, blk, 0)),
            pl.BlockSpec(memory_space=pl.ANY),
        ],
        out_specs=pl.BlockSpec((None, c, w), lambda bi: (bi, blk, 0)),
        out_shape=jax.ShapeDtypeStruct((b, r, w), BF16),
        input_output_aliases={2: 0},
        compiler_params=_cparams(("parallel",)),
        name="mla_context",
    )(mq, mkv, a3)


def _gqa_blocks(qs, kc, vc, sink_ref, bands):
    tq = qs[0].shape[0]
    kw = GQA_KV_HEADS * HEAD_DIM
    lo = lax.broadcasted_iota(jnp.int32, (tq, kw), 1) < HEAD_DIM
    sk = jnp.concatenate([jnp.broadcast_to(sink_ref[r:r + 1, 0:1], (tq, 1)) for r in range(GQA_Q_HEADS)], axis=0)
    qsts = []
    for q in qs:
        q0, q1 = q[:, :kw], q[:, kw:]
        zero = jnp.zeros_like(q0)
        qsts.append(jnp.concatenate([jnp.where(lo, q0, zero), jnp.where(lo, zero, q0),
                                     jnp.where(lo, q1, zero), jnp.where(lo, zero, q1)], axis=0))
    scs = [_dot_nt(qst, kc) for qst in qsts]
    mxs = [jnp.maximum(jnp.max(sc, axis=-1, keepdims=True), sk) for sc in scs]
    if bands is not None:
        sbs = [jnp.where(jnp.concatenate([valid] * GQA_Q_HEADS, axis=0), _dot_nt(qst, kb), NEG_INF)
               for qst, (kb, _, valid) in zip(qsts, bands)]
        mxs = [jnp.maximum(mx, jnp.max(sb, axis=-1, keepdims=True)) for mx, sb in zip(mxs, sbs)]
    pcs = [jnp.exp(sc - mx) for sc, mx in zip(scs, mxs)]
    dens = [jnp.sum(pc, axis=-1, keepdims=True) + jnp.exp(sk - mx) for pc, mx in zip(pcs, mxs)]
    os_ = [_dot(pc.astype(BF16), vc) for pc in pcs]
    if bands is not None:
        pbs = [jnp.exp(sb - mx) for sb, mx in zip(sbs, mxs)]
        dens = [den + jnp.sum(pb, axis=-1, keepdims=True) for den, pb in zip(dens, pbs)]
        os_ = [o + _dot(pb.astype(BF16), vb) for o, pb, (_, vb, _) in zip(os_, pbs, bands)]
    outs = []
    for o, den in zip(os_, dens):
        o = o / den
        c0 = jnp.where(lo, o[0:tq], o[tq:2 * tq])
        c1 = jnp.where(lo, o[2 * tq:3 * tq], o[3 * tq:4 * tq])
        outs.append(jnp.concatenate([c0, c1], axis=-1))
    return outs


def _gqa_lat_kernel(q_ref, kv_ref, sink_ref, o_ref, *, lat_rows):
    kw = GQA_KV_HEADS * HEAD_DIM
    kc, vc = kv_ref[lat_rows:, :kw], kv_ref[lat_rows:, kw:]
    nbk = 3 * W_BLOCK
    n_in = q_ref.shape[0] // W_BLOCK
    qs, bands = [], []
    for i in range(n_in):
        n = pl.program_id(1) * n_in + i
        start = pl.multiple_of(jnp.clip((n - 1) * W_BLOCK, 0, lat_rows - nbk), W_BLOCK)
        kb = kv_ref[pl.ds(start, nbk), :kw]
        vb = kv_ref[pl.ds(start, nbk), kw:]
        qpos = n * W_BLOCK + lax.broadcasted_iota(jnp.int32, (W_BLOCK, nbk), 0)
        kpos = start + lax.broadcasted_iota(jnp.int32, (W_BLOCK, nbk), 1)
        bands.append((kb, vb, jnp.abs(qpos - kpos) <= WINDOW))
        qs.append(q_ref[i * W_BLOCK:(i + 1) * W_BLOCK])
    outs = _gqa_blocks(qs, kc, vc, sink_ref, bands)
    o_ref[...] = jnp.concatenate(outs, axis=0).astype(BF16)


def _gqa_ctx_kernel(q_ref, kv_ref, sink_ref, b3_ref, o_ref):
    kw = GQA_KV_HEADS * HEAD_DIM
    tq = q_ref.shape[0] // 2
    outs = _gqa_blocks([q_ref[:tq], q_ref[tq:]], kv_ref[:, :kw], kv_ref[:, kw:], sink_ref, None)
    o_ref[...] = jnp.concatenate(outs, axis=0).astype(BF16)


def _gqa_lat_call(gq, gkv, sink_rows, *, layer, lat_rows):
    b, r, w = gq.shape
    tq = GQA_Q_TILE
    return pl.pallas_call(
        functools.partial(_gqa_lat_kernel, lat_rows=lat_rows),
        grid=(b, lat_rows // tq),
        in_specs=[
            pl.BlockSpec((None, tq, w), lambda bi, j: (bi, j, 0)),
            pl.BlockSpec((None, r, w), lambda bi, j: (bi, 0, 0)),
            pl.BlockSpec((None, SUBLANES, LANES), lambda bi, j: (layer, 0, 0)),
        ],
        out_specs=pl.BlockSpec((None, tq, w), lambda bi, j: (bi, j, 0)),
        out_shape=jax.ShapeDtypeStruct((b, r, w), BF16),
        compiler_params=_cparams(("parallel", "parallel")),
        name="gqa_latent",
    )(gq, gkv, sink_rows)


def _gqa_ctx_call(gq, gkv, sink_rows, b3, *, layer, lat_rows):
    b, r, w = gq.shape
    c = r - lat_rows
    blk = lat_rows // c
    return pl.pallas_call(
        _gqa_ctx_kernel,
        grid=(b,),
        in_specs=[
            pl.BlockSpec((None, c, w), lambda bi: (bi, blk, 0)),
            pl.BlockSpec((None, c, w), lambda bi: (bi, blk, 0)),
            pl.BlockSpec((None, SUBLANES, LANES), lambda bi: (layer, 0, 0)),
            pl.BlockSpec(memory_space=pl.ANY),
        ],
        out_specs=pl.BlockSpec((None, c, w), lambda bi: (bi, blk, 0)),
        out_shape=jax.ShapeDtypeStruct((b, r, w), BF16),
        input_output_aliases={3: 0},
        compiler_params=_cparams(("parallel",)),
        name="gqa_context",
    )(gq, gkv, sink_rows, b3)


def _s5_kernel(*refs, reverse, chunk, batch):
    if reverse:
        (u_ref, pm_ref, a_ref, bcat_ref, ccat_ref, o_ref, buf_a, buf_b, u_a, u_b, st) = refs
        rb_ref = pmt_ref = dsk_ref = wglu_ref = bglu_ref = None
    else:
        (u_ref, rb_ref, pm_ref, pmt_ref, a_ref, bcat_ref, ccat_ref, dsk_ref, wglu_ref, bglu_ref,
         o_ref, buf_a, buf_b, u_a, u_b, st) = refs
    rws = chunk * batch
    n_tiles = 2 * S5_LANES // MXU_TILE
    per_tile = chunk // n_tiles

    @pl.when(pl.program_id(0) == 0)
    def _():
        buf_a[...] = jnp.zeros_like(buf_a)
        buf_b[...] = jnp.zeros_like(buf_b)
        u_a[...] = jnp.zeros_like(u_a)
        u_b[...] = jnp.zeros_like(u_b)
        st[...] = jnp.zeros_like(st)

    a_r, a_i = a_ref[0], a_ref[1]

    def stage(pos_in, pos_out, cur, oth, u_cur):
        u_old = u_cur[...]
        ub = _dot(pm_ref[...], u_ref[:, pos_in].reshape(rws, S5_CHANNELS)).astype(BF16)
        u_cur[...] = ub
        sr, si = st[:, :S5_LANES], st[:, S5_LANES:]
        racc = None
        for j in range(n_tiles):
            for k in range(per_tile):
                idx = j * per_tile + k
                off = ((chunk - 1 - idx) if reverse else idx) * batch
                nr = a_r * sr - a_i * si + oth[off:off + batch, :S5_LANES]
                ni = a_r * si + a_i * sr + oth[off:off + batch, S5_LANES:]
                oth[off:off + batch, :S5_LANES] = nr
                oth[off:off + batch, S5_LANES:] = ni
                sr, si = nr, ni
            cols = slice(j * MXU_TILE, (j + 1) * MXU_TILE)
            part = _dot(cur[:, cols].astype(BF16), ccat_ref[cols, :])
            racc = part if racc is None else racc + part
            cur[:, cols] = _dot(ub, bcat_ref[:, cols])
        st[:, :S5_LANES] = sr
        st[:, S5_LANES:] = si
        if reverse:
            o_ref[pos_out] = racc
        else:
            y = jax.nn.gelu(racc + rb_ref[pos_out] + dsk_ref[...] * u_old.astype(F32))
            z = _dot(y.astype(BF16), wglu_ref[...]) + bglu_ref[...]
            o_tb = (z[:, :S5_CHANNELS] * jax.nn.sigmoid(z[:, S5_CHANNELS:])).astype(BF16)
            o_ref[:, pos_out] = _dot(pmt_ref[...], o_tb).astype(BF16).reshape(batch, chunk, S5_CHANNELS)

    first, second = (1, 0) if reverse else (0, 1)
    stage(first, first, buf_a, buf_b, u_a)
    stage(second, second, buf_b, buf_a, u_b)


def _s5_call(u4, rb, pm, pmt, a_ri, b_cat, c_cat, d_skip, w_glu, b_glu, *, layer, reverse, lat_rows):
    batch, n_all, chunk, _ = u4.shape
    rws = chunk * batch
    n_l = lat_rows // chunk
    n_c = n_all - n_l
    assert n_l % 2 == 0 and n_c % 2 == 0
    np_all, np_l, np_c = n_all // 2, n_l // 2, n_c // 2
    dirn = 1 if reverse else 0

    def pair(g):
        g = jnp.clip(g, 0, np_all - 1)
        if reverse:
            return np_all - 1 - g
        return jnp.where(g < np_c, np_l + g, g - np_c)

    def par4(g):
        return (layer, dirn, 0, 0)

    def lyr3(g):
        return (layer, 0, 0)

    in_specs = [pl.BlockSpec((batch, 2, chunk, S5_CHANNELS), lambda g: (0, pair(g), 0, 0))]
    args = [u4]
    if not reverse:
        in_specs.append(pl.BlockSpec((2, rws, S5_CHANNELS), lambda g: (pair(g - 1), 0, 0)))
        args.append(rb)
    in_specs.append(_const_spec((rws, rws), lambda g: (0, 0)))
    args.append(pm)
    if not reverse:
        in_specs.append(_const_spec((rws, rws), lambda g: (0, 0)))
        args.append(pmt)
    in_specs += [
        _const_spec((None, None, 2, batch, S5_LANES), lambda g: (layer, dirn, 0, 0, 0)),
        _const_spec((None, None, S5_CHANNELS, 2 * S5_LANES), par4),
        _const_spec((None, None, 2 * S5_LANES, S5_CHANNELS), par4),
    ]
    args += [a_ri, b_cat, c_cat]
    if reverse:
        out_spec = pl.BlockSpec((2, rws, S5_CHANNELS), lambda g: (pair(g - 1), 0, 0))
        out_shape = jax.ShapeDtypeStruct((np_all * 2, rws, S5_CHANNELS), F32)
    else:
        in_specs += [
            _const_spec((None, 1, S5_CHANNELS), lyr3),
            _const_spec((None, S5_CHANNELS, 2 * S5_CHANNELS), lyr3),
            _const_spec((None, 1, 2 * S5_CHANNELS), lyr3),
        ]
        args += [d_skip, w_glu, b_glu]
        out_spec = pl.BlockSpec((batch, 2, chunk, S5_CHANNELS), lambda g: (0, pair(g - 1), 0, 0))
        out_shape = jax.ShapeDtypeStruct(u4.shape, BF16)
    return pl.pallas_call(
        functools.partial(_s5_kernel, reverse=reverse, chunk=chunk, batch=batch),
        grid=(np_all + 1,),
        in_specs=in_specs,
        out_specs=out_spec,
        out_shape=out_shape,
        scratch_shapes=[
            pltpu.VMEM((rws, 2 * S5_LANES), F32),
            pltpu.VMEM((rws, 2 * S5_LANES), F32),
            pltpu.VMEM((rws, S5_CHANNELS), BF16),
            pltpu.VMEM((rws, S5_CHANNELS), BF16),
            pltpu.VMEM((batch, 2 * S5_LANES), F32),
        ],
        compiler_params=_cparams(("arbitrary",)),
        name="s5_bwd" if reverse else "s5_fwd",
    )(*args)


def _merge_kernel(x_ref, mod_ref, modt_ref, gpre_ref, gpost_ref, a_ref, b_ref, c_ref, d_ref, wg_ref, wb_ref, wo_ref,
                  o_ref, *, tail):
    tm, d = x_ref.shape
    ms = (mod_ref[...], modt_ref[...])
    pieces = _mod_pieces(tm, tail)
    h = jnp.concatenate(
        [(_rms(x_ref[r0:r1], gpre_ref[1:2]) * (1.0 + ms[k][4:5]) + ms[k][3:4]).astype(BF16) for r0, r1, k in pieces],
        axis=0)
    merged = None
    for i, br in enumerate((a_ref, b_ref, c_ref, d_ref)):
        gate = jax.nn.sigmoid(_dot(h, wg_ref[:, i * d:(i + 1) * d]))
        term = gate * _dot(br[...], wb_ref[i])
        merged = term if merged is None else merged + term
    y = _dot(merged.astype(BF16), wo_ref[...])
    for r0, r1, k in pieces:
        o_ref[r0:r1] = x_ref[r0:r1] + ms[k][5:6] * _rms(y[r0:r1], gpost_ref[1:2])


def _merge_call(x3, mods, norm_pre, norm_post, a3, b3, c3, d3, wg_bf, wb_bf, wo_bf, *, layer, tm, n_blk, ctx_rows):
    b, _, d = x3.shape

    def row(bi, j):
        return (bi, j, 0)

    def lyr3(bi, j):
        return (layer, 0, 0)

    return pl.pallas_call(
        functools.partial(_merge_kernel, tail=ctx_rows),
        grid=(b, n_blk),
        in_specs=_wide_specs(b, tm, d, n_blk, layer, ctx_rows > 0) + [
            _const_spec((None, 3, d), lyr3),
            _const_spec((None, 3, d), lyr3),
            pl.BlockSpec((None, tm, BRANCH_WIDTH), row),
            pl.BlockSpec((None, tm, BRANCH_WIDTH), row),
            pl.BlockSpec((None, tm, BRANCH_WIDTH), row),
            pl.BlockSpec((None, tm, BRANCH_WIDTH), row),
            _const_spec((None, d, N_BRANCH * d), lyr3),
            _const_spec((None, N_BRANCH, BRANCH_WIDTH, d), lambda bi, j: (layer, 0, 0, 0)),
            _const_spec((None, d, d), lyr3),
        ],
        out_specs=pl.BlockSpec((None, tm, d), row),
        out_shape=jax.ShapeDtypeStruct((b, n_blk * tm, d), F32),
        compiler_params=_cparams(("parallel", "parallel")),
        name="merge",
    )(x3, mods, mods, norm_pre, norm_post, a3, b3, c3, d3, wg_bf, wb_bf, wo_bf)


def _rot_cols(w):
    q = w.shape[-1] // 4
    return jnp.concatenate([-w[..., q:2 * q], w[..., 0:q], -w[..., 3 * q:4 * q], w[..., 2 * q:3 * q]], axis=-1)


def _rope_full(rows_n, rot_dim):
    f32 = np.float32
    axis_dim = rot_dim // 2
    inv_freq = (f32(ROPE_THETA) ** (-np.arange(0, axis_dim, 2, dtype=f32) / f32(axis_dim))).astype(f32)
    row = np.repeat(np.arange(rows_n, dtype=f32), GRID_W)
    col = np.tile(np.arange(GRID_W, dtype=f32), rows_n)
    ang_r = row[:, None] * inv_freq[None, :]
    ang_c = col[:, None] * inv_freq[None, :]
    cos = np.concatenate([np.cos(ang_r), np.cos(ang_r), np.cos(ang_c), np.cos(ang_c)], axis=-1)
    sin = np.concatenate([np.sin(ang_r), np.sin(ang_r), np.sin(ang_c), np.sin(ang_c)], axis=-1)
    return cos.astype(f32), sin.astype(f32)


def _rope_table(l, tm):
    f32 = np.float32
    cg, sg = _rope_full(l // GRID_W, HEAD_DIM)
    cm, sm = _rope_full(l // GRID_W, MLA_ROPE)
    scale = f32((MLA_NOPE + MLA_ROPE) ** -0.5)
    pad = MLA_SLOT - MLA_NOPE - MLA_ROPE

    def build(cg, sg, cm, sm):
        n = cg.shape[0]
        ones = np.ones((n, MLA_NOPE), f32)
        zeros = np.zeros((n, MLA_NOPE), f32)
        return np.concatenate([
            cg, cg, sg, sg,
            scale * ones, scale * cm, zeros[:, :pad],
            zeros, scale * sm, zeros[:, :pad],
            cm, sm, zeros,
        ], axis=-1)

    lat = build(cg, sg, cm, sm)
    one_g, zero_g = np.ones((tm, HEAD_DIM), f32), np.zeros((tm, HEAD_DIM), f32)
    ctx = build(one_g, zero_g, one_g[:, :MLA_ROPE], zero_g[:, :MLA_ROPE])
    return jnp.asarray(np.concatenate([lat, ctx], axis=0))


GQA_HEAD_ORDER = (0, 2, 1, 3)


def _proj_cols(w):
    o = 0

    def take(n):
        nonlocal o
        v = w[..., o:o + n]
        o += n
        return v

    kvl, kpe = take(MLA_KV_LORA), take(MLA_ROPE)
    gk, gv = take(GQA_KV_HEADS * HEAD_DIM), take(GQA_KV_HEADS * HEAD_DIM)
    u, ql, gq, z = take(S5_CHANNELS), take(MLA_Q_LORA), take(GQA_Q_HEADS * HEAD_DIM), take(2 * GMLP_WIDTH)
    gate = w[..., o:]
    gqh = [gq[..., i * HEAD_DIM:(i + 1) * HEAD_DIM] for i in range(GQA_Q_HEADS)]
    gkh = [gk[..., i * HEAD_DIM:(i + 1) * HEAD_DIM] for i in range(GQA_KV_HEADS)]
    pad = jnp.zeros(w.shape[:-1] + (PC_U - PC_A - MLA_KV_LORA - 2 * MLA_ROPE,), w.dtype)
    wp = jnp.concatenate([z, kvl, kpe, _rot_cols(kpe), pad, u, ql]
                         + [gqh[i] for i in GQA_HEAD_ORDER] + [_rot_cols(gqh[i]) for i in GQA_HEAD_ORDER]
                         + [gk] + [_rot_cols(h) for h in gkh] + [gv], axis=-1)
    return wp, gate


def _wprep_kernel(w_ref, wp_ref, wg_ref):
    wp, wg = _proj_cols(w_ref[...])
    wp_ref[...] = wp.astype(BF16)
    wg_ref[...] = wg.astype(BF16)


def _wprep_call(w_in):
    depth, d, n = w_in.shape
    tr = 256
    n_gate = N_BRANCH * d
    return pl.pallas_call(
        _wprep_kernel,
        grid=(depth, d // tr),
        in_specs=[pl.BlockSpec((None, tr, n), lambda l, i: (l, i, 0))],
        out_specs=[pl.BlockSpec((None, tr, PC_END), lambda l, i: (l, i, 0)),
                   pl.BlockSpec((None, tr, n_gate), lambda l, i: (l, i, 0))],
        out_shape=[jax.ShapeDtypeStruct((depth, d, PC_END), BF16), jax.ShapeDtypeStruct((depth, d, n_gate), BF16)],
        compiler_params=_cparams(("parallel", "parallel")),
        name="w_prep",
    )(w_in)


def _mla_kv_weight(w_ukv):
    depth = w_ukv.shape[0]
    wh = w_ukv.reshape(depth, MLA_KV_LORA, MLA_HEADS, MLA_NOPE + MLA_V)
    k_nope, v = wh[..., :MLA_NOPE], wh[..., MLA_NOPE:]
    kslot = jnp.concatenate([k_nope, jnp.zeros((depth, MLA_KV_LORA, MLA_HEADS, MLA_SLOT - MLA_NOPE), F32)], axis=-1)
    top = jnp.concatenate([kslot.reshape(depth, MLA_KV_LORA, -1), v.reshape(depth, MLA_KV_LORA, -1)], axis=-1)
    eye = jnp.eye(MLA_ROPE, dtype=F32)
    pe_slot = jnp.concatenate([jnp.zeros((MLA_ROPE, MLA_NOPE), F32), eye,
                               jnp.zeros((MLA_ROPE, MLA_SLOT - MLA_NOPE - MLA_ROPE), F32)], axis=-1)
    pe_rows = jnp.concatenate([jnp.tile(pe_slot, (1, MLA_HEADS)), jnp.zeros((MLA_ROPE, MLA_HEADS * MLA_V), F32)],
                              axis=-1)
    pe_rows = jnp.broadcast_to(pe_rows, (depth,) + pe_rows.shape)
    tail = jnp.zeros((depth, 256 - MLA_KV_LORA - 2 * MLA_ROPE, MLA_KV_COLS), F32)
    return jnp.concatenate([top, pe_rows, pe_rows, tail], axis=1).astype(BF16)


def _mla_q_weight(w_uq):
    depth = w_uq.shape[0]
    wh = w_uq.reshape(depth, MLA_Q_LORA, MLA_HEADS, MLA_NOPE + MLA_ROPE)
    nope, pe = wh[..., :MLA_NOPE], wh[..., MLA_NOPE:]
    pad = jnp.zeros((depth, MLA_Q_LORA, MLA_HEADS, MLA_SLOT - MLA_NOPE - MLA_ROPE), F32)
    full = jnp.concatenate([nope, pe, pad], axis=-1).reshape(depth, MLA_Q_LORA, -1)
    rot = jnp.concatenate([jnp.zeros_like(nope), _rot_cols(pe), pad], axis=-1).reshape(depth, MLA_Q_LORA, -1)
    return jnp.concatenate([full, rot], axis=-1).astype(BF16)


def _block_diag(w):
    g, a, b = w.shape[-3:]
    lead = w.shape[:-3]
    cols = jnp.swapaxes(w, -3, -2).reshape(lead + (a, g * b))
    tiled = jnp.tile(cols, (1,) * len(lead) + (g, 1))
    same = (jnp.arange(g * a)[:, None] // a) == (jnp.arange(g * b)[None, :] // b)
    return jnp.where(same, tiled, jnp.zeros((), w.dtype))


def kernel(x, c, ctx, c_ctx, w_ada, b_ada, norm_pre, norm_post, w_ffn_in, w_ffn_out, w_in, mla_q_norm, mla_w_uq,
           mla_kv_norm, mla_w_ukv, gqa_sink, s5_lam_re, s5_lam_im, s5_log_dt, s5_b_re, s5_b_im, s5_c_re, s5_c_im,
           s5_d, s5_w_glu, s5_b_glu, gmlp_norm, gmlp_w_s, gmlp_b_s, w_branch, w_out):
    b, l, d = x.shape
    cl = ctx.shape[1]
    r = l + cl
    depth = w_ada.shape[0]
    assert b == SUBLANES and b < MOD_ROWS
    assert l % GRID_W == 0 and l >= 3 * W_BLOCK and l % cl == 0
    assert l % PROJ_TILE == 0 and cl % PROJ_TILE == 0 and l % MLA_Q_TILE == 0 and l % GQA_Q_TILE == 0
    assert l % S5_CHUNK == 0 and cl % S5_CHUNK == 0 and l % LAT_TILE == 0
    assert r % WIDE_TILE == 0 and cl <= WIDE_TILE

    wffn_in_bf = w_ffn_in.astype(BF16)
    wffn_out_bf = w_ffn_out.astype(BF16)
    wp_bf, wg_bf = _wprep_call(w_in)
    wkv_bf = _mla_kv_weight(mla_w_ukv)
    wq_bf = _mla_q_weight(mla_w_uq)
    head_order = jnp.array(GQA_HEAD_ORDER)
    wb1 = w_branch[:, 1].reshape(depth, GQA_Q_HEADS, HEAD_DIM, d)[:, head_order].reshape(depth, BRANCH_WIDTH, d)
    wb_bf = jnp.concatenate([w_branch[:, :1], wb1[:, None], w_branch[:, 2:]], axis=1).astype(BF16)
    wo_bf = w_out.astype(BF16)
    sink_rows = jnp.broadcast_to(
        jnp.concatenate([gqa_sink[:, head_order], jnp.zeros((depth, SUBLANES - GQA_Q_HEADS), F32)], axis=1)[:, :, None],
        (depth, SUBLANES, LANES))
    ws_bf = jnp.transpose(gmlp_w_s, (0, 2, 1, 3)).reshape(depth, GMLP_CHUNK, GMLP_GROUPS * GMLP_CHUNK).astype(BF16)
    bs_f = jnp.repeat(jnp.transpose(gmlp_b_s, (0, 2, 1)), GMLP_WIDTH // GMLP_GROUPS, axis=2)
    tab = _rope_table(l, PROJ_TILE)
    mla_kv_norm3 = mla_kv_norm[:, None, :]
    mla_q_norm3 = mla_q_norm[:, None, :]
    gmlp_norm3 = gmlp_norm[:, None, :]
    s5_d3 = s5_d[:, None, :]
    s5_b_glu3 = s5_b_glu[:, None, :]
    wglu_bf = s5_w_glu.astype(BF16)
    tb = jnp.arange(S5_CHUNK * b)
    pm = (tb[None, :] == ((tb % b) * S5_CHUNK + tb // b)[:, None]).astype(BF16)
    pmt = pm.T

    hg, p, g = S5_GROUP, S5_STATE, S5_GROUPS
    n_par = depth * 2 * g
    rep = lambda t: jnp.repeat(t.reshape(n_par, p), hg, axis=1)
    ldt = jnp.broadcast_to(s5_log_dt[..., None], (depth, 2, g, p))
    a_re_x, a_im_x, bb_re, bb_im = _s5_disc_call(
        rep(s5_lam_re), rep(s5_lam_im), rep(ldt), s5_b_re.reshape(n_par, p * hg), s5_b_im.reshape(n_par, p * hg))
    a_ri = jnp.broadcast_to(
        jnp.stack([a_re_x[:, ::hg].reshape(depth, 2, g * p), a_im_x[:, ::hg].reshape(depth, 2, g * p)], axis=2)
        [:, :, :, None, :], (depth, 2, 2, b, g * p))
    bb_re = jnp.swapaxes(bb_re.reshape(depth, 2, g, p, hg), -1, -2)
    bb_im = jnp.swapaxes(bb_im.reshape(depth, 2, g, p, hg), -1, -2)
    s5_b_cat = jnp.concatenate([_block_diag(bb_re), _block_diag(bb_im)], axis=-1).astype(BF16)
    s5_c_cat = jnp.concatenate([_block_diag(jnp.swapaxes(s5_c_re, -1, -2)),
                                _block_diag(jnp.swapaxes(-s5_c_im, -1, -2))], axis=-2).astype(BF16)

    cs = jnp.concatenate([c, c_ctx[None, :], jnp.zeros((MOD_ROWS - b - 1, d), F32)], axis=0)
    mods = _ada_call(cs, w_ada, b_ada[:, None, :]).reshape(depth, MOD_ROWS, N_MOD, d)

    xs = jnp.concatenate([x, ctx], axis=1)
    n_wide = r // WIDE_TILE
    for layer in range(depth):
        last = layer == depth - 1
        xs = _ffn_call(xs, mods, norm_pre, norm_post, wffn_in_bf, wffn_out_bf,
                       layer=layer, s=0, which=0, tm=WIDE_TILE, n_blk=n_wide, ctx_rows=cl)
        mq, mkv, gq, gkv, u3, d3 = _proj_call(
            xs, mods, norm_pre, wp_bf, tab, mla_kv_norm3, wkv_bf, mla_q_norm3, wq_bf, gmlp_norm3, ws_bf, bs_f,
            layer=layer, lat_rows=l)
        a3 = _mla_lat_call(mq, mkv, lat_rows=l)
        b3 = _gqa_lat_call(gq, gkv, sink_rows, layer=layer, lat_rows=l)
        u4 = u3.reshape(b, r // S5_CHUNK, S5_CHUNK, S5_CHANNELS)
        rb = _s5_call(u4, None, pm, None, a_ri, s5_b_cat, s5_c_cat, None, None, None,
                      layer=layer, reverse=True, lat_rows=l)
        c3 = _s5_call(u4, rb, pm, pmt, a_ri, s5_b_cat, s5_c_cat, s5_d3, wglu_bf, s5_b_glu3,
                      layer=layer, reverse=False, lat_rows=l).reshape(b, r, S5_CHANNELS)
        if last:
            tm, n_blk, ctx_rows = LAT_TILE, l // LAT_TILE, 0
        else:
            a3 = _mla_ctx_call(mq, mkv, a3, lat_rows=l)
            b3 = _gqa_ctx_call(gq, gkv, sink_rows, b3, layer=layer, lat_rows=l)
            tm, n_blk, ctx_rows = WIDE_TILE, n_wide, cl
        xs = _merge_call(xs, mods, norm_pre, norm_post, a3, b3, c3, d3, wg_bf, wb_bf, wo_bf,
                         layer=layer, tm=tm, n_blk=n_blk, ctx_rows=ctx_rows)
        xs = _ffn_call(xs, mods, norm_pre, norm_post, wffn_in_bf, wffn_out_bf,
                       layer=layer, s=2, which=1, tm=tm, n_blk=n_blk, ctx_rows=ctx_rows)
    return xs
```

```python
import functools

import jax
import jax.numpy as jnp
import numpy as np
from jax import lax
from jax.experimental import pallas as pl
from jax.experimental.pallas import tpu as pltpu

F32 = jnp.float32
BF16 = jnp.bfloat16

GRID_W = 64
HEAD_DIM = 64
ROPE_THETA = 10000.0
RMS_EPS = 1e-6
LN_EPS = 1e-5
NEG_INF = -1e30
MLA_HEADS = 4
MLA_Q_LORA = 256
MLA_KV_LORA = 128
MLA_NOPE = 64
MLA_ROPE = 32
MLA_V = 64
GQA_Q_HEADS = 4
GQA_KV_HEADS = 2
WINDOW = 128
W_BLOCK = 128
S5_CHANNELS = 256
S5_GROUP = 16
S5_GROUPS = S5_CHANNELS // S5_GROUP
S5_STATE = 64
GMLP_WIDTH = 256
GMLP_CHUNK = 128
GMLP_GROUPS = 4
N_BRANCH = 4
BRANCH_WIDTH = 256
N_MOD = 9

LANES = 128
SUBLANES = 8
MXU_TILE = 256
VMEM_LIMIT_BYTES = 56 * 1024 * 1024

PROJ_TILE = 256
WIDE_TILE = 768
LAT_TILE = 512
MLA_Q_TILE = 512
GQA_Q_TILE = 512
S5_CHUNK = 64
MOD_ROWS = 16
MLA_SLOT = LANES
S5_LANES = S5_GROUPS * S5_STATE

PC_Z = 0
PC_A = PC_Z + 2 * GMLP_WIDTH
PC_U = PC_A + 256
PC_QL = PC_U + S5_CHANNELS
PC_GQ = PC_QL + MLA_Q_LORA
PC_GQR = PC_GQ + GQA_Q_HEADS * HEAD_DIM
PC_GK = PC_GQR + GQA_Q_HEADS * HEAD_DIM
PC_GKR = PC_GK + GQA_KV_HEADS * HEAD_DIM
PC_GV = PC_GKR + GQA_KV_HEADS * HEAD_DIM
PC_END = PC_GV + GQA_KV_HEADS * HEAD_DIM

TC_GC, TC_GS, TC_MQC, TC_MQS, TC_MKT, TC_END = 0, 128, 256, 384, 512, 640

MLA_K_COLS = MLA_HEADS * MLA_SLOT
MLA_KV_COLS = MLA_K_COLS + MLA_HEADS * MLA_V


def _cparams(sem):
    return pltpu.CompilerParams(dimension_semantics=sem, vmem_limit_bytes=VMEM_LIMIT_BYTES)


def _const_spec(shape, index_map):
    return pl.BlockSpec(shape, index_map, pipeline_mode=pl.Buffered(1))


def _rms(x, gain):
    return x * lax.rsqrt(jnp.mean(x * x, axis=-1, keepdims=True) + RMS_EPS) * gain


def _dot(a, b):
    return jnp.dot(a, b, preferred_element_type=F32)


def _dot_nt(a, b):
    return lax.dot_general(a, b, (((1,), (1,)), ((), ())), preferred_element_type=F32)


def _ada_kernel(cs_ref, w_ref, b_ref, o_ref):
    cs = cs_ref[...]
    s = (cs * jax.nn.sigmoid(cs)).astype(BF16)
    o_ref[...] = _dot(s, w_ref[...].astype(BF16)) + b_ref[...]


def _ada_call(cs, w_ada, b_ada3):
    depth, d, n = w_ada.shape
    tn = 1024
    return pl.pallas_call(
        _ada_kernel,
        grid=(depth, n // tn),
        in_specs=[
            pl.BlockSpec((MOD_ROWS, d), lambda l, j: (0, 0)),
            pl.BlockSpec((None, d, tn), lambda l, j: (l, 0, j)),
            pl.BlockSpec((None, 1, tn), lambda l, j: (l, 0, j)),
        ],
        out_specs=pl.BlockSpec((None, MOD_ROWS, tn), lambda l, j: (l, 0, j)),
        out_shape=jax.ShapeDtypeStruct((depth, MOD_ROWS, n), F32),
        compiler_params=_cparams(("parallel", "parallel")),
        name="ada_mod",
    )(cs, w_ada, b_ada3)


def _s5_disc_kernel(lre_ref, lim_ref, ldt_ref, bre_ref, bim_ref, are_ref, aim_ref, bbre_ref, bbim_ref):
    lam_re = jnp.minimum(lre_ref[...], -1e-4)
    lam_im = lim_ref[...]
    dt = jnp.exp(ldt_ref[...])
    mag = jnp.exp(lam_re * dt)
    a_re = mag * jnp.cos(lam_im * dt)
    a_im = mag * jnp.sin(lam_im * dt)
    nr, ni = a_re - 1.0, a_im
    den = lam_re * lam_re + lam_im * lam_im
    coef_re = (nr * lam_re + ni * lam_im) / den
    coef_im = (ni * lam_re - nr * lam_im) / den
    b_re, b_im = bre_ref[...], bim_ref[...]
    are_ref[...] = a_re
    aim_ref[...] = a_im
    bbre_ref[...] = coef_re * b_re - coef_im * b_im
    bbim_ref[...] = coef_re * b_im + coef_im * b_re


def _s5_disc_call(lre, lim, ldt, bre, bim):
    shp = jax.ShapeDtypeStruct(lre.shape, F32)
    return pl.pallas_call(_s5_disc_kernel, out_shape=(shp, shp, shp, shp), name="s5_disc")(lre, lim, ldt, bre, bim)


def _mod_pieces(tm, tail):
    return ((0, tm - tail, 0), (tm - tail, tm, 1)) if tail else ((0, tm, 0),)


def _lag_block(b, n_blk, lag):
    def bj(q):
        q = jnp.clip(q - lag, 0, b * n_blk - 1)
        return q // n_blk, q % n_blk
    return bj


def _lag_specs(b, tm, d, n_blk, layer, ctx_tail, lag):
    bj = _lag_block(b, n_blk, lag)

    def tail_row(q):
        bi, j = bj(q)
        return jnp.where(j == n_blk - 1, b, bi) if ctx_tail else bi

    return [
        pl.BlockSpec((None, tm, d), lambda q: (*bj(q), 0)),
        pl.BlockSpec((None, None, N_MOD, d), lambda q: (layer, bj(q)[0], 0, 0)),
        pl.BlockSpec((None, None, N_MOD, d), lambda q: (layer, tail_row(q), 0, 0)),
    ]


def _ffn_kernel(xn_ref, modn_ref, modnt_ref, xo_ref, modo_ref, modot_ref, gpre_ref, gpost_ref, wi_ref, wo_ref,
                o_ref, h_ref, y_ref, *, s, d_ff, chunks, tail, n_steps):
    q = pl.program_id(0)
    pieces = _mod_pieces(xn_ref.shape[0], tail)

    def pre_norm():
        ms = (modn_ref[...], modnt_ref[...])
        return jnp.concatenate(
            [(_rms(xn_ref[r0:r1], gpre_ref[s:s + 1]) * (1.0 + ms[k][3 * s + 1:3 * s + 2]) + ms[k][3 * s:3 * s + 1])
             .astype(BF16) for r0, r1, k in pieces], axis=0)

    def post_norm():
        ms = (modo_ref[...], modot_ref[...])
        return jnp.concatenate(
            [xo_ref[r0:r1] + 0.5 * ms[k][3 * s + 2:3 * s + 3] * _rms(y_ref[r0:r1], gpost_ref[s:s + 1])
             for r0, r1, k in pieces], axis=0)

    def up(h, c0, c1):
        a = _dot(h, wi_ref[:, c0:c1])
        g = _dot(h, wi_ref[:, d_ff + c0:d_ff + c1])
        return (a * jax.nn.sigmoid(a) * g).astype(BF16)

    @pl.when(q == 0)
    def _():
        h_ref[...] = pre_norm()
        y_ref[...] = jnp.zeros_like(y_ref)

    @pl.when(jnp.logical_and(q > 0, q < n_steps - 1))
    def _():
        always = q < n_steps
        h = h_ref[...]
        acts = [up(h, *chunks[0])]
        out = post_norm()
        o_ref[...] = out
        h = jnp.where(always, h, out.astype(BF16))
        acts += [up(h, c0, c1) for c0, c1 in chunks[1:]]
        h_new = pre_norm()
        h_ref[...] = h_new
        w0 = chunks[0][1] - chunks[0][0]
        acts[0] = jnp.concatenate([jnp.where(always, acts[0][:, :h_new.shape[1]], h_new), acts[0][:, h_new.shape[1]:]],
                                  axis=1) if w0 > h_new.shape[1] else jnp.where(always, acts[0], h_new[:, :w0])
        y = None
        for act, (c0, c1) in zip(acts, chunks):
            part = _dot(act, wo_ref[c0:c1, :])
            y = part if y is None else y + part
        y_ref[...] = y

    @pl.when(q == n_steps - 1)
    def _():
        o_ref[...] = post_norm()


def _ffn_call(x3, mods, norm_pre, norm_post, w_in_bf, w_out_bf, *, layer, s, which, tm, n_blk, ctx_rows):
    b, _, d = x3.shape
    d_ff = w_out_bf.shape[2]
    half = (d_ff // MXU_TILE + 1) // 2 * MXU_TILE
    chunks = ((0, half), (half, d_ff)) if 0 < half < d_ff else ((0, d_ff),)
    n = b * n_blk
    kern = functools.partial(_ffn_kernel, s=s, d_ff=d_ff, chunks=chunks, tail=ctx_rows, n_steps=n + 2)

    def specs(lag):
        return _lag_specs(b, tm, d, n_blk, layer, ctx_rows > 0, lag)

    return pl.pallas_call(
        kern,
        grid=(n + 2,),
        in_specs=specs(0) + specs(2) + [
            _const_spec((None, 3, d), lambda q: (layer, 0, 0)),
            _const_spec((None, 3, d), lambda q: (layer, 0, 0)),
            _const_spec((None, None, d, 2 * d_ff), lambda q: (layer, which, 0, 0)),
            _const_spec((None, None, d_ff, d), lambda q: (layer, which, 0, 0)),
        ],
        out_specs=specs(2)[0],
        out_shape=jax.ShapeDtypeStruct((b, n_blk * tm, d), F32),
        scratch_shapes=[pltpu.VMEM((tm, d), BF16), pltpu.VMEM((tm, d), F32)],
        compiler_params=_cparams(("arbitrary",)),
        name="ffn",
    )(x3, mods, mods, x3, mods, mods, norm_pre, norm_post, w_in_bf, w_out_bf)


def _proj_kernel(x_ref, mod_ref, gpre_ref, wp_ref, tab_ref, gkvn_ref, wkv_ref, gqn_ref, wq_ref,
                 gmn_ref, ws_ref, bs_ref,
                 mq_ref, mkv_ref, gq_ref, gkv_ref, u_ref, d_ref):
    x = x_ref[...]
    m = mod_ref[...]
    h = (_rms(x, gpre_ref[1:2]) * (1.0 + m[4:5]) + m[3:4]).astype(BF16)
    p = _dot(h, wp_ref[...])
    tab = tab_ref[...]
    gc, gs = tab[:, TC_GC:TC_GS], tab[:, TC_GS:TC_MQC]
    mqc, mqs, mkt = tab[:, TC_MQC:TC_MQS], tab[:, TC_MQS:TC_MKT], tab[:, TC_MKT:TC_END]

    kvl = p[:, PC_A:PC_A + MLA_KV_LORA]
    kvn = _rms(kvl, gkvn_ref[...])
    pe = p[:, PC_A + MLA_KV_LORA:PC_U] * mkt
    a2 = jnp.concatenate([kvn, pe], axis=-1).astype(BF16)
    mkv_ref[...] = _dot(a2, wkv_ref[...]).astype(BF16)

    qn = _rms(p[:, PC_QL:PC_GQ], gqn_ref[...]).astype(BF16)
    q = _dot(qn, wq_ref[...])
    qs = [q[:, hh * MLA_SLOT:(hh + 1) * MLA_SLOT] * mqc
          + q[:, MLA_K_COLS + hh * MLA_SLOT:MLA_K_COLS + (hh + 1) * MLA_SLOT] * mqs
          for hh in range(MLA_HEADS)]
    mq_ref[...] = jnp.concatenate(qs, axis=-1).astype(BF16)

    gq = [(p[:, PC_GQ + j * LANES:PC_GQ + (j + 1) * LANES] * gc
           + p[:, PC_GQR + j * LANES:PC_GQR + (j + 1) * LANES] * gs) * (HEAD_DIM ** -0.5)
          for j in range(GQA_Q_HEADS * HEAD_DIM // LANES)]
    gq_ref[...] = jnp.concatenate(gq, axis=-1).astype(BF16)
    gk = p[:, PC_GK:PC_GKR] * gc + p[:, PC_GKR:PC_GV] * gs
    gkv_ref[...] = jnp.concatenate([gk, p[:, PC_GV:PC_END]], axis=-1).astype(BF16)

    u_ref[...] = p[:, PC_U:PC_QL].astype(BF16)

    zz = jax.nn.gelu(p[:, PC_Z:PC_A])
    ug, v = zz[:, :GMLP_WIDTH], zz[:, GMLP_WIDTH:]
    vc = v - jnp.mean(v, axis=-1, keepdims=True)
    vn = vc * lax.rsqrt(jnp.mean(vc * vc, axis=-1, keepdims=True) + LN_EPS) * gmn_ref[...]
    group_of_lane = lax.broadcasted_iota(jnp.int32, (GMLP_CHUNK, GMLP_WIDTH), 1) // (GMLP_WIDTH // GMLP_GROUPS)
    outs = []
    for ci in range(x.shape[0] // GMLP_CHUNK):
        r0 = ci * GMLP_CHUNK
        vck = vn[r0:r0 + GMLP_CHUNK]
        vbd = jnp.concatenate([jnp.where(group_of_lane == g, vck, 0.0) for g in range(GMLP_GROUPS)], axis=0)
        mixed = _dot(ws_ref[...], vbd.astype(BF16)) + bs_ref[...]
        outs.append(ug[r0:r0 + GMLP_CHUNK] * mixed)
    d_ref[...] = jnp.concatenate(outs, axis=0).astype(BF16)


def _proj_call(x3, mods, norm_pre, wp_bf, tab, mla_kv_norm3, wkv_bf, mla_q_norm3, wq_bf, gmlp_norm3, ws_bf, bs_f,
               *, layer, lat_rows):
    b, r, d = x3.shape
    tm = PROJ_TILE
    n_lat = lat_rows // tm

    def row(bi, j):
        return (bi, j, 0)

    def lyr3(bi, j):
        return (layer, 0, 0)

    widths = (MLA_K_COLS, MLA_KV_COLS, GQA_Q_HEADS * HEAD_DIM, 2 * GQA_KV_HEADS * HEAD_DIM, S5_CHANNELS, GMLP_WIDTH)
    return pl.pallas_call(
        _proj_kernel,
        grid=(b, r // tm),
        in_specs=[
            pl.BlockSpec((None, tm, d), row),
            pl.BlockSpec((None, None, N_MOD, d), lambda bi, j: (layer, jnp.where(j >= n_lat, b, bi), 0, 0)),
            _const_spec((None, 3, d), lyr3),
            _const_spec((None, d, PC_END), lyr3),
            pl.BlockSpec((tm, TC_END), lambda bi, j: (jnp.minimum(j, n_lat), 0)),
            _const_spec((None, 1, MLA_KV_LORA), lyr3),
            _const_spec((None, 256, MLA_KV_COLS), lyr3),
            _const_spec((None, 1, MLA_Q_LORA), lyr3),
            _const_spec((None, MLA_Q_LORA, 2 * MLA_K_COLS), lyr3),
            _const_spec((None, 1, GMLP_WIDTH), lyr3),
            _const_spec((None, GMLP_CHUNK, GMLP_GROUPS * GMLP_CHUNK), lyr3),
            _const_spec((None, GMLP_CHUNK, GMLP_WIDTH), lyr3),
        ],
        out_specs=[pl.BlockSpec((None, tm, w), row) for w in widths],
        out_shape=[jax.ShapeDtypeStruct((b, r, w), BF16) for w in widths],
        compiler_params=_cparams(("parallel", "parallel")),
        name="in_proj",
    )(x3, mods, norm_pre, wp_bf, tab, mla_kv_norm3, wkv_bf, mla_q_norm3, wq_bf, gmlp_norm3, ws_bf, bs_f)


def _mla_kernel(q_ref, kv_ref, *rest):
    o_ref = rest[-1]
    tq = q_ref.shape[0]
    head_of_lane = lax.broadcasted_iota(jnp.int32, (tq, MLA_HEADS * MLA_V), 1) // MLA_V
    acc = jnp.zeros((tq, MLA_HEADS * MLA_V), F32)
    v = kv_ref[:, MLA_K_COLS:]
    sls = [slice(hh * MLA_SLOT, (hh + 1) * MLA_SLOT) for hh in range(MLA_HEADS)]
    scs = [_dot_nt(q_ref[:, sl], kv_ref[:, sl]) for sl in sls]
    ps = [jnp.exp(sc - jnp.max(sc, axis=-1, keepdims=True)) for sc in scs]
    dens = [jnp.sum(p, axis=-1, keepdims=True) for p in ps]
    o_all = _dot(jnp.concatenate([p.astype(BF16) for p in ps], axis=0), v)
    for hh in range(MLA_HEADS):
        acc = jnp.where(head_of_lane == hh, o_all[hh * tq:(hh + 1) * tq] / dens[hh], acc)
    o_ref[...] = acc.astype(BF16)


def _mla_lat_call(mq, mkv, *, lat_rows):
    b, r, _ = mq.shape
    tq = MLA_Q_TILE
    w = MLA_HEADS * MLA_V
    return pl.pallas_call(
        _mla_kernel,
        grid=(b, lat_rows // tq),
        in_specs=[
            pl.BlockSpec((None, tq, MLA_K_COLS), lambda bi, j: (bi, j, 0)),
            pl.BlockSpec((None, r, MLA_KV_COLS), lambda bi, j: (bi, 0, 0)),
        ],
        out_specs=pl.BlockSpec((None, tq, w), lambda bi, j: (bi, j, 0)),
        out_shape=jax.ShapeDtypeStruct((b, r, w), BF16),
        compiler_params=_cparams(("parallel", "parallel")),
        name="mla_latent",
    )(mq, mkv)


def _mla_ctx_call(mq, mkv, a3, *, lat_rows):
    b, r, _ = mq.shape
    c = r - lat_rows
    blk = lat_rows // c
    w = MLA_HEADS * MLA_V
    return pl.pallas_call(
        _mla_kernel,
        grid=(b,),
        in_specs=[
            pl.BlockSpec((None, c, MLA_K_COLS), lambda bi: (bi, blk, 0)),
            pl.BlockSpec((None, c, MLA_KV_COLS), lambda bi: (bi, blk, 0)),
            pl.BlockSpec(memory_space=pl.ANY),
        ],
        out_specs=pl.BlockSpec((None, c, w), lambda bi: (bi, blk, 0)),
        out_shape=jax.ShapeDtypeStruct((b, r, w), BF16),
        input_output_aliases={2: 0},
        compiler_params=_cparams(("parallel",)),
        name="mla_context",
    )(mq, mkv, a3)


def _gqa_blocks(qs, kc, vc, sink_ref, bands):
    tq = qs[0].shape[0]
    kw = GQA_KV_HEADS * HEAD_DIM
    lo = lax.broadcasted_iota(jnp.int32, (tq, kw), 1) < HEAD_DIM
    sk = jnp.concatenate([jnp.broadcast_to(sink_ref[r:r + 1, 0:1], (tq, 1)) for r in range(GQA_Q_HEADS)], axis=0)
    qsts = []
    for q in qs:
        q0, q1 = q[:, :kw], q[:, kw:]
        zero = jnp.zeros_like(q0)
        qsts.append(jnp.concatenate([jnp.where(lo, q0, zero), jnp.where(lo, zero, q0),
                                     jnp.where(lo, q1, zero), jnp.where(lo, zero, q1)], axis=0))
    scs = [_dot_nt(qst, kc) for qst in qsts]
    mxs = [jnp.maximum(jnp.max(sc, axis=-1, keepdims=True), sk) for sc in scs]
    if bands is not None:
        sbs = [jnp.where(jnp.concatenate([valid] * GQA_Q_HEADS, axis=0), _dot_nt(qst, kb), NEG_INF)
               for qst, (kb, _, valid) in zip(qsts, bands)]
        mxs = [jnp.maximum(mx, jnp.max(sb, axis=-1, keepdims=True)) for mx, sb in zip(mxs, sbs)]
    pcs = [jnp.exp(sc - mx) for sc, mx in zip(scs, mxs)]
    dens = [jnp.sum(pc, axis=-1, keepdims=True) + jnp.exp(sk - mx) for pc, mx in zip(pcs, mxs)]
    os_ = [_dot(pc.astype(BF16), vc) for pc in pcs]
    if bands is not None:
        pbs = [jnp.exp(sb - mx) for sb, mx in zip(sbs, mxs)]
        dens = [den + jnp.sum(pb, axis=-1, keepdims=True) for den, pb in zip(dens, pbs)]
        os_ = [o + _dot(pb.astype(BF16), vb) for o, pb, (_, vb, _) in zip(os_, pbs, bands)]
    outs = []
    for o, den in zip(os_, dens):
        o = o / den
        c0 = jnp.where(lo, o[0:tq], o[tq:2 * tq])
        c1 = jnp.where(lo, o[2 * tq:3 * tq], o[3 * tq:4 * tq])
        outs.append(jnp.concatenate([c0, c1], axis=-1))
    return outs


def _gqa_lat_kernel(q_ref, kv_ref, sink_ref, o_ref, *, lat_rows):
    kw = GQA_KV_HEADS * HEAD_DIM
    kc, vc = kv_ref[lat_rows:, :kw], kv_ref[lat_rows:, kw:]
    nbk = 3 * W_BLOCK
    n_in = q_ref.shape[0] // W_BLOCK
    qs, bands = [], []
    for i in range(n_in):
        n = pl.program_id(1) * n_in + i
        start = pl.multiple_of(jnp.clip((n - 1) * W_BLOCK, 0, lat_rows - nbk), W_BLOCK)
        kb = kv_ref[pl.ds(start, nbk), :kw]
        vb = kv_ref[pl.ds(start, nbk), kw:]
        qpos = n * W_BLOCK + lax.broadcasted_iota(jnp.int32, (W_BLOCK, nbk), 0)
        kpos = start + lax.broadcasted_iota(jnp.int32, (W_BLOCK, nbk), 1)
        bands.append((kb, vb, jnp.abs(qpos - kpos) <= WINDOW))
        qs.append(q_ref[i * W_BLOCK:(i + 1) * W_BLOCK])
    outs = _gqa_blocks(qs, kc, vc, sink_ref, bands)
    o_ref[...] = jnp.concatenate(outs, axis=0).astype(BF16)


def _gqa_ctx_kernel(q_ref, kv_ref, sink_ref, b3_ref, o_ref):
    kw = GQA_KV_HEADS * HEAD_DIM
    tq = q_ref.shape[0] // 2
    outs = _gqa_blocks([q_ref[:tq], q_ref[tq:]], kv_ref[:, :kw], kv_ref[:, kw:], sink_ref, None)
    o_ref[...] = jnp.concatenate(outs, axis=0).astype(BF16)


def _gqa_lat_call(gq, gkv, sink_rows, *, layer, lat_rows):
    b, r, w = gq.shape
    tq = GQA_Q_TILE
    return pl.pallas_call(
        functools.partial(_gqa_lat_kernel, lat_rows=lat_rows),
        grid=(b, lat_rows // tq),
        in_specs=[
            pl.BlockSpec((None, tq, w), lambda bi, j: (bi, j, 0)),
            pl.BlockSpec((None, r, w), lambda bi, j: (bi, 0, 0)),
            pl.BlockSpec((None, SUBLANES, LANES), lambda bi, j: (layer, 0, 0)),
        ],
        out_specs=pl.BlockSpec((None, tq, w), lambda bi, j: (bi, j, 0)),
        out_shape=jax.ShapeDtypeStruct((b, r, w), BF16),
        compiler_params=_cparams(("parallel", "parallel")),
        name="gqa_latent",
    )(gq, gkv, sink_rows)


def _gqa_ctx_call(gq, gkv, sink_rows, b3, *, layer, lat_rows):
    b, r, w = gq.shape
    c = r - lat_rows
    blk = lat_rows // c
    return pl.pallas_call(
        _gqa_ctx_kernel,
        grid=(b,),
        in_specs=[
            pl.BlockSpec((None, c, w), lambda bi: (bi, blk, 0)),
            pl.BlockSpec((None, c, w), lambda bi: (bi, blk, 0)),
            pl.BlockSpec((None, SUBLANES, LANES), lambda bi: (layer, 0, 0)),
            pl.BlockSpec(memory_space=pl.ANY),
        ],
        out_specs=pl.BlockSpec((None, c, w), lambda bi: (bi, blk, 0)),
        out_shape=jax.ShapeDtypeStruct((b, r, w), BF16),
        input_output_aliases={3: 0},
        compiler_params=_cparams(("parallel",)),
        name="gqa_context",
    )(gq, gkv, sink_rows, b3)


def _s5_kernel(*refs, reverse, chunk, batch):
    if reverse:
        (u_ref, pm_ref, a_ref, bcat_ref, ccat_ref, o_ref, buf_a, buf_b, u_a, u_b, st) = refs
        rb_ref = pmt_ref = dsk_ref = wglu_ref = bglu_ref = None
    else:
        (u_ref, rb_ref, pm_ref, pmt_ref, a_ref, bcat_ref, ccat_ref, dsk_ref, wglu_ref, bglu_ref,
         o_ref, buf_a, buf_b, u_a, u_b, st) = refs
    rws = chunk * batch
    n_tiles = 2 * S5_LANES // MXU_TILE
    per_tile = chunk // n_tiles

    @pl.when(pl.program_id(0) == 0)
    def _():
        buf_a[...] = jnp.zeros_like(buf_a)
        buf_b[...] = jnp.zeros_like(buf_b)
        u_a[...] = jnp.zeros_like(u_a)
        u_b[...] = jnp.zeros_like(u_b)
        st[...] = jnp.zeros_like(st)

    a_r, a_i = a_ref[0], a_ref[1]

    def stage(pos_in, pos_out, cur, oth, u_cur):
        u_old = u_cur[...]
        ub = _dot(pm_ref[...], u_ref[:, pos_in].reshape(rws, S5_CHANNELS)).astype(BF16)
        u_cur[...] = ub
        sr, si = st[:, :S5_LANES], st[:, S5_LANES:]
        racc = None
        for j in range(n_tiles):
            for k in range(per_tile):
                idx = j * per_tile + k
                off = ((chunk - 1 - idx) if reverse else idx) * batch
                nr = a_r * sr - a_i * si + oth[off:off + batch, :S5_LANES]
                ni = a_r * si + a_i * sr + oth[off:off + batch, S5_LANES:]
                oth[off:off + batch, :S5_LANES] = nr
                oth[off:off + batch, S5_LANES:] = ni
                sr, si = nr, ni
            cols = slice(j * MXU_TILE, (j + 1) * MXU_TILE)
            part = _dot(cur[:, cols].astype(BF16), ccat_ref[cols, :])
            racc = part if racc is None else racc + part
            cur[:, cols] = _dot(ub, bcat_ref[:, cols])
        st[:, :S5_LANES] = sr
        st[:, S5_LANES:] = si
        if reverse:
            o_ref[pos_out] = racc
        else:
            y = jax.nn.gelu(racc + rb_ref[pos_out] + dsk_ref[...] * u_old.astype(F32))
            z = _dot(y.astype(BF16), wglu_ref[...]) + bglu_ref[...]
            o_tb = (z[:, :S5_CHANNELS] * jax.nn.sigmoid(z[:, S5_CHANNELS:])).astype(BF16)
            o_ref[:, pos_out] = _dot(pmt_ref[...], o_tb).astype(BF16).reshape(batch, chunk, S5_CHANNELS)

    first, second = (1, 0) if reverse else (0, 1)
    stage(first, first, buf_a, buf_b, u_a)
    stage(second, second, buf_b, buf_a, u_b)


def _s5_call(u4, rb, pm, pmt, a_ri, b_cat, c_cat, d_skip, w_glu, b_glu, *, layer, reverse, lat_rows):
    batch, n_all, chunk, _ = u4.shape
    rws = chunk * batch
    n_l = lat_rows // chunk
    n_c = n_all - n_l
    assert n_l % 2 == 0 and n_c % 2 == 0
    np_all, np_l, np_c = n_all // 2, n_l // 2, n_c // 2
    dirn = 1 if reverse else 0

    def pair(g):
        g = jnp.clip(g, 0, np_all - 1)
        if reverse:
            return np_all - 1 - g
        return jnp.where(g < np_c, np_l + g, g - np_c)

    def par4(g):
        return (layer, dirn, 0, 0)

    def lyr3(g):
        return (layer, 0, 0)

    in_specs = [pl.BlockSpec((batch, 2, chunk, S5_CHANNELS), lambda g: (0, pair(g), 0, 0))]
    args = [u4]
    if not reverse:
        in_specs.append(pl.BlockSpec((2, rws, S5_CHANNELS), lambda g: (pair(g - 1), 0, 0)))
        args.append(rb)
    in_specs.append(_const_spec((rws, rws), lambda g: (0, 0)))
    args.append(pm)
    if not reverse:
        in_specs.append(_const_spec((rws, rws), lambda g: (0, 0)))
        args.append(pmt)
    in_specs += [
        _const_spec((None, None, 2, batch, S5_LANES), lambda g: (layer, dirn, 0, 0, 0)),
        _const_spec((None, None, S5_CHANNELS, 2 * S5_LANES), par4),
        _const_spec((None, None, 2 * S5_LANES, S5_CHANNELS), par4),
    ]
    args += [a_ri, b_cat, c_cat]
    if reverse:
        out_spec = pl.BlockSpec((2, rws, S5_CHANNELS), lambda g: (pair(g - 1), 0, 0))
        out_shape = jax.ShapeDtypeStruct((np_all * 2, rws, S5_CHANNELS), F32)
    else:
        in_specs += [
            _const_spec((None, 1, S5_CHANNELS), lyr3),
            _const_spec((None, S5_CHANNELS, 2 * S5_CHANNELS), lyr3),
            _const_spec((None, 1, 2 * S5_CHANNELS), lyr3),
        ]
        args += [d_skip, w_glu, b_glu]
        out_spec = pl.BlockSpec((batch, 2, chunk, S5_CHANNELS), lambda g: (0, pair(g - 1), 0, 0))
        out_shape = jax.ShapeDtypeStruct(u4.shape, BF16)
    return pl.pallas_call(
        functools.partial(_s5_kernel, reverse=reverse, chunk=chunk, batch=batch),
        grid=(np_all + 1,),
        in_specs=in_specs,
        out_specs=out_spec,
        out_shape=out_shape,
        scratch_shapes=[
            pltpu.VMEM((rws, 2 * S5_LANES), F32),
            pltpu.VMEM((rws, 2 * S5_LANES), F32),
            pltpu.VMEM((rws, S5_CHANNELS), BF16),
            pltpu.VMEM((rws, S5_CHANNELS), BF16),
            pltpu.VMEM((batch, 2 * S5_LANES), F32),
        ],
        compiler_params=_cparams(("arbitrary",)),
        name="s5_bwd" if reverse else "s5_fwd",
    )(*args)


def _merge_kernel(xn_ref, modn_ref, modnt_ref, xo_ref, modo_ref, modot_ref, gpre_ref, gpost_ref,
                  a_ref, b_ref, c_ref, d_ref, wg_ref, wb_ref, wo_ref, o_ref, h_ref, y_ref, *, tail, n_steps):
    q = pl.program_id(0)
    d = xn_ref.shape[1]
    pieces = _mod_pieces(xn_ref.shape[0], tail)

    def pre_norm():
        ms = (modn_ref[...], modnt_ref[...])
        return jnp.concatenate(
            [(_rms(xn_ref[r0:r1], gpre_ref[1:2]) * (1.0 + ms[k][4:5]) + ms[k][3:4]).astype(BF16)
             for r0, r1, k in pieces], axis=0)

    def post_norm():
        ms = (modo_ref[...], modot_ref[...])
        return jnp.concatenate(
            [xo_ref[r0:r1] + ms[k][5:6] * _rms(y_ref[r0:r1], gpost_ref[1:2]) for r0, r1, k in pieces], axis=0)

    def branch(h, i, br):
        return jax.nn.sigmoid(_dot(h, wg_ref[:, i * d:(i + 1) * d])) * _dot(br[...], wb_ref[i])

    @pl.when(q == 0)
    def _():
        h_ref[...] = pre_norm()
        y_ref[...] = jnp.zeros_like(y_ref)

    @pl.when(jnp.logical_and(q > 0, q < n_steps - 1))
    def _():
        always = q < n_steps
        h = h_ref[...]
        merged = branch(h, 0, a_ref)
        out = post_norm()
        o_ref[...] = out
        h = jnp.where(always, h, out.astype(BF16))
        for i, br in ((1, b_ref), (2, c_ref), (3, d_ref)):
            merged = merged + branch(h, i, br)
        h_new = pre_norm()
        h_ref[...] = h_new
        y_ref[...] = _dot(jnp.where(always, merged.astype(BF16), h_new), wo_ref[...])

    @pl.when(q == n_steps - 1)
    def _():
        o_ref[...] = post_norm()


def _merge_call(x3, mods, norm_pre, norm_post, a3, b3, c3, d3, wg_bf, wb_bf, wo_bf, *, layer, tm, n_blk, ctx_rows):
    b, _, d = x3.shape
    n = b * n_blk
    bj1 = _lag_block(b, n_blk, 1)

    def specs(lag):
        return _lag_specs(b, tm, d, n_blk, layer, ctx_rows > 0, lag)

    def lyr3(q):
        return (layer, 0, 0)

    branch_spec = pl.BlockSpec((None, tm, BRANCH_WIDTH), lambda q: (*bj1(q), 0))
    return pl.pallas_call(
        functools.partial(_merge_kernel, tail=ctx_rows, n_steps=n + 2),
        grid=(n + 2,),
        in_specs=specs(0) + specs(2) + [
            _const_spec((None, 3, d), lyr3),
            _const_spec((None, 3, d), lyr3),
            branch_spec, branch_spec, branch_spec, branch_spec,
            _const_spec((None, d, N_BRANCH * d), lyr3),
            _const_spec((None, N_BRANCH, BRANCH_WIDTH, d), lambda q: (layer, 0, 0, 0)),
            _const_spec((None, d, d), lyr3),
        ],
        out_specs=specs(2)[0],
        out_shape=jax.ShapeDtypeStruct((b, n_blk * tm, d), F32),
        scratch_shapes=[pltpu.VMEM((tm, d), BF16), pltpu.VMEM((tm, d), F32)],
        compiler_params=_cparams(("arbitrary",)),
        name="merge",
    )(x3, mods, mods, x3, mods, mods, norm_pre, norm_post, a3, b3, c3, d3, wg_bf, wb_bf, wo_bf)


def _rot_cols(w):
    q = w.shape[-1] // 4
    return jnp.concatenate([-w[..., q:2 * q], w[..., 0:q], -w[..., 3 * q:4 * q], w[..., 2 * q:3 * q]], axis=-1)


def _rope_full(rows_n, rot_dim):
    f32 = np.float32
    axis_dim = rot_dim // 2
    inv_freq = (f32(ROPE_THETA) ** (-np.arange(0, axis_dim, 2, dtype=f32) / f32(axis_dim))).astype(f32)
    row = np.repeat(np.arange(rows_n, dtype=f32), GRID_W)
    col = np.tile(np.arange(GRID_W, dtype=f32), rows_n)
    ang_r = row[:, None] * inv_freq[None, :]
    ang_c = col[:, None] * inv_freq[None, :]
    cos = np.concatenate([np.cos(ang_r), np.cos(ang_r), np.cos(ang_c), np.cos(ang_c)], axis=-1)
    sin = np.concatenate([np.sin(ang_r), np.sin(ang_r), np.sin(ang_c), np.sin(ang_c)], axis=-1)
    return cos.astype(f32), sin.astype(f32)


def _rope_table(l, tm):
    f32 = np.float32
    cg, sg = _rope_full(l // GRID_W, HEAD_DIM)
    cm, sm = _rope_full(l // GRID_W, MLA_ROPE)
    scale = f32((MLA_NOPE + MLA_ROPE) ** -0.5)
    pad = MLA_SLOT - MLA_NOPE - MLA_ROPE

    def build(cg, sg, cm, sm):
        n = cg.shape[0]
        ones = np.ones((n, MLA_NOPE), f32)
        zeros = np.zeros((n, MLA_NOPE), f32)
        return np.concatenate([
            cg, cg, sg, sg,
            scale * ones, scale * cm, zeros[:, :pad],
            zeros, scale * sm, zeros[:, :pad],
            cm, sm, zeros,
        ], axis=-1)

    lat = build(cg, sg, cm, sm)
    one_g, zero_g = np.ones((tm, HEAD_DIM), f32), np.zeros((tm, HEAD_DIM), f32)
    ctx = build(one_g, zero_g, one_g[:, :MLA_ROPE], zero_g[:, :MLA_ROPE])
    return jnp.asarray(np.concatenate([lat, ctx], axis=0))


GQA_HEAD_ORDER = (0, 2, 1, 3)


def _proj_cols(w):
    o = 0

    def take(n):
        nonlocal o
        v = w[..., o:o + n]
        o += n
        return v

    kvl, kpe = take(MLA_KV_LORA), take(MLA_ROPE)
    gk, gv = take(GQA_KV_HEADS * HEAD_DIM), take(GQA_KV_HEADS * HEAD_DIM)
    u, ql, gq, z = take(S5_CHANNELS), take(MLA_Q_LORA), take(GQA_Q_HEADS * HEAD_DIM), take(2 * GMLP_WIDTH)
    gate = w[..., o:]
    gqh = [gq[..., i * HEAD_DIM:(i + 1) * HEAD_DIM] for i in range(GQA_Q_HEADS)]
    gkh = [gk[..., i * HEAD_DIM:(i + 1) * HEAD_DIM] for i in range(GQA_KV_HEADS)]
    pad = jnp.zeros(w.shape[:-1] + (PC_U - PC_A - MLA_KV_LORA - 2 * MLA_ROPE,), w.dtype)
    wp = jnp.concatenate([z, kvl, kpe, _rot_cols(kpe), pad, u, ql]
                         + [gqh[i] for i in GQA_HEAD_ORDER] + [_rot_cols(gqh[i]) for i in GQA_HEAD_ORDER]
                         + [gk] + [_rot_cols(h) for h in gkh] + [gv], axis=-1)
    return wp, gate


def _wprep_kernel(w_ref, wp_ref, wg_ref):
    wp, wg = _proj_cols(w_ref[...])
    wp_ref[...] = wp.astype(BF16)
    wg_ref[...] = wg.astype(BF16)


def _wprep_call(w_in):
    depth, d, n = w_in.shape
    tr = 256
    n_gate = N_BRANCH * d
    return pl.pallas_call(
        _wprep_kernel,
        grid=(depth, d // tr),
        in_specs=[pl.BlockSpec((None, tr, n), lambda l, i: (l, i, 0))],
        out_specs=[pl.BlockSpec((None, tr, PC_END), lambda l, i: (l, i, 0)),
                   pl.BlockSpec((None, tr, n_gate), lambda l, i: (l, i, 0))],
        out_shape=[jax.ShapeDtypeStruct((depth, d, PC_END), BF16), jax.ShapeDtypeStruct((depth, d, n_gate), BF16)],
        compiler_params=_cparams(("parallel", "parallel")),
        name="w_prep",
    )(w_in)


def _mla_kv_weight(w_ukv):
    depth = w_ukv.shape[0]
    wh = w_ukv.reshape(depth, MLA_KV_LORA, MLA_HEADS, MLA_NOPE + MLA_V)
    k_nope, v = wh[..., :MLA_NOPE], wh[..., MLA_NOPE:]
    kslot = jnp.concatenate([k_nope, jnp.zeros((depth, MLA_KV_LORA, MLA_HEADS, MLA_SLOT - MLA_NOPE), F32)], axis=-1)
    top = jnp.concatenate([kslot.reshape(depth, MLA_KV_LORA, -1), v.reshape(depth, MLA_KV_LORA, -1)], axis=-1)
    eye = jnp.eye(MLA_ROPE, dtype=F32)
    pe_slot = jnp.concatenate([jnp.zeros((MLA_ROPE, MLA_NOPE), F32), eye,
                               jnp.zeros((MLA_ROPE, MLA_SLOT - MLA_NOPE - MLA_ROPE), F32)], axis=-1)
    pe_rows = jnp.concatenate([jnp.tile(pe_slot, (1, MLA_HEADS)), jnp.zeros((MLA_ROPE, MLA_HEADS * MLA_V), F32)],
                              axis=-1)
    pe_rows = jnp.broadcast_to(pe_rows, (depth,) + pe_rows.shape)
    tail = jnp.zeros((depth, 256 - MLA_KV_LORA - 2 * MLA_ROPE, MLA_KV_COLS), F32)
    return jnp.concatenate([top, pe_rows, pe_rows, tail], axis=1).astype(BF16)


def _mla_q_weight(w_uq):
    depth = w_uq.shape[0]
    wh = w_uq.reshape(depth, MLA_Q_LORA, MLA_HEADS, MLA_NOPE + MLA_ROPE)
    nope, pe = wh[..., :MLA_NOPE], wh[..., MLA_NOPE:]
    pad = jnp.zeros((depth, MLA_Q_LORA, MLA_HEADS, MLA_SLOT - MLA_NOPE - MLA_ROPE), F32)
    full = jnp.concatenate([nope, pe, pad], axis=-1).reshape(depth, MLA_Q_LORA, -1)
    rot = jnp.concatenate([jnp.zeros_like(nope), _rot_cols(pe), pad], axis=-1).reshape(depth, MLA_Q_LORA, -1)
    return jnp.concatenate([full, rot], axis=-1).astype(BF16)


def _block_diag(w):
    g, a, b = w.shape[-3:]
    lead = w.shape[:-3]
    cols = jnp.swapaxes(w, -3, -2).reshape(lead + (a, g * b))
    tiled = jnp.tile(cols, (1,) * len(lead) + (g, 1))
    same = (jnp.arange(g * a)[:, None] // a) == (jnp.arange(g * b)[None, :] // b)
    return jnp.where(same, tiled, jnp.zeros((), w.dtype))


def kernel(x, c, ctx, c_ctx, w_ada, b_ada, norm_pre, norm_post, w_ffn_in, w_ffn_out, w_in, mla_q_norm, mla_w_uq,
           mla_kv_norm, mla_w_ukv, gqa_sink, s5_lam_re, s5_lam_im, s5_log_dt, s5_b_re, s5_b_im, s5_c_re, s5_c_im,
           s5_d, s5_w_glu, s5_b_glu, gmlp_norm, gmlp_w_s, gmlp_b_s, w_branch, w_out):
    b, l, d = x.shape
    cl = ctx.shape[1]
    r = l + cl
    depth = w_ada.shape[0]
    assert b == SUBLANES and b < MOD_ROWS
    assert l % GRID_W == 0 and l >= 3 * W_BLOCK and l % cl == 0
    assert l % PROJ_TILE == 0 and cl % PROJ_TILE == 0 and l % MLA_Q_TILE == 0 and l % GQA_Q_TILE == 0
    assert l % S5_CHUNK == 0 and cl % S5_CHUNK == 0 and l % LAT_TILE == 0
    assert r % WIDE_TILE == 0 and cl <= WIDE_TILE

    wffn_in_bf = w_ffn_in.astype(BF16)
    wffn_out_bf = w_ffn_out.astype(BF16)
    wp_bf, wg_bf = _wprep_call(w_in)
    wkv_bf = _mla_kv_weight(mla_w_ukv)
    wq_bf = _mla_q_weight(mla_w_uq)
    head_order = jnp.array(GQA_HEAD_ORDER)
    wb1 = w_branch[:, 1].reshape(depth, GQA_Q_HEADS, HEAD_DIM, d)[:, head_order].reshape(depth, BRANCH_WIDTH, d)
    wb_bf = jnp.concatenate([w_branch[:, :1], wb1[:, None], w_branch[:, 2:]], axis=1).astype(BF16)
    wo_bf = w_out.astype(BF16)
    sink_rows = jnp.broadcast_to(
        jnp.concatenate([gqa_sink[:, head_order], jnp.zeros((depth, SUBLANES - GQA_Q_HEADS), F32)], axis=1)[:, :, None],
        (depth, SUBLANES, LANES))
    ws_bf = jnp.transpose(gmlp_w_s, (0, 2, 1, 3)).reshape(depth, GMLP_CHUNK, GMLP_GROUPS * GMLP_CHUNK).astype(BF16)
    bs_f = jnp.repeat(jnp.transpose(gmlp_b_s, (0, 2, 1)), GMLP_WIDTH // GMLP_GROUPS, axis=2)
    tab = _rope_table(l, PROJ_TILE)
    mla_kv_norm3 = mla_kv_norm[:, None, :]
    mla_q_norm3 = mla_q_norm[:, None, :]
    gmlp_norm3 = gmlp_norm[:, None, :]
    s5_d3 = s5_d[:, None, :]
    s5_b_glu3 = s5_b_glu[:, None, :]
    wglu_bf = s5_w_glu.astype(BF16)
    tb = jnp.arange(S5_CHUNK * b)
    pm = (tb[None, :] == ((tb % b) * S5_CHUNK + tb // b)[:, None]).astype(BF16)
    pmt = pm.T

    hg, p, g = S5_GROUP, S5_STATE, S5_GROUPS
    n_par = depth * 2 * g
    rep = lambda t: jnp.repeat(t.reshape(n_par, p), hg, axis=1)
    ldt = jnp.broadcast_to(s5_log_dt[..., None], (depth, 2, g, p))
    a_re_x, a_im_x, bb_re, bb_im = _s5_disc_call(
        rep(s5_lam_re), rep(s5_lam_im), rep(ldt), s5_b_re.reshape(n_par, p * hg), s5_b_im.reshape(n_par, p * hg))
    a_ri = jnp.broadcast_to(
        jnp.stack([a_re_x[:, ::hg].reshape(depth, 2, g * p), a_im_x[:, ::hg].reshape(depth, 2, g * p)], axis=2)
        [:, :, :, None, :], (depth, 2, 2, b, g * p))
    bb_re = jnp.swapaxes(bb_re.reshape(depth, 2, g, p, hg), -1, -2)
    bb_im = jnp.swapaxes(bb_im.reshape(depth, 2, g, p, hg), -1, -2)
    s5_b_cat = jnp.concatenate([_block_diag(bb_re), _block_diag(bb_im)], axis=-1).astype(BF16)
    s5_c_cat = jnp.concatenate([_block_diag(jnp.swapaxes(s5_c_re, -1, -2)),
                                _block_diag(jnp.swapaxes(-s5_c_im, -1, -2))], axis=-2).astype(BF16)

    cs = jnp.concatenate([c, c_ctx[None, :], jnp.zeros((MOD_ROWS - b - 1, d), F32)], axis=0)
    mods = _ada_call(cs, w_ada, b_ada[:, None, :]).reshape(depth, MOD_ROWS, N_MOD, d)

    xs = jnp.concatenate([x, ctx], axis=1)
    n_wide = r // WIDE_TILE
    for layer in range(depth):
        last = layer == depth - 1
        xs = _ffn_call(xs, mods, norm_pre, norm_post, wffn_in_bf, wffn_out_bf,
                       layer=layer, s=0, which=0, tm=WIDE_TILE, n_blk=n_wide, ctx_rows=cl)
        mq, mkv, gq, gkv, u3, d3 = _proj_call(
            xs, mods, norm_pre, wp_bf, tab, mla_kv_norm3, wkv_bf, mla_q_norm3, wq_bf, gmlp_norm3, ws_bf, bs_f,
            layer=layer, lat_rows=l)
        a3 = _mla_lat_call(mq, mkv, lat_rows=l)
        b3 = _gqa_lat_call(gq, gkv, sink_rows, layer=layer, lat_rows=l)
        u4 = u3.reshape(b, r // S5_CHUNK, S5_CHUNK, S5_CHANNELS)
        rb = _s5_call(u4, None, pm, None, a_ri, s5_b_cat, s5_c_cat, None, None, None,
                      layer=layer, reverse=True, lat_rows=l)
        c3 = _s5_call(u4, rb, pm, pmt, a_ri, s5_b_cat, s5_c_cat, s5_d3, wglu_bf, s5_b_glu3,
                      layer=layer, reverse=False, lat_rows=l).reshape(b, r, S5_CHANNELS)
        if last:
            tm, n_blk, ctx_rows = LAT_TILE, l // LAT_TILE, 0
        else:
            a3 = _mla_ctx_call(mq, mkv, a3, lat_rows=l)
            b3 = _gqa_ctx_call(gq, gkv, sink_rows, b3, layer=layer, lat_rows=l)
            tm, n_blk, ctx_rows = WIDE_TILE, n_wide, cl
        xs = _merge_call(xs, mods, norm_pre, norm_post, a3, b3, c3, d3, wg_bf, wb_bf, wo_bf,
                         layer=layer, tm=tm, n_blk=n_blk, ctx_rows=ctx_rows)
        xs = _ffn_call(xs, mods, norm_pre, norm_post, wffn_in_bf, wffn_out_bf,
                       layer=layer, s=2, which=1, tm=tm, n_blk=n_blk, ctx_rows=ctx_rows)
    return xs
```

```python
import functools

import jax
import jax.numpy as jnp
import numpy as np
from jax import lax
from jax.experimental import pallas as pl
from jax.experimental.pallas import tpu as pltpu

F32 = jnp.float32
BF16 = jnp.bfloat16

GRID_W = 64
HEAD_DIM = 64
ROPE_THETA = 10000.0
RMS_EPS = 1e-6
LN_EPS = 1e-5
NEG_INF = -1e30
MLA_HEADS = 4
MLA_Q_LORA = 256
MLA_KV_LORA = 128
MLA_NOPE = 64
MLA_ROPE = 32
MLA_V = 64
GQA_Q_HEADS = 4
GQA_KV_HEADS = 2
WINDOW = 128
W_BLOCK = 128
S5_CHANNELS = 256
S5_GROUP = 16
S5_GROUPS = S5_CHANNELS // S5_GROUP
S5_STATE = 64
GMLP_WIDTH = 256
GMLP_CHUNK = 128
GMLP_GROUPS = 4
N_BRANCH = 4
BRANCH_WIDTH = 256
N_MOD = 9

LANES = 128
SUBLANES = 8
MXU_TILE = 256
VMEM_LIMIT_BYTES = 56 * 1024 * 1024

PROJ_TILE = 256
WIDE_TILE = 768
LAT_TILE = 512
MLA_Q_TILE = 512
GQA_Q_TILE = 512
S5_CHUNK = 64
MOD_ROWS = 16
MLA_SLOT = LANES
S5_LANES = S5_GROUPS * S5_STATE

PC_Z = 0
PC_A = PC_Z + 2 * GMLP_WIDTH
PC_U = PC_A + 256
PC_QL = PC_U + S5_CHANNELS
PC_GQ = PC_QL + MLA_Q_LORA
PC_GQR = PC_GQ + GQA_Q_HEADS * HEAD_DIM
PC_GK = PC_GQR + GQA_Q_HEADS * HEAD_DIM
PC_GKR = PC_GK + GQA_KV_HEADS * HEAD_DIM
PC_GV = PC_GKR + GQA_KV_HEADS * HEAD_DIM
PC_END = PC_GV + GQA_KV_HEADS * HEAD_DIM

TC_GC, TC_GS, TC_MQC, TC_MQS, TC_MKT, TC_END = 0, 128, 256, 384, 512, 640

MLA_K_COLS = MLA_HEADS * MLA_SLOT
MLA_KV_COLS = MLA_K_COLS + MLA_HEADS * MLA_V


def _cparams(sem):
    return pltpu.CompilerParams(dimension_semantics=sem, vmem_limit_bytes=VMEM_LIMIT_BYTES)


def _const_spec(shape, index_map):
    return pl.BlockSpec(shape, index_map, pipeline_mode=pl.Buffered(1))


def _rms(x, gain):
    return x * lax.rsqrt(jnp.mean(x * x, axis=-1, keepdims=True) + RMS_EPS) * gain


def _dot(a, b):
    return jnp.dot(a, b, preferred_element_type=F32)


def _dot_nt(a, b):
    return lax.dot_general(a, b, (((1,), (1,)), ((), ())), preferred_element_type=F32)


def _ada_kernel(cs_ref, w_ref, b_ref, o_ref):
    cs = cs_ref[...]
    s = (cs * jax.nn.sigmoid(cs)).astype(BF16)
    o_ref[...] = _dot(s, w_ref[...].astype(BF16)) + b_ref[...]


def _ada_call(cs, w_ada, b_ada3):
    depth, d, n = w_ada.shape
    tn = 1024
    return pl.pallas_call(
        _ada_kernel,
        grid=(depth, n // tn),
        in_specs=[
            pl.BlockSpec((MOD_ROWS, d), lambda l, j: (0, 0)),
            pl.BlockSpec((None, d, tn), lambda l, j: (l, 0, j)),
            pl.BlockSpec((None, 1, tn), lambda l, j: (l, 0, j)),
        ],
        out_specs=pl.BlockSpec((None, MOD_ROWS, tn), lambda l, j: (l, 0, j)),
        out_shape=jax.ShapeDtypeStruct((depth, MOD_ROWS, n), F32),
        compiler_params=_cparams(("parallel", "parallel")),
        name="ada_mod",
    )(cs, w_ada, b_ada3)


def _s5_disc_kernel(lre_ref, lim_ref, ldt_ref, bre_ref, bim_ref, are_ref, aim_ref, bbre_ref, bbim_ref):
    lam_re = jnp.minimum(lre_ref[...], -1e-4)
    lam_im = lim_ref[...]
    dt = jnp.exp(ldt_ref[...])
    mag = jnp.exp(lam_re * dt)
    a_re = mag * jnp.cos(lam_im * dt)
    a_im = mag * jnp.sin(lam_im * dt)
    nr, ni = a_re - 1.0, a_im
    den = lam_re * lam_re + lam_im * lam_im
    coef_re = (nr * lam_re + ni * lam_im) / den
    coef_im = (ni * lam_re - nr * lam_im) / den
    b_re, b_im = bre_ref[...], bim_ref[...]
    are_ref[...] = a_re
    aim_ref[...] = a_im
    bbre_ref[...] = coef_re * b_re - coef_im * b_im
    bbim_ref[...] = coef_re * b_im + coef_im * b_re


def _s5_disc_call(lre, lim, ldt, bre, bim):
    shp = jax.ShapeDtypeStruct(lre.shape, F32)
    return pl.pallas_call(_s5_disc_kernel, out_shape=(shp, shp, shp, shp), name="s5_disc")(lre, lim, ldt, bre, bim)


def _mod_pieces(tm, tail):
    return ((0, tm - tail, 0), (tm - tail, tm, 1)) if tail else ((0, tm, 0),)


def _lag_block(b, n_blk, lag):
    def bj(q):
        q = jnp.clip(q - lag, 0, b * n_blk - 1)
        return q // n_blk, q % n_blk
    return bj


def _lag_specs(b, tm, d, n_blk, layer, ctx_tail, lag):
    bj = _lag_block(b, n_blk, lag)

    def tail_row(q):
        bi, j = bj(q)
        return jnp.where(j == n_blk - 1, b, bi) if ctx_tail else bi

    return [
        pl.BlockSpec((None, tm, d), lambda q: (*bj(q), 0)),
        pl.BlockSpec((None, None, N_MOD, d), lambda q: (layer, bj(q)[0], 0, 0)),
        pl.BlockSpec((None, None, N_MOD, d), lambda q: (layer, tail_row(q), 0, 0)),
    ]


def _ffn_kernel(xn_ref, modn_ref, modnt_ref, xo_ref, modo_ref, modot_ref, gpre_ref, gpost_ref, wi_ref, wo_ref,
                o_ref, h_ref, y_ref, *, s, d_ff, chunks, tail, n_steps):
    q = pl.program_id(0)
    pieces = _mod_pieces(xn_ref.shape[0], tail)

    def pre_norm():
        ms = (modn_ref[...], modnt_ref[...])
        return jnp.concatenate(
            [(_rms(xn_ref[r0:r1], gpre_ref[s:s + 1]) * (1.0 + ms[k][3 * s + 1:3 * s + 2]) + ms[k][3 * s:3 * s + 1])
             .astype(BF16) for r0, r1, k in pieces], axis=0)

    def post_norm():
        ms = (modo_ref[...], modot_ref[...])
        return jnp.concatenate(
            [xo_ref[r0:r1] + 0.5 * ms[k][3 * s + 2:3 * s + 3] * _rms(y_ref[r0:r1], gpost_ref[s:s + 1])
             for r0, r1, k in pieces], axis=0)

    def up(h, c0, c1):
        a = _dot(h, wi_ref[:, c0:c1])
        g = _dot(h, wi_ref[:, d_ff + c0:d_ff + c1])
        return (a * jax.nn.sigmoid(a) * g).astype(BF16)

    @pl.when(q == 0)
    def _():
        h_ref[...] = pre_norm()
        y_ref[...] = jnp.zeros_like(y_ref)

    @pl.when(jnp.logical_and(q > 0, q < n_steps - 1))
    def _():
        always = q < n_steps
        h = h_ref[...]
        acts = [up(h, *chunks[0])]
        out = post_norm()
        o_ref[...] = out
        h = jnp.where(always, h, out.astype(BF16))
        acts += [up(h, c0, c1) for c0, c1 in chunks[1:]]
        h_new = pre_norm()
        h_ref[...] = h_new
        w0 = chunks[0][1] - chunks[0][0]
        acts[0] = jnp.concatenate([jnp.where(always, acts[0][:, :h_new.shape[1]], h_new), acts[0][:, h_new.shape[1]:]],
                                  axis=1) if w0 > h_new.shape[1] else jnp.where(always, acts[0], h_new[:, :w0])
        y = None
        for act, (c0, c1) in zip(acts, chunks):
            part = _dot(act, wo_ref[c0:c1, :])
            y = part if y is None else y + part
        y_ref[...] = y

    @pl.when(q == n_steps - 1)
    def _():
        o_ref[...] = post_norm()


def _ffn_call(x3, mods, norm_pre, norm_post, w_in_bf, w_out_bf, *, layer, s, which, tm, n_blk, ctx_rows):
    b, _, d = x3.shape
    d_ff = w_out_bf.shape[2]
    half = (d_ff // MXU_TILE + 1) // 2 * MXU_TILE
    chunks = ((0, half), (half, d_ff)) if 0 < half < d_ff else ((0, d_ff),)
    n = b * n_blk
    kern = functools.partial(_ffn_kernel, s=s, d_ff=d_ff, chunks=chunks, tail=ctx_rows, n_steps=n + 2)

    def specs(lag):
        return _lag_specs(b, tm, d, n_blk, layer, ctx_rows > 0, lag)

    return pl.pallas_call(
        kern,
        grid=(n + 2,),
        in_specs=specs(0) + specs(2) + [
            _const_spec((None, 3, d), lambda q: (layer, 0, 0)),
            _const_spec((None, 3, d), lambda q: (layer, 0, 0)),
            _const_spec((None, None, d, 2 * d_ff), lambda q: (layer, which, 0, 0)),
            _const_spec((None, None, d_ff, d), lambda q: (layer, which, 0, 0)),
        ],
        out_specs=specs(2)[0],
        out_shape=jax.ShapeDtypeStruct((b, n_blk * tm, d), F32),
        scratch_shapes=[pltpu.VMEM((tm, d), BF16), pltpu.VMEM((tm, d), F32)],
        compiler_params=_cparams(("arbitrary",)),
        name="ffn",
    )(x3, mods, mods, x3, mods, mods, norm_pre, norm_post, w_in_bf, w_out_bf)


def _proj_kernel(x_ref, mod_ref, gpre_ref, wp_ref, tab_ref, gkvn_ref, wkv_ref, gqn_ref, wq_ref,
                 gmn_ref, ws_ref, bs_ref,
                 mq_ref, mkv_ref, gq_ref, gkv_ref, u_ref, d_ref):
    x = x_ref[...]
    m = mod_ref[...]
    h = (_rms(x, gpre_ref[1:2]) * (1.0 + m[4:5]) + m[3:4]).astype(BF16)
    p = _dot(h, wp_ref[...])
    tab = tab_ref[...]
    gc, gs = tab[:, TC_GC:TC_GS], tab[:, TC_GS:TC_MQC]
    mqc, mqs, mkt = tab[:, TC_MQC:TC_MQS], tab[:, TC_MQS:TC_MKT], tab[:, TC_MKT:TC_END]

    kvl = p[:, PC_A:PC_A + MLA_KV_LORA]
    kvn = _rms(kvl, gkvn_ref[...])
    pe = p[:, PC_A + MLA_KV_LORA:PC_U] * mkt
    a2 = jnp.concatenate([kvn, pe], axis=-1).astype(BF16)
    mkv_ref[...] = _dot(a2, wkv_ref[...]).astype(BF16)

    qn = _rms(p[:, PC_QL:PC_GQ], gqn_ref[...]).astype(BF16)
    q = _dot(qn, wq_ref[...])
    qs = [q[:, hh * MLA_SLOT:(hh + 1) * MLA_SLOT] * mqc
          + q[:, MLA_K_COLS + hh * MLA_SLOT:MLA_K_COLS + (hh + 1) * MLA_SLOT] * mqs
          for hh in range(MLA_HEADS)]
    mq_ref[...] = jnp.concatenate(qs, axis=-1).astype(BF16)

    gq = [(p[:, PC_GQ + j * LANES:PC_GQ + (j + 1) * LANES] * gc
           + p[:, PC_GQR + j * LANES:PC_GQR + (j + 1) * LANES] * gs) * (HEAD_DIM ** -0.5)
          for j in range(GQA_Q_HEADS * HEAD_DIM // LANES)]
    gq_ref[...] = jnp.concatenate(gq, axis=-1).astype(BF16)
    gk = p[:, PC_GK:PC_GKR] * gc + p[:, PC_GKR:PC_GV] * gs
    gkv_ref[...] = jnp.concatenate([gk, p[:, PC_GV:PC_END]], axis=-1).astype(BF16)

    u_ref[...] = p[:, PC_U:PC_QL].astype(BF16)

    zz = jax.nn.gelu(p[:, PC_Z:PC_A])
    ug, v = zz[:, :GMLP_WIDTH], zz[:, GMLP_WIDTH:]
    vc = v - jnp.mean(v, axis=-1, keepdims=True)
    vn = vc * lax.rsqrt(jnp.mean(vc * vc, axis=-1, keepdims=True) + LN_EPS) * gmn_ref[...]
    group_of_lane = lax.broadcasted_iota(jnp.int32, (GMLP_CHUNK, GMLP_WIDTH), 1) // (GMLP_WIDTH // GMLP_GROUPS)
    outs = []
    for ci in range(x.shape[0] // GMLP_CHUNK):
        r0 = ci * GMLP_CHUNK
        vck = vn[r0:r0 + GMLP_CHUNK]
        vbd = jnp.concatenate([jnp.where(group_of_lane == g, vck, 0.0) for g in range(GMLP_GROUPS)], axis=0)
        mixed = _dot(ws_ref[...], vbd.astype(BF16)) + bs_ref[...]
        outs.append(ug[r0:r0 + GMLP_CHUNK] * mixed)
    d_ref[...] = jnp.concatenate(outs, axis=0).astype(BF16)


def _proj_call(x3, mods, norm_pre, wp_bf, tab, mla_kv_norm3, wkv_bf, mla_q_norm3, wq_bf, gmlp_norm3, ws_bf, bs_f,
               *, layer, lat_rows):
    b, r, d = x3.shape
    tm = PROJ_TILE
    n_lat = lat_rows // tm

    def row(bi, j):
        return (bi, j, 0)

    def lyr3(bi, j):
        return (layer, 0, 0)

    widths = (MLA_K_COLS, MLA_KV_COLS, GQA_Q_HEADS * HEAD_DIM, 2 * GQA_KV_HEADS * HEAD_DIM, S5_CHANNELS, GMLP_WIDTH)
    return pl.pallas_call(
        _proj_kernel,
        grid=(b, r // tm),
        in_specs=[
            pl.BlockSpec((None, tm, d), row),
            pl.BlockSpec((None, None, N_MOD, d), lambda bi, j: (layer, jnp.where(j >= n_lat, b, bi), 0, 0)),
            _const_spec((None, 3, d), lyr3),
            _const_spec((None, d, PC_END), lyr3),
            pl.BlockSpec((tm, TC_END), lambda bi, j: (jnp.minimum(j, n_lat), 0)),
            _const_spec((None, 1, MLA_KV_LORA), lyr3),
            _const_spec((None, 256, MLA_KV_COLS), lyr3),
            _const_spec((None, 1, MLA_Q_LORA), lyr3),
            _const_spec((None, MLA_Q_LORA, 2 * MLA_K_COLS), lyr3),
            _const_spec((None, 1, GMLP_WIDTH), lyr3),
            _const_spec((None, GMLP_CHUNK, GMLP_GROUPS * GMLP_CHUNK), lyr3),
            _const_spec((None, GMLP_CHUNK, GMLP_WIDTH), lyr3),
        ],
        out_specs=[pl.BlockSpec((None, tm, w), row) for w in widths],
        out_shape=[jax.ShapeDtypeStruct((b, r, w), BF16) for w in widths],
        compiler_params=_cparams(("parallel", "parallel")),
        name="in_proj",
    )(x3, mods, norm_pre, wp_bf, tab, mla_kv_norm3, wkv_bf, mla_q_norm3, wq_bf, gmlp_norm3, ws_bf, bs_f)


def _mla_attend(q_ref, kv_ref):
    tq = q_ref.shape[0]
    head_of_lane = lax.broadcasted_iota(jnp.int32, (tq, MLA_HEADS * MLA_V), 1) // MLA_V
    acc = jnp.zeros((tq, MLA_HEADS * MLA_V), F32)
    v = kv_ref[:, MLA_K_COLS:]
    sls = [slice(hh * MLA_SLOT, (hh + 1) * MLA_SLOT) for hh in range(MLA_HEADS)]
    scs = [_dot_nt(q_ref[:, sl], kv_ref[:, sl]) for sl in sls]
    ps = [jnp.exp(sc - jnp.max(sc, axis=-1, keepdims=True)) for sc in scs]
    dens = [jnp.sum(p, axis=-1, keepdims=True) for p in ps]
    o_all = _dot(jnp.concatenate([p.astype(BF16) for p in ps], axis=0), v)
    for hh in range(MLA_HEADS):
        acc = jnp.where(head_of_lane == hh, o_all[hh * tq:(hh + 1) * tq] / dens[hh], acc)
    return acc.astype(BF16)


def _mla_lat_kernel(q_ref, kv_ref, *rest):
    rest[-1][...] = _mla_attend(q_ref, kv_ref)


def _mla_ctx_kernel(q_ref, kv_ref, o_ref, *, lat_rows):
    o_ref[:lat_rows] = jnp.zeros((lat_rows, o_ref.shape[1]), o_ref.dtype)
    o_ref[lat_rows:] = _mla_attend(q_ref, kv_ref)


def _mla_ctx_call(mq, mkv, *, lat_rows):
    b, r, _ = mq.shape
    c = r - lat_rows
    blk = lat_rows // c
    w = MLA_HEADS * MLA_V
    return pl.pallas_call(
        functools.partial(_mla_ctx_kernel, lat_rows=lat_rows),
        grid=(b,),
        in_specs=[
            pl.BlockSpec((None, c, MLA_K_COLS), lambda bi: (bi, blk, 0)),
            pl.BlockSpec((None, c, MLA_KV_COLS), lambda bi: (bi, blk, 0)),
        ],
        out_specs=pl.BlockSpec((None, r, w), lambda bi: (bi, 0, 0)),
        out_shape=jax.ShapeDtypeStruct((b, r, w), BF16),
        compiler_params=_cparams(("parallel",)),
        name="mla_context",
    )(mq, mkv)


def _mla_lat_call(mq, mkv, dst, *, lat_rows):
    b, r, _ = mq.shape
    tq = MLA_Q_TILE
    w = MLA_HEADS * MLA_V
    in_specs = [
        pl.BlockSpec((None, tq, MLA_K_COLS), lambda bi, j: (bi, j, 0)),
        pl.BlockSpec((None, r, MLA_KV_COLS), lambda bi, j: (bi, 0, 0)),
    ]
    args = [mq, mkv]
    if dst is not None:
        in_specs.append(pl.BlockSpec(memory_space=pl.ANY))
        args.append(dst)
    return pl.pallas_call(
        _mla_lat_kernel,
        grid=(b, lat_rows // tq),
        in_specs=in_specs,
        out_specs=pl.BlockSpec((None, tq, w), lambda bi, j: (bi, j, 0)),
        out_shape=jax.ShapeDtypeStruct((b, lat_rows if dst is None else r, w), BF16),
        input_output_aliases={} if dst is None else {2: 0},
        compiler_params=_cparams(("parallel", "parallel")),
        name="mla_latent",
    )(*args)


def _gqa_blocks(qs, kc, vc, sink_ref, bands):
    tq = qs[0].shape[0]
    kw = GQA_KV_HEADS * HEAD_DIM
    lo = lax.broadcasted_iota(jnp.int32, (tq, kw), 1) < HEAD_DIM
    sk = jnp.concatenate([jnp.broadcast_to(sink_ref[r:r + 1, 0:1], (tq, 1)) for r in range(GQA_Q_HEADS)], axis=0)
    qsts = []
    for q in qs:
        q0, q1 = q[:, :kw], q[:, kw:]
        zero = jnp.zeros_like(q0)
        qsts.append(jnp.concatenate([jnp.where(lo, q0, zero), jnp.where(lo, zero, q0),
                                     jnp.where(lo, q1, zero), jnp.where(lo, zero, q1)], axis=0))
    scs = [_dot_nt(qst, kc) for qst in qsts]
    mxs = [jnp.maximum(jnp.max(sc, axis=-1, keepdims=True), sk) for sc in scs]
    if bands is not None:
        sbs = [jnp.where(jnp.concatenate([valid] * GQA_Q_HEADS, axis=0), _dot_nt(qst, kb), NEG_INF)
               for qst, (kb, _, valid) in zip(qsts, bands)]
        mxs = [jnp.maximum(mx, jnp.max(sb, axis=-1, keepdims=True)) for mx, sb in zip(mxs, sbs)]
    pcs = [jnp.exp(sc - mx) for sc, mx in zip(scs, mxs)]
    dens = [jnp.sum(pc, axis=-1, keepdims=True) + jnp.exp(sk - mx) for pc, mx in zip(pcs, mxs)]
    os_ = [_dot(pc.astype(BF16), vc) for pc in pcs]
    if bands is not None:
        pbs = [jnp.exp(sb - mx) for sb, mx in zip(sbs, mxs)]
        dens = [den + jnp.sum(pb, axis=-1, keepdims=True) for den, pb in zip(dens, pbs)]
        os_ = [o + _dot(pb.astype(BF16), vb) for o, pb, (_, vb, _) in zip(os_, pbs, bands)]
    outs = []
    for o, den in zip(os_, dens):
        o = o / den
        c0 = jnp.where(lo, o[0:tq], o[tq:2 * tq])
        c1 = jnp.where(lo, o[2 * tq:3 * tq], o[3 * tq:4 * tq])
        outs.append(jnp.concatenate([c0, c1], axis=-1))
    return outs


def _gqa_lat_kernel(q_ref, kv_ref, sink_ref, *rest, lat_rows):
    o_ref = rest[-1]
    kw = GQA_KV_HEADS * HEAD_DIM
    kc, vc = kv_ref[lat_rows:, :kw], kv_ref[lat_rows:, kw:]
    nbk = 3 * W_BLOCK
    n_in = q_ref.shape[0] // W_BLOCK
    qs, bands = [], []
    for i in range(n_in):
        n = pl.program_id(1) * n_in + i
        start = pl.multiple_of(jnp.clip((n - 1) * W_BLOCK, 0, lat_rows - nbk), W_BLOCK)
        kb = kv_ref[pl.ds(start, nbk), :kw]
        vb = kv_ref[pl.ds(start, nbk), kw:]
        qpos = n * W_BLOCK + lax.broadcasted_iota(jnp.int32, (W_BLOCK, nbk), 0)
        kpos = start + lax.broadcasted_iota(jnp.int32, (W_BLOCK, nbk), 1)
        bands.append((kb, vb, jnp.abs(qpos - kpos) <= WINDOW))
        qs.append(q_ref[i * W_BLOCK:(i + 1) * W_BLOCK])
    outs = _gqa_blocks(qs, kc, vc, sink_ref, bands)
    o_ref[...] = jnp.concatenate(outs, axis=0).astype(BF16)


def _gqa_ctx_kernel(q_ref, kv_ref, sink_ref, o_ref, *, lat_rows):
    kw = GQA_KV_HEADS * HEAD_DIM
    tq = q_ref.shape[0] // 2
    outs = _gqa_blocks([q_ref[:tq], q_ref[tq:]], kv_ref[:, :kw], kv_ref[:, kw:], sink_ref, None)
    o_ref[:lat_rows] = jnp.zeros((lat_rows, o_ref.shape[1]), o_ref.dtype)
    o_ref[lat_rows:] = jnp.concatenate(outs, axis=0).astype(BF16)


def _gqa_ctx_call(gq, gkv, sink_rows, *, layer, lat_rows):
    b, r, w = gq.shape
    c = r - lat_rows
    blk = lat_rows // c
    return pl.pallas_call(
        functools.partial(_gqa_ctx_kernel, lat_rows=lat_rows),
        grid=(b,),
        in_specs=[
            pl.BlockSpec((None, c, w), lambda bi: (bi, blk, 0)),
            pl.BlockSpec((None, c, w), lambda bi: (bi, blk, 0)),
            pl.BlockSpec((None, SUBLANES, LANES), lambda bi: (layer, 0, 0)),
        ],
        out_specs=pl.BlockSpec((None, r, w), lambda bi: (bi, 0, 0)),
        out_shape=jax.ShapeDtypeStruct((b, r, w), BF16),
        compiler_params=_cparams(("parallel",)),
        name="gqa_context",
    )(gq, gkv, sink_rows)


def _gqa_lat_call(gq, gkv, sink_rows, dst, *, layer, lat_rows):
    b, r, w = gq.shape
    tq = GQA_Q_TILE
    in_specs = [
        pl.BlockSpec((None, tq, w), lambda bi, j: (bi, j, 0)),
        pl.BlockSpec((None, r, w), lambda bi, j: (bi, 0, 0)),
        pl.BlockSpec((None, SUBLANES, LANES), lambda bi, j: (layer, 0, 0)),
    ]
    args = [gq, gkv, sink_rows]
    if dst is not None:
        in_specs.append(pl.BlockSpec(memory_space=pl.ANY))
        args.append(dst)
    return pl.pallas_call(
        functools.partial(_gqa_lat_kernel, lat_rows=lat_rows),
        grid=(b, lat_rows // tq),
        in_specs=in_specs,
        out_specs=pl.BlockSpec((None, tq, w), lambda bi, j: (bi, j, 0)),
        out_shape=jax.ShapeDtypeStruct((b, lat_rows if dst is None else r, w), BF16),
        input_output_aliases={} if dst is None else {3: 0},
        compiler_params=_cparams(("parallel", "parallel")),
        name="gqa_latent",
    )(*args)


def _s5_kernel(*refs, reverse, chunk, batch):
    if reverse:
        (u_ref, pm_ref, a_ref, bcat_ref, ccat_ref, o_ref, buf_a, buf_b, u_a, u_b, st) = refs
        rb_ref = pmt_ref = dsk_ref = wglu_ref = bglu_ref = None
    else:
        (u_ref, rb_ref, pm_ref, pmt_ref, a_ref, bcat_ref, ccat_ref, dsk_ref, wglu_ref, bglu_ref,
         o_ref, buf_a, buf_b, u_a, u_b, st) = refs
    rws = chunk * batch
    n_tiles = 2 * S5_LANES // MXU_TILE
    per_tile = chunk // n_tiles

    @pl.when(pl.program_id(0) == 0)
    def _():
        buf_a[...] = jnp.zeros_like(buf_a)
        buf_b[...] = jnp.zeros_like(buf_b)
        u_a[...] = jnp.zeros_like(u_a)
        u_b[...] = jnp.zeros_like(u_b)
        st[...] = jnp.zeros_like(st)

    a_r, a_i = a_ref[0], a_ref[1]

    def stage(pos_in, pos_out, cur, oth, u_cur):
        u_old = u_cur[...]
        ub = _dot(pm_ref[...], u_ref[:, pos_in].reshape(rws, S5_CHANNELS)).astype(BF16)
        u_cur[...] = ub
        sr, si = st[:, :S5_LANES], st[:, S5_LANES:]
        racc = None
        for j in range(n_tiles):
            for k in range(per_tile):
                idx = j * per_tile + k
                off = ((chunk - 1 - idx) if reverse else idx) * batch
                nr = a_r * sr - a_i * si + oth[off:off + batch, :S5_LANES]
                ni = a_r * si + a_i * sr + oth[off:off + batch, S5_LANES:]
                oth[off:off + batch, :S5_LANES] = nr
                oth[off:off + batch, S5_LANES:] = ni
                sr, si = nr, ni
            cols = slice(j * MXU_TILE, (j + 1) * MXU_TILE)
            part = _dot(cur[:, cols].astype(BF16), ccat_ref[cols, :])
            racc = part if racc is None else racc + part
            cur[:, cols] = _dot(ub, bcat_ref[:, cols])
        st[:, :S5_LANES] = sr
        st[:, S5_LANES:] = si
        if reverse:
            o_ref[pos_out] = racc
        else:
            y = jax.nn.gelu(racc + rb_ref[pos_out] + dsk_ref[...] * u_old.astype(F32))
            z = _dot(y.astype(BF16), wglu_ref[...]) + bglu_ref[...]
            o_tb = (z[:, :S5_CHANNELS] * jax.nn.sigmoid(z[:, S5_CHANNELS:])).astype(BF16)
            o_ref[:, pos_out] = _dot(pmt_ref[...], o_tb).astype(BF16).reshape(batch, chunk, S5_CHANNELS)

    first, second = (1, 0) if reverse else (0, 1)
    stage(first, first, buf_a, buf_b, u_a)
    stage(second, second, buf_b, buf_a, u_b)


def _s5_call(u4, rb, pm, pmt, a_ri, b_cat, c_cat, d_skip, w_glu, b_glu, *, layer, reverse, lat_rows):
    batch, n_all, chunk, _ = u4.shape
    rws = chunk * batch
    n_l = lat_rows // chunk
    n_c = n_all - n_l
    assert n_l % 2 == 0 and n_c % 2 == 0
    np_all, np_l, np_c = n_all // 2, n_l // 2, n_c // 2
    dirn = 1 if reverse else 0

    def pair(g):
        g = jnp.clip(g, 0, np_all - 1)
        if reverse:
            return np_all - 1 - g
        return jnp.where(g < np_c, np_l + g, g - np_c)

    def par4(g):
        return (layer, dirn, 0, 0)

    def lyr3(g):
        return (layer, 0, 0)

    in_specs = [pl.BlockSpec((batch, 2, chunk, S5_CHANNELS), lambda g: (0, pair(g), 0, 0))]
    args = [u4]
    if not reverse:
        in_specs.append(pl.BlockSpec((2, rws, S5_CHANNELS), lambda g: (pair(g - 1), 0, 0)))
        args.append(rb)
    in_specs.append(_const_spec((rws, rws), lambda g: (0, 0)))
    args.append(pm)
    if not reverse:
        in_specs.append(_const_spec((rws, rws), lambda g: (0, 0)))
        args.append(pmt)
    in_specs += [
        _const_spec((None, None, 2, batch, S5_LANES), lambda g: (layer, dirn, 0, 0, 0)),
        _const_spec((None, None, S5_CHANNELS, 2 * S5_LANES), par4),
        _const_spec((None, None, 2 * S5_LANES, S5_CHANNELS), par4),
    ]
    args += [a_ri, b_cat, c_cat]
    if reverse:
        out_spec = pl.BlockSpec((2, rws, S5_CHANNELS), lambda g: (pair(g - 1), 0, 0))
        out_shape = jax.ShapeDtypeStruct((np_all * 2, rws, S5_CHANNELS), F32)
    else:
        in_specs += [
            _const_spec((None, 1, S5_CHANNELS), lyr3),
            _const_spec((None, S5_CHANNELS, 2 * S5_CHANNELS), lyr3),
            _const_spec((None, 1, 2 * S5_CHANNELS), lyr3),
        ]
        args += [d_skip, w_glu, b_glu]
        out_spec = pl.BlockSpec((batch, 2, chunk, S5_CHANNELS), lambda g: (0, pair(g - 1), 0, 0))
        out_shape = jax.ShapeDtypeStruct(u4.shape, BF16)
    return pl.pallas_call(
        functools.partial(_s5_kernel, reverse=reverse, chunk=chunk, batch=batch),
        grid=(np_all + 1,),
        in_specs=in_specs,
        out_specs=out_spec,
        out_shape=out_shape,
        scratch_shapes=[
            pltpu.VMEM((rws, 2 * S5_LANES), F32),
            pltpu.VMEM((rws, 2 * S5_LANES), F32),
            pltpu.VMEM((rws, S5_CHANNELS), BF16),
            pltpu.VMEM((rws, S5_CHANNELS), BF16),
            pltpu.VMEM((batch, 2 * S5_LANES), F32),
        ],
        compiler_params=_cparams(("arbitrary",)),
        name="s5_bwd" if reverse else "s5_fwd",
    )(*args)


def _merge_kernel(x_ref, mod_ref, modt_ref, gpre_ref, gpost_ref, a_ref, b_ref, c_ref, d_ref, wg_ref, wb_ref, wo_ref,
                  o_ref, *, tail):
    tm, d = x_ref.shape
    ms = (mod_ref[...], modt_ref[...])
    pieces = _mod_pieces(tm, tail)
    h = jnp.concatenate(
        [(_rms(x_ref[r0:r1], gpre_ref[1:2]) * (1.0 + ms[k][4:5]) + ms[k][3:4]).astype(BF16) for r0, r1, k in pieces],
        axis=0)
    merged = None
    for i, br in enumerate((a_ref, b_ref, c_ref, d_ref)):
        gate = jax.nn.sigmoid(_dot(h, wg_ref[:, i * d:(i + 1) * d]))
        term = gate * _dot(br[...], wb_ref[i])
        merged = term if merged is None else merged + term
    y = _dot(merged.astype(BF16), wo_ref[...])
    for r0, r1, k in pieces:
        o_ref[r0:r1] = x_ref[r0:r1] + ms[k][5:6] * _rms(y[r0:r1], gpost_ref[1:2])


def _merge_call(x3, mods, norm_pre, norm_post, a3, b3, c3, d3, wg_bf, wb_bf, wo_bf, *, layer, tm, n_blk, ctx_rows):
    b, _, d = x3.shape
    bj = _lag_block(b, n_blk, 0)

    def row(q):
        return (*bj(q), 0)

    def lyr3(q):
        return (layer, 0, 0)

    return pl.pallas_call(
        functools.partial(_merge_kernel, tail=ctx_rows),
        grid=(b * n_blk,),
        in_specs=_lag_specs(b, tm, d, n_blk, layer, ctx_rows > 0, 0) + [
            _const_spec((None, 3, d), lyr3),
            _const_spec((None, 3, d), lyr3),
            pl.BlockSpec((None, tm, BRANCH_WIDTH), row),
            pl.BlockSpec((None, tm, BRANCH_WIDTH), row),
            pl.BlockSpec((None, tm, BRANCH_WIDTH), row),
            pl.BlockSpec((None, tm, BRANCH_WIDTH), row),
            _const_spec((None, d, N_BRANCH * d), lyr3),
            _const_spec((None, N_BRANCH, BRANCH_WIDTH, d), lambda q: (layer, 0, 0, 0)),
            _const_spec((None, d, d), lyr3),
        ],
        out_specs=pl.BlockSpec((None, tm, d), row),
        out_shape=jax.ShapeDtypeStruct((b, n_blk * tm, d), F32),
        compiler_params=_cparams(("parallel",)),
        name="merge",
    )(x3, mods, mods, norm_pre, norm_post, a3, b3, c3, d3, wg_bf, wb_bf, wo_bf)


def _rot_cols(w):
    q = w.shape[-1] // 4
    return jnp.concatenate([-w[..., q:2 * q], w[..., 0:q], -w[..., 3 * q:4 * q], w[..., 2 * q:3 * q]], axis=-1)


def _rope_full(rows_n, rot_dim):
    f32 = np.float32
    axis_dim = rot_dim // 2
    inv_freq = (f32(ROPE_THETA) ** (-np.arange(0, axis_dim, 2, dtype=f32) / f32(axis_dim))).astype(f32)
    row = np.repeat(np.arange(rows_n, dtype=f32), GRID_W)
    col = np.tile(np.arange(GRID_W, dtype=f32), rows_n)
    ang_r = row[:, None] * inv_freq[None, :]
    ang_c = col[:, None] * inv_freq[None, :]
    cos = np.concatenate([np.cos(ang_r), np.cos(ang_r), np.cos(ang_c), np.cos(ang_c)], axis=-1)
    sin = np.concatenate([np.sin(ang_r), np.sin(ang_r), np.sin(ang_c), np.sin(ang_c)], axis=-1)
    return cos.astype(f32), sin.astype(f32)


def _rope_table(l, tm):
    f32 = np.float32
    cg, sg = _rope_full(l // GRID_W, HEAD_DIM)
    cm, sm = _rope_full(l // GRID_W, MLA_ROPE)
    scale = f32((MLA_NOPE + MLA_ROPE) ** -0.5)
    pad = MLA_SLOT - MLA_NOPE - MLA_ROPE

    def build(cg, sg, cm, sm):
        n = cg.shape[0]
        ones = np.ones((n, MLA_NOPE), f32)
        zeros = np.zeros((n, MLA_NOPE), f32)
        return np.concatenate([
            cg, cg, sg, sg,
            scale * ones, scale * cm, zeros[:, :pad],
            zeros, scale * sm, zeros[:, :pad],
            cm, sm, zeros,
        ], axis=-1)

    lat = build(cg, sg, cm, sm)
    one_g, zero_g = np.ones((tm, HEAD_DIM), f32), np.zeros((tm, HEAD_DIM), f32)
    ctx = build(one_g, zero_g, one_g[:, :MLA_ROPE], zero_g[:, :MLA_ROPE])
    return jnp.asarray(np.concatenate([lat, ctx], axis=0))


GQA_HEAD_ORDER = (0, 2, 1, 3)


def _proj_cols(w):
    o = 0

    def take(n):
        nonlocal o
        v = w[..., o:o + n]
        o += n
        return v

    kvl, kpe = take(MLA_KV_LORA), take(MLA_ROPE)
    gk, gv = take(GQA_KV_HEADS * HEAD_DIM), take(GQA_KV_HEADS * HEAD_DIM)
    u, ql, gq, z = take(S5_CHANNELS), take(MLA_Q_LORA), take(GQA_Q_HEADS * HEAD_DIM), take(2 * GMLP_WIDTH)
    gate = w[..., o:]
    gqh = [gq[..., i * HEAD_DIM:(i + 1) * HEAD_DIM] for i in range(GQA_Q_HEADS)]
    gkh = [gk[..., i * HEAD_DIM:(i + 1) * HEAD_DIM] for i in range(GQA_KV_HEADS)]
    pad = jnp.zeros(w.shape[:-1] + (PC_U - PC_A - MLA_KV_LORA - 2 * MLA_ROPE,), w.dtype)
    wp = jnp.concatenate([z, kvl, kpe, _rot_cols(kpe), pad, u, ql]
                         + [gqh[i] for i in GQA_HEAD_ORDER] + [_rot_cols(gqh[i]) for i in GQA_HEAD_ORDER]
                         + [gk] + [_rot_cols(h) for h in gkh] + [gv], axis=-1)
    return wp, gate


def _wprep_kernel(wt_ref, wp_ref, wg_ref):
    n = wt_ref.shape[0]
    full = n // LANES * LANES
    pieces = [wt_ref[r0:r0 + LANES, :].T for r0 in range(0, full, LANES)]
    if full < n:
        rest = jnp.concatenate([wt_ref[full:, :], jnp.zeros((LANES - (n - full), wt_ref.shape[1]), F32)], axis=0)
        pieces.append(rest.T[:, :n - full])
    wp, wg = _proj_cols(jnp.concatenate(pieces, axis=1))
    wp_ref[...] = wp.astype(BF16)
    wg_ref[...] = wg.astype(BF16)


def _wprep_call(w_in):
    depth, d, n = w_in.shape
    tr = 256
    n_gate = N_BRANCH * d
    return pl.pallas_call(
        _wprep_kernel,
        grid=(depth, d // tr),
        in_specs=[pl.BlockSpec((None, n, tr), lambda l, i: (l, 0, i))],
        out_specs=[pl.BlockSpec((None, tr, PC_END), lambda l, i: (l, i, 0)),
                   pl.BlockSpec((None, tr, n_gate), lambda l, i: (l, i, 0))],
        out_shape=[jax.ShapeDtypeStruct((depth, d, PC_END), BF16), jax.ShapeDtypeStruct((depth, d, n_gate), BF16)],
        compiler_params=_cparams(("parallel", "parallel")),
        name="w_prep",
    )(jnp.swapaxes(w_in, 1, 2))


def _mla_kv_weight(w_ukv):
    depth = w_ukv.shape[0]
    wh = w_ukv.reshape(depth, MLA_KV_LORA, MLA_HEADS, MLA_NOPE + MLA_V)
    k_nope, v = wh[..., :MLA_NOPE], wh[..., MLA_NOPE:]
    kslot = jnp.concatenate([k_nope, jnp.zeros((depth, MLA_KV_LORA, MLA_HEADS, MLA_SLOT - MLA_NOPE), F32)], axis=-1)
    top = jnp.concatenate([kslot.reshape(depth, MLA_KV_LORA, -1), v.reshape(depth, MLA_KV_LORA, -1)], axis=-1)
    eye = jnp.eye(MLA_ROPE, dtype=F32)
    pe_slot = jnp.concatenate([jnp.zeros((MLA_ROPE, MLA_NOPE), F32), eye,
                               jnp.zeros((MLA_ROPE, MLA_SLOT - MLA_NOPE - MLA_ROPE), F32)], axis=-1)
    pe_rows = jnp.concatenate([jnp.tile(pe_slot, (1, MLA_HEADS)), jnp.zeros((MLA_ROPE, MLA_HEADS * MLA_V), F32)],
                              axis=-1)
    pe_rows = jnp.broadcast_to(pe_rows, (depth,) + pe_rows.shape)
    tail = jnp.zeros((depth, 256 - MLA_KV_LORA - 2 * MLA_ROPE, MLA_KV_COLS), F32)
    return jnp.concatenate([top, pe_rows, pe_rows, tail], axis=1).astype(BF16)


def _mla_q_weight(w_uq):
    depth = w_uq.shape[0]
    wh = w_uq.reshape(depth, MLA_Q_LORA, MLA_HEADS, MLA_NOPE + MLA_ROPE)
    nope, pe = wh[..., :MLA_NOPE], wh[..., MLA_NOPE:]
    pad = jnp.zeros((depth, MLA_Q_LORA, MLA_HEADS, MLA_SLOT - MLA_NOPE - MLA_ROPE), F32)
    full = jnp.concatenate([nope, pe, pad], axis=-1).reshape(depth, MLA_Q_LORA, -1)
    rot = jnp.concatenate([jnp.zeros_like(nope), _rot_cols(pe), pad], axis=-1).reshape(depth, MLA_Q_LORA, -1)
    return jnp.concatenate([full, rot], axis=-1).astype(BF16)


def _block_diag(w):
    g, a, b = w.shape[-3:]
    lead = w.shape[:-3]
    cols = jnp.swapaxes(w, -3, -2).reshape(lead + (a, g * b))
    tiled = jnp.tile(cols, (1,) * len(lead) + (g, 1))
    same = (jnp.arange(g * a)[:, None] // a) == (jnp.arange(g * b)[None, :] // b)
    return jnp.where(same, tiled, jnp.zeros((), w.dtype))


def kernel(x, c, ctx, c_ctx, w_ada, b_ada, norm_pre, norm_post, w_ffn_in, w_ffn_out, w_in, mla_q_norm, mla_w_uq,
           mla_kv_norm, mla_w_ukv, gqa_sink, s5_lam_re, s5_lam_im, s5_log_dt, s5_b_re, s5_b_im, s5_c_re, s5_c_im,
           s5_d, s5_w_glu, s5_b_glu, gmlp_norm, gmlp_w_s, gmlp_b_s, w_branch, w_out):
    b, l, d = x.shape
    cl = ctx.shape[1]
    r = l + cl
    depth = w_ada.shape[0]
    assert b == SUBLANES and b < MOD_ROWS
    assert l % GRID_W == 0 and l >= 3 * W_BLOCK and l % cl == 0
    assert l % PROJ_TILE == 0 and cl % PROJ_TILE == 0 and l % MLA_Q_TILE == 0 and l % GQA_Q_TILE == 0
    assert l % S5_CHUNK == 0 and cl % S5_CHUNK == 0 and l % LAT_TILE == 0
    assert r % WIDE_TILE == 0 and cl <= WIDE_TILE

    wffn_in_bf = w_ffn_in.astype(BF16)
    wffn_out_bf = w_ffn_out.astype(BF16)
    wp_bf, wg_bf = _wprep_call(w_in)
    wkv_bf = _mla_kv_weight(mla_w_ukv)
    wq_bf = _mla_q_weight(mla_w_uq)
    head_order = jnp.array(GQA_HEAD_ORDER)
    wb1 = w_branch[:, 1].reshape(depth, GQA_Q_HEADS, HEAD_DIM, d)[:, head_order].reshape(depth, BRANCH_WIDTH, d)
    wb_bf = jnp.concatenate([w_branch[:, :1], wb1[:, None], w_branch[:, 2:]], axis=1).astype(BF16)
    wo_bf = w_out.astype(BF16)
    sink_rows = jnp.broadcast_to(
        jnp.concatenate([gqa_sink[:, head_order], jnp.zeros((depth, SUBLANES - GQA_Q_HEADS), F32)], axis=1)[:, :, None],
        (depth, SUBLANES, LANES))
    ws_bf = jnp.transpose(gmlp_w_s, (0, 2, 1, 3)).reshape(depth, GMLP_CHUNK, GMLP_GROUPS * GMLP_CHUNK).astype(BF16)
    bs_f = jnp.repeat(jnp.transpose(gmlp_b_s, (0, 2, 1)), GMLP_WIDTH // GMLP_GROUPS, axis=2)
    tab = _rope_table(l, PROJ_TILE)
    mla_kv_norm3 = mla_kv_norm[:, None, :]
    mla_q_norm3 = mla_q_norm[:, None, :]
    gmlp_norm3 = gmlp_norm[:, None, :]
    s5_d3 = s5_d[:, None, :]
    s5_b_glu3 = s5_b_glu[:, None, :]
    wglu_bf = s5_w_glu.astype(BF16)
    tb = jnp.arange(S5_CHUNK * b)
    pm = (tb[None, :] == ((tb % b) * S5_CHUNK + tb // b)[:, None]).astype(BF16)
    pmt = pm.T

    hg, p, g = S5_GROUP, S5_STATE, S5_GROUPS
    n_par = depth * 2 * g
    rep = lambda t: jnp.repeat(t.reshape(n_par, p), hg, axis=1)
    ldt = jnp.broadcast_to(s5_log_dt[..., None], (depth, 2, g, p))
    a_re_x, a_im_x, bb_re, bb_im = _s5_disc_call(
        rep(s5_lam_re), rep(s5_lam_im), rep(ldt), s5_b_re.reshape(n_par, p * hg), s5_b_im.reshape(n_par, p * hg))
    a_ri = jnp.broadcast_to(
        jnp.stack([a_re_x[:, ::hg].reshape(depth, 2, g * p), a_im_x[:, ::hg].reshape(depth, 2, g * p)], axis=2)
        [:, :, :, None, :], (depth, 2, 2, b, g * p))
    bb_re = jnp.swapaxes(bb_re.reshape(depth, 2, g, p, hg), -1, -2)
    bb_im = jnp.swapaxes(bb_im.reshape(depth, 2, g, p, hg), -1, -2)
    s5_b_cat = jnp.concatenate([_block_diag(bb_re), _block_diag(bb_im)], axis=-1).astype(BF16)
    s5_c_cat = jnp.concatenate([_block_diag(jnp.swapaxes(s5_c_re, -1, -2)),
                                _block_diag(jnp.swapaxes(-s5_c_im, -1, -2))], axis=-2).astype(BF16)

    cs = jnp.concatenate([c, c_ctx[None, :], jnp.zeros((MOD_ROWS - b - 1, d), F32)], axis=0)
    mods = _ada_call(cs, w_ada, b_ada[:, None, :]).reshape(depth, MOD_ROWS, N_MOD, d)

    xs = jnp.concatenate([x, ctx], axis=1)
    n_wide = r // WIDE_TILE
    for layer in range(depth):
        last = layer == depth - 1
        xs = _ffn_call(xs, mods, norm_pre, norm_post, wffn_in_bf, wffn_out_bf,
                       layer=layer, s=0, which=0, tm=WIDE_TILE, n_blk=n_wide, ctx_rows=cl)
        mq, mkv, gq, gkv, u3, d3 = _proj_call(
            xs, mods, norm_pre, wp_bf, tab, mla_kv_norm3, wkv_bf, mla_q_norm3, wq_bf, gmlp_norm3, ws_bf, bs_f,
            layer=layer, lat_rows=l)
        a_dst = None if last else _mla_ctx_call(mq, mkv, lat_rows=l)
        b_dst = None if last else _gqa_ctx_call(gq, gkv, sink_rows, layer=layer, lat_rows=l)
        a3 = _mla_lat_call(mq, mkv, a_dst, lat_rows=l)
        b3 = _gqa_lat_call(gq, gkv, sink_rows, b_dst, layer=layer, lat_rows=l)
        u4 = u3.reshape(b, r // S5_CHUNK, S5_CHUNK, S5_CHANNELS)
        rb = _s5_call(u4, None, pm, None, a_ri, s5_b_cat, s5_c_cat, None, None, None,
                      layer=layer, reverse=True, lat_rows=l)
        c3 = _s5_call(u4, rb, pm, pmt, a_ri, s5_b_cat, s5_c_cat, s5_d3, wglu_bf, s5_b_glu3,
                      layer=layer, reverse=False, lat_rows=l).reshape(b, r, S5_CHANNELS)
        if last:
            tm, n_blk, ctx_rows = LAT_TILE, l // LAT_TILE, 0
        else:
            tm, n_blk, ctx_rows = WIDE_TILE, n_wide, cl
        xs = _merge_call(xs, mods, norm_pre, norm_post, a3, b3, c3, d3, wg_bf, wb_bf, wo_bf,
                         layer=layer, tm=tm, n_blk=n_blk, ctx_rows=ctx_rows)
        xs = _ffn_call(xs, mods, norm_pre, norm_post, wffn_in_bf, wffn_out_bf,
                       layer=layer, s=2, which=1, tm=tm, n_blk=n_blk, ctx_rows=ctx_rows)
    return xs
```

```python
import functools

import jax
import jax.numpy as jnp
import numpy as np
from jax import lax
from jax.experimental import pallas as pl
from jax.experimental.pallas import tpu as pltpu

F32 = jnp.float32
BF16 = jnp.bfloat16

GRID_W = 64
HEAD_DIM = 64
ROPE_THETA = 10000.0
RMS_EPS = 1e-6
LN_EPS = 1e-5
NEG_INF = -1e30
MLA_HEADS = 4
MLA_Q_LORA = 256
MLA_KV_LORA = 128
MLA_NOPE = 64
MLA_ROPE = 32
MLA_V = 64
GQA_Q_HEADS = 4
GQA_KV_HEADS = 2
WINDOW = 128
W_BLOCK = 128
S5_CHANNELS = 256
S5_GROUP = 16
S5_GROUPS = S5_CHANNELS // S5_GROUP
S5_STATE = 64
GMLP_WIDTH = 256
GMLP_CHUNK = 128
GMLP_GROUPS = 4
N_BRANCH = 4
BRANCH_WIDTH = 256
N_MOD = 9

LANES = 128
SUBLANES = 8
MXU_TILE = 256
VMEM_LIMIT_BYTES = 56 * 1024 * 1024

WIDE_TILE = 768
LAT_TILE = 512
MLA_Q_TILE = 512
GQA_Q_TILE = 512
S5_CHUNK = 64
MOD_ROWS = 16
MLA_SLOT = LANES
S5_LANES = S5_GROUPS * S5_STATE

PC_Z = 0
PC_A = PC_Z + 2 * GMLP_WIDTH
PC_U = PC_A + 256
PC_QL = PC_U + S5_CHANNELS
PC_GQ = PC_QL + MLA_Q_LORA
PC_GQR = PC_GQ + GQA_Q_HEADS * HEAD_DIM
PC_GK = PC_GQR + GQA_Q_HEADS * HEAD_DIM
PC_GKR = PC_GK + GQA_KV_HEADS * HEAD_DIM
PC_GV = PC_GKR + GQA_KV_HEADS * HEAD_DIM
PC_END = PC_GV + GQA_KV_HEADS * HEAD_DIM

TC_GC, TC_GS, TC_MQC, TC_MQS, TC_MKT, TC_END = 0, 128, 256, 384, 512, 640

MLA_K_COLS = MLA_HEADS * MLA_SLOT
MLA_KV_COLS = MLA_K_COLS + MLA_HEADS * MLA_V


def _cparams(sem):
    return pltpu.CompilerParams(dimension_semantics=sem, vmem_limit_bytes=VMEM_LIMIT_BYTES)


def _const_spec(shape, index_map):
    return pl.BlockSpec(shape, index_map, pipeline_mode=pl.Buffered(1))


def _rms(x, gain):
    return x * lax.rsqrt(jnp.mean(x * x, axis=-1, keepdims=True) + RMS_EPS) * gain


def _dot(a, b):
    return jnp.dot(a, b, preferred_element_type=F32)


def _dot_nt(a, b):
    return lax.dot_general(a, b, (((1,), (1,)), ((), ())), preferred_element_type=F32)


def _ada_kernel(cs_ref, w_ref, b_ref, o_ref):
    cs = cs_ref[...]
    s = (cs * jax.nn.sigmoid(cs)).astype(BF16)
    o_ref[...] = _dot(s, w_ref[...].astype(BF16)) + b_ref[...]


def _ada_call(cs, w_ada, b_ada3):
    depth, d, n = w_ada.shape
    tn = n // 4 if n % (4 * LANES) == 0 else n
    return pl.pallas_call(
        _ada_kernel,
        grid=(depth, n // tn),
        in_specs=[
            pl.BlockSpec((MOD_ROWS, d), lambda l, j: (0, 0)),
            pl.BlockSpec((None, d, tn), lambda l, j: (l, 0, j)),
            pl.BlockSpec((None, 1, tn), lambda l, j: (l, 0, j)),
        ],
        out_specs=pl.BlockSpec((None, MOD_ROWS, tn), lambda l, j: (l, 0, j)),
        out_shape=jax.ShapeDtypeStruct((depth, MOD_ROWS, n), F32),
        compiler_params=_cparams(("parallel", "parallel")),
        name="ada_mod",
    )(cs, w_ada, b_ada3)


def _s5_disc_kernel(lre_ref, lim_ref, ldt_ref, bre_ref, bim_ref, are_ref, aim_ref, bbre_ref, bbim_ref):
    lam_re = jnp.minimum(lre_ref[...], -1e-4)
    lam_im = lim_ref[...]
    dt = jnp.exp(ldt_ref[...])
    mag = jnp.exp(lam_re * dt)
    a_re = mag * jnp.cos(lam_im * dt)
    a_im = mag * jnp.sin(lam_im * dt)
    nr, ni = a_re - 1.0, a_im
    den = lam_re * lam_re + lam_im * lam_im
    coef_re = (nr * lam_re + ni * lam_im) / den
    coef_im = (ni * lam_re - nr * lam_im) / den
    b_re, b_im = bre_ref[...], bim_ref[...]
    are_ref[...] = a_re
    aim_ref[...] = a_im
    bbre_ref[...] = coef_re * b_re - coef_im * b_im
    bbim_ref[...] = coef_re * b_im + coef_im * b_re


def _s5_disc_call(lre, lim, ldt, bre, bim):
    shp = jax.ShapeDtypeStruct(lre.shape, F32)
    return pl.pallas_call(_s5_disc_kernel, out_shape=(shp, shp, shp, shp), name="s5_disc")(lre, lim, ldt, bre, bim)


def _mod_pieces(tm, tail):
    return ((0, tm - tail, 0), (tm - tail, tm, 1)) if tail else ((0, tm, 0),)


def _lag_block(b, n_blk, lag):
    def bj(q):
        q = jnp.clip(q - lag, 0, b * n_blk - 1)
        return q // n_blk, q % n_blk
    return bj


def _lag_specs(b, tm, d, n_blk, layer, ctx_tail, lag):
    bj = _lag_block(b, n_blk, lag)

    def tail_row(q):
        bi, j = bj(q)
        return jnp.where(j == n_blk - 1, b, bi) if ctx_tail else bi

    return [
        pl.BlockSpec((None, tm, d), lambda q: (*bj(q), 0)),
        pl.BlockSpec((None, None, N_MOD, d), lambda q: (layer, bj(q)[0], 0, 0)),
        pl.BlockSpec((None, None, N_MOD, d), lambda q: (layer, tail_row(q), 0, 0)),
    ]


def _ffn_kernel(xn_ref, modn_ref, modnt_ref, xo_ref, modo_ref, modot_ref, gpre_ref, gpost_ref, wi_ref, wo_ref,
                o_ref, h_ref, y_ref, *, s, d_ff, chunks, tail, n_steps):
    q = pl.program_id(0)
    pieces = _mod_pieces(xn_ref.shape[0], tail)

    def pre_norm():
        ms = (modn_ref[...], modnt_ref[...])
        return jnp.concatenate(
            [(_rms(xn_ref[r0:r1], gpre_ref[s:s + 1]) * (1.0 + ms[k][3 * s + 1:3 * s + 2]) + ms[k][3 * s:3 * s + 1])
             .astype(BF16) for r0, r1, k in pieces], axis=0)

    def post_norm():
        ms = (modo_ref[...], modot_ref[...])
        return jnp.concatenate(
            [xo_ref[r0:r1] + 0.5 * ms[k][3 * s + 2:3 * s + 3] * _rms(y_ref[r0:r1], gpost_ref[s:s + 1])
             for r0, r1, k in pieces], axis=0)

    def up(h, c0, c1):
        a = _dot(h, wi_ref[:, c0:c1])
        g = _dot(h, wi_ref[:, d_ff + c0:d_ff + c1])
        return (a * jax.nn.sigmoid(a) * g).astype(BF16)

    @pl.when(q == 0)
    def _():
        h_ref[...] = pre_norm()
        y_ref[...] = jnp.zeros_like(y_ref)

    @pl.when(jnp.logical_and(q > 0, q < n_steps - 1))
    def _():
        always = q < n_steps
        h = h_ref[...]
        acts = [up(h, *chunks[0])]
        out = post_norm()
        o_ref[...] = out
        h = jnp.where(always, h, out.astype(BF16))
        acts += [up(h, c0, c1) for c0, c1 in chunks[1:]]
        h_new = pre_norm()
        h_ref[...] = h_new
        w0 = chunks[0][1] - chunks[0][0]
        acts[0] = jnp.concatenate([jnp.where(always, acts[0][:, :h_new.shape[1]], h_new), acts[0][:, h_new.shape[1]:]],
                                  axis=1) if w0 > h_new.shape[1] else jnp.where(always, acts[0], h_new[:, :w0])
        y = None
        for act, (c0, c1) in zip(acts, chunks):
            part = _dot(act, wo_ref[c0:c1, :])
            y = part if y is None else y + part
        y_ref[...] = y

    @pl.when(q == n_steps - 1)
    def _():
        o_ref[...] = post_norm()


def _ffn_call(x3, mods, norm_pre, norm_post, w_in_bf, w_out_bf, *, layer, s, which, tm, n_blk, ctx_rows):
    b, _, d = x3.shape
    d_ff = w_out_bf.shape[2]
    half = (d_ff // MXU_TILE + 1) // 2 * MXU_TILE
    chunks = ((0, half), (half, d_ff)) if 0 < half < d_ff else ((0, d_ff),)
    n = b * n_blk
    kern = functools.partial(_ffn_kernel, s=s, d_ff=d_ff, chunks=chunks, tail=ctx_rows, n_steps=n + 2)

    def specs(lag):
        return _lag_specs(b, tm, d, n_blk, layer, ctx_rows > 0, lag)

    return pl.pallas_call(
        kern,
        grid=(n + 2,),
        in_specs=specs(0) + specs(2) + [
            _const_spec((None, 3, d), lambda q: (layer, 0, 0)),
            _const_spec((None, 3, d), lambda q: (layer, 0, 0)),
            _const_spec((None, None, d, 2 * d_ff), lambda q: (layer, which, 0, 0)),
            _const_spec((None, None, d_ff, d), lambda q: (layer, which, 0, 0)),
        ],
        out_specs=specs(2)[0],
        out_shape=jax.ShapeDtypeStruct((b, n_blk * tm, d), F32),
        scratch_shapes=[pltpu.VMEM((tm, d), BF16), pltpu.VMEM((tm, d), F32)],
        compiler_params=_cparams(("arbitrary",)),
        name="ffn",
    )(x3, mods, mods, x3, mods, mods, norm_pre, norm_post, w_in_bf, w_out_bf)


def _proj_kernel(x_ref, mod_ref, modt_ref, gpre_ref, wp_ref, tab_ref, gkvn_ref, wkv_ref, gqn_ref, wq_ref,
                 gmn_ref, ws_ref, bs_ref,
                 mq_ref, mkv_ref, gq_ref, gkv_ref, u_ref, d_ref, *, tail):
    ms = (mod_ref[...], modt_ref[...])
    h = jnp.concatenate(
        [(_rms(x_ref[r0:r1], gpre_ref[1:2]) * (1.0 + ms[k][4:5]) + ms[k][3:4]).astype(BF16)
         for r0, r1, k in _mod_pieces(x_ref.shape[0], tail)], axis=0)
    p = _dot(h, wp_ref[...])
    tab = tab_ref[...]
    gc, gs = tab[:, TC_GC:TC_GS], tab[:, TC_GS:TC_MQC]
    mqc, mqs, mkt = tab[:, TC_MQC:TC_MQS], tab[:, TC_MQS:TC_MKT], tab[:, TC_MKT:TC_END]

    kvl = p[:, PC_A:PC_A + MLA_KV_LORA]
    kvn = _rms(kvl, gkvn_ref[...])
    pe = p[:, PC_A + MLA_KV_LORA:PC_U] * mkt
    a2 = jnp.concatenate([kvn, pe], axis=-1).astype(BF16)
    mkv_ref[...] = _dot(a2, wkv_ref[...]).astype(BF16)

    qn = _rms(p[:, PC_QL:PC_GQ], gqn_ref[...]).astype(BF16)
    q = _dot(qn, wq_ref[...])
    qs = [q[:, hh * MLA_SLOT:(hh + 1) * MLA_SLOT] * mqc
          + q[:, MLA_K_COLS + hh * MLA_SLOT:MLA_K_COLS + (hh + 1) * MLA_SLOT] * mqs
          for hh in range(MLA_HEADS)]
    mq_ref[...] = jnp.concatenate(qs, axis=-1).astype(BF16)

    gq = [(p[:, PC_GQ + j * LANES:PC_GQ + (j + 1) * LANES] * gc
           + p[:, PC_GQR + j * LANES:PC_GQR + (j + 1) * LANES] * gs) * (HEAD_DIM ** -0.5)
          for j in range(GQA_Q_HEADS * HEAD_DIM // LANES)]
    gq_ref[...] = jnp.concatenate(gq, axis=-1).astype(BF16)
    gk = p[:, PC_GK:PC_GKR] * gc + p[:, PC_GKR:PC_GV] * gs
    gkv_ref[...] = jnp.concatenate([gk, p[:, PC_GV:PC_END]], axis=-1).astype(BF16)

    u_ref[...] = p[:, PC_U:PC_QL].astype(BF16)

    zz = jax.nn.gelu(p[:, PC_Z:PC_A])
    ug, v = zz[:, :GMLP_WIDTH], zz[:, GMLP_WIDTH:]
    vc = v - jnp.mean(v, axis=-1, keepdims=True)
    vn = vc * lax.rsqrt(jnp.mean(vc * vc, axis=-1, keepdims=True) + LN_EPS) * gmn_ref[...]
    group_of_lane = lax.broadcasted_iota(jnp.int32, (GMLP_CHUNK, GMLP_WIDTH), 1) // (GMLP_WIDTH // GMLP_GROUPS)
    outs = []
    for ci in range(x_ref.shape[0] // GMLP_CHUNK):
        r0 = ci * GMLP_CHUNK
        vck = vn[r0:r0 + GMLP_CHUNK]
        vbd = jnp.concatenate([jnp.where(group_of_lane == g, vck, 0.0) for g in range(GMLP_GROUPS)], axis=0)
        mixed = _dot(ws_ref[...], vbd.astype(BF16)) + bs_ref[...]
        outs.append(ug[r0:r0 + GMLP_CHUNK] * mixed)
    d_ref[...] = jnp.concatenate(outs, axis=0).astype(BF16)


def _proj_call(x3, mods, norm_pre, wp_bf, tab, mla_kv_norm3, wkv_bf, mla_q_norm3, wq_bf, gmlp_norm3, ws_bf, bs_f,
               *, layer, lat_rows):
    b, r, d = x3.shape
    tm = WIDE_TILE
    n_blk = r // tm
    bj = _lag_block(b, n_blk, 0)

    def row(q):
        return (*bj(q), 0)

    def lyr3(q):
        return (layer, 0, 0)

    widths = (MLA_K_COLS, MLA_KV_COLS, GQA_Q_HEADS * HEAD_DIM, 2 * GQA_KV_HEADS * HEAD_DIM, S5_CHANNELS, GMLP_WIDTH)
    return pl.pallas_call(
        functools.partial(_proj_kernel, tail=r - lat_rows),
        grid=(b * n_blk,),
        in_specs=_lag_specs(b, tm, d, n_blk, layer, True, 0) + [
            _const_spec((None, 3, d), lyr3),
            _const_spec((None, d, PC_END), lyr3),
            pl.BlockSpec((tm, TC_END), lambda q: (bj(q)[1], 0)),
            _const_spec((None, 1, MLA_KV_LORA), lyr3),
            _const_spec((None, 256, MLA_KV_COLS), lyr3),
            _const_spec((None, 1, MLA_Q_LORA), lyr3),
            _const_spec((None, MLA_Q_LORA, 2 * MLA_K_COLS), lyr3),
            _const_spec((None, 1, GMLP_WIDTH), lyr3),
            _const_spec((None, GMLP_CHUNK, GMLP_GROUPS * GMLP_CHUNK), lyr3),
            _const_spec((None, GMLP_CHUNK, GMLP_WIDTH), lyr3),
        ],
        out_specs=[pl.BlockSpec((None, tm, w), row) for w in widths],
        out_shape=[jax.ShapeDtypeStruct((b, r, w), BF16) for w in widths],
        compiler_params=_cparams(("parallel",)),
        name="in_proj",
    )(x3, mods, mods, norm_pre, wp_bf, tab, mla_kv_norm3, wkv_bf, mla_q_norm3, wq_bf, gmlp_norm3, ws_bf, bs_f)


def _mla_attend(q_ref, kv_ref):
    tq = q_ref.shape[0]
    head_of_lane = lax.broadcasted_iota(jnp.int32, (tq, MLA_HEADS * MLA_V), 1) // MLA_V
    acc = jnp.zeros((tq, MLA_HEADS * MLA_V), F32)
    v = kv_ref[:, MLA_K_COLS:]
    sls = [slice(hh * MLA_SLOT, (hh + 1) * MLA_SLOT) for hh in range(MLA_HEADS)]
    scs = [_dot_nt(q_ref[:, sl], kv_ref[:, sl]) for sl in sls]
    ps = [jnp.exp(sc - jnp.max(sc, axis=-1, keepdims=True)) for sc in scs]
    dens = [jnp.sum(p, axis=-1, keepdims=True) for p in ps]
    o_all = _dot(jnp.concatenate([p.astype(BF16) for p in ps], axis=0), v)
    for hh in range(MLA_HEADS):
        acc = jnp.where(head_of_lane == hh, o_all[hh * tq:(hh + 1) * tq] / dens[hh], acc)
    return acc.astype(BF16)


def _mla_lat_kernel(q_ref, kv_ref, *rest):
    rest[-1][...] = _mla_attend(q_ref, kv_ref)


def _mla_ctx_kernel(q_ref, kv_ref, o_ref, *, lat_rows):
    o_ref[:lat_rows] = jnp.zeros((lat_rows, o_ref.shape[1]), o_ref.dtype)
    o_ref[lat_rows:] = _mla_attend(q_ref, kv_ref)


def _mla_ctx_call(mq, mkv, *, lat_rows):
    b, r, _ = mq.shape
    c = r - lat_rows
    blk = lat_rows // c
    w = MLA_HEADS * MLA_V
    return pl.pallas_call(
        functools.partial(_mla_ctx_kernel, lat_rows=lat_rows),
        grid=(b,),
        in_specs=[
            pl.BlockSpec((None, c, MLA_K_COLS), lambda bi: (bi, blk, 0)),
            pl.BlockSpec((None, c, MLA_KV_COLS), lambda bi: (bi, blk, 0)),
        ],
        out_specs=pl.BlockSpec((None, r, w), lambda bi: (bi, 0, 0)),
        out_shape=jax.ShapeDtypeStruct((b, r, w), BF16),
        compiler_params=_cparams(("parallel",)),
        name="mla_context",
    )(mq, mkv)


def _mla_lat_call(mq, mkv, dst, *, lat_rows):
    b, r, _ = mq.shape
    tq = MLA_Q_TILE
    w = MLA_HEADS * MLA_V
    in_specs = [
        pl.BlockSpec((None, tq, MLA_K_COLS), lambda bi, j: (bi, j, 0)),
        pl.BlockSpec((None, r, MLA_KV_COLS), lambda bi, j: (bi, 0, 0)),
    ]
    args = [mq, mkv]
    if dst is not None:
        in_specs.append(pl.BlockSpec(memory_space=pl.ANY))
        args.append(dst)
    return pl.pallas_call(
        _mla_lat_kernel,
        grid=(b, lat_rows // tq),
        in_specs=in_specs,
        out_specs=pl.BlockSpec((None, tq, w), lambda bi, j: (bi, j, 0)),
        out_shape=jax.ShapeDtypeStruct((b, lat_rows if dst is None else r, w), BF16),
        input_output_aliases={} if dst is None else {2: 0},
        compiler_params=_cparams(("parallel", "parallel")),
        name="mla_latent",
    )(*args)


def _gqa_blocks(qs, kc, vc, sink_ref, bands):
    tq = qs[0].shape[0]
    kw = GQA_KV_HEADS * HEAD_DIM
    lo = lax.broadcasted_iota(jnp.int32, (tq, kw), 1) < HEAD_DIM
    sk = jnp.concatenate([jnp.broadcast_to(sink_ref[r:r + 1, 0:1], (tq, 1)) for r in range(GQA_Q_HEADS)], axis=0)
    qsts = []
    for q in qs:
        q0, q1 = q[:, :kw], q[:, kw:]
        zero = jnp.zeros_like(q0)
        qsts.append(jnp.concatenate([jnp.where(lo, q0, zero), jnp.where(lo, zero, q0),
                                     jnp.where(lo, q1, zero), jnp.where(lo, zero, q1)], axis=0))
    scs = [_dot_nt(qst, kc) for qst in qsts]
    mxs = [jnp.maximum(jnp.max(sc, axis=-1, keepdims=True), sk) for sc in scs]
    if bands is not None:
        sbs = [jnp.where(jnp.concatenate([valid] * GQA_Q_HEADS, axis=0), _dot_nt(qst, kb), NEG_INF)
               for qst, (kb, _, valid) in zip(qsts, bands)]
        mxs = [jnp.maximum(mx, jnp.max(sb, axis=-1, keepdims=True)) for mx, sb in zip(mxs, sbs)]
    pcs = [jnp.exp(sc - mx) for sc, mx in zip(scs, mxs)]
    dens = [jnp.sum(pc, axis=-1, keepdims=True) + jnp.exp(sk - mx) for pc, mx in zip(pcs, mxs)]
    os_ = [_dot(pc.astype(BF16), vc) for pc in pcs]
    if bands is not None:
        pbs = [jnp.exp(sb - mx) for sb, mx in zip(sbs, mxs)]
        dens = [den + jnp.sum(pb, axis=-1, keepdims=True) for den, pb in zip(dens, pbs)]
        os_ = [o + _dot(pb.astype(BF16), vb) for o, pb, (_, vb, _) in zip(os_, pbs, bands)]
    outs = []
    for o, den in zip(os_, dens):
        o = o / den
        c0 = jnp.where(lo, o[0:tq], o[tq:2 * tq])
        c1 = jnp.where(lo, o[2 * tq:3 * tq], o[3 * tq:4 * tq])
        outs.append(jnp.concatenate([c0, c1], axis=-1))
    return outs


def _gqa_lat_kernel(q_ref, kv_ref, sink_ref, *rest, lat_rows):
    o_ref = rest[-1]
    kw = GQA_KV_HEADS * HEAD_DIM
    kc, vc = kv_ref[lat_rows:, :kw], kv_ref[lat_rows:, kw:]
    nbk = 3 * W_BLOCK
    n_in = q_ref.shape[0] // W_BLOCK
    qs, bands = [], []
    for i in range(n_in):
        n = pl.program_id(1) * n_in + i
        start = pl.multiple_of(jnp.clip((n - 1) * W_BLOCK, 0, lat_rows - nbk), W_BLOCK)
        kb = kv_ref[pl.ds(start, nbk), :kw]
        vb = kv_ref[pl.ds(start, nbk), kw:]
        qpos = n * W_BLOCK + lax.broadcasted_iota(jnp.int32, (W_BLOCK, nbk), 0)
        kpos = start + lax.broadcasted_iota(jnp.int32, (W_BLOCK, nbk), 1)
        bands.append((kb, vb, jnp.abs(qpos - kpos) <= WINDOW))
        qs.append(q_ref[i * W_BLOCK:(i + 1) * W_BLOCK])
    outs = _gqa_blocks(qs, kc, vc, sink_ref, bands)
    o_ref[...] = jnp.concatenate(outs, axis=0).astype(BF16)


def _gqa_ctx_kernel(q_ref, kv_ref, sink_ref, o_ref, *, lat_rows):
    kw = GQA_KV_HEADS * HEAD_DIM
    tq = q_ref.shape[0] // 2
    outs = _gqa_blocks([q_ref[:tq], q_ref[tq:]], kv_ref[:, :kw], kv_ref[:, kw:], sink_ref, None)
    o_ref[:lat_rows] = jnp.zeros((lat_rows, o_ref.shape[1]), o_ref.dtype)
    o_ref[lat_rows:] = jnp.concatenate(outs, axis=0).astype(BF16)


def _gqa_ctx_call(gq, gkv, sink_rows, *, layer, lat_rows):
    b, r, w = gq.shape
    c = r - lat_rows
    blk = lat_rows // c
    return pl.pallas_call(
        functools.partial(_gqa_ctx_kernel, lat_rows=lat_rows),
        grid=(b,),
        in_specs=[
            pl.BlockSpec((None, c, w), lambda bi: (bi, blk, 0)),
            pl.BlockSpec((None, c, w), lambda bi: (bi, blk, 0)),
            pl.BlockSpec((None, SUBLANES, LANES), lambda bi: (layer, 0, 0)),
        ],
        out_specs=pl.BlockSpec((None, r, w), lambda bi: (bi, 0, 0)),
        out_shape=jax.ShapeDtypeStruct((b, r, w), BF16),
        compiler_params=_cparams(("parallel",)),
        name="gqa_context",
    )(gq, gkv, sink_rows)


def _gqa_lat_call(gq, gkv, sink_rows, dst, *, layer, lat_rows):
    b, r, w = gq.shape
    tq = GQA_Q_TILE
    in_specs = [
        pl.BlockSpec((None, tq, w), lambda bi, j: (bi, j, 0)),
        pl.BlockSpec((None, r, w), lambda bi, j: (bi, 0, 0)),
        pl.BlockSpec((None, SUBLANES, LANES), lambda bi, j: (layer, 0, 0)),
    ]
    args = [gq, gkv, sink_rows]
    if dst is not None:
        in_specs.append(pl.BlockSpec(memory_space=pl.ANY))
        args.append(dst)
    return pl.pallas_call(
        functools.partial(_gqa_lat_kernel, lat_rows=lat_rows),
        grid=(b, lat_rows // tq),
        in_specs=in_specs,
        out_specs=pl.BlockSpec((None, tq, w), lambda bi, j: (bi, j, 0)),
        out_shape=jax.ShapeDtypeStruct((b, lat_rows if dst is None else r, w), BF16),
        input_output_aliases={} if dst is None else {3: 0},
        compiler_params=_cparams(("parallel", "parallel")),
        name="gqa_latent",
    )(*args)


def _s5_kernel(*refs, reverse, chunk, batch):
    if reverse:
        (u_ref, pm_ref, a_ref, bcat_ref, ccat_ref, o_ref, buf_a, buf_b, u_a, u_b, st) = refs
        rb_ref = pmt_ref = dsk_ref = wglu_ref = bglu_ref = None
    else:
        (u_ref, rb_ref, pm_ref, pmt_ref, a_ref, bcat_ref, ccat_ref, dsk_ref, wglu_ref, bglu_ref,
         o_ref, buf_a, buf_b, u_a, u_b, st) = refs
    rws = chunk * batch
    n_tiles = 2 * S5_LANES // MXU_TILE
    per_tile = chunk // n_tiles

    @pl.when(pl.program_id(0) == 0)
    def _():
        buf_a[...] = jnp.zeros_like(buf_a)
        buf_b[...] = jnp.zeros_like(buf_b)
        u_a[...] = jnp.zeros_like(u_a)
        u_b[...] = jnp.zeros_like(u_b)
        st[...] = jnp.zeros_like(st)

    a_r, a_i = a_ref[0], a_ref[1]

    def stage(pos_in, pos_out, cur, oth, u_cur):
        u_old = u_cur[...]
        ub = _dot(pm_ref[...], u_ref[:, pos_in].reshape(rws, S5_CHANNELS)).astype(BF16)
        u_cur[...] = ub
        sr, si = st[:, :S5_LANES], st[:, S5_LANES:]
        racc = None
        for j in range(n_tiles):
            for k in range(per_tile):
                idx = j * per_tile + k
                off = ((chunk - 1 - idx) if reverse else idx) * batch
                nr = a_r * sr - a_i * si + oth[off:off + batch, :S5_LANES]
                ni = a_r * si + a_i * sr + oth[off:off + batch, S5_LANES:]
                oth[off:off + batch, :S5_LANES] = nr
                oth[off:off + batch, S5_LANES:] = ni
                sr, si = nr, ni
            cols = slice(j * MXU_TILE, (j + 1) * MXU_TILE)
            part = _dot(cur[:, cols].astype(BF16), ccat_ref[cols, :])
            racc = part if racc is None else racc + part
            cur[:, cols] = _dot(ub, bcat_ref[:, cols])
        st[:, :S5_LANES] = sr
        st[:, S5_LANES:] = si
        if reverse:
            o_ref[pos_out] = racc
        else:
            y = jax.nn.gelu(racc + rb_ref[pos_out] + dsk_ref[...] * u_old.astype(F32))
            z = _dot(y.astype(BF16), wglu_ref[...]) + bglu_ref[...]
            o_tb = (z[:, :S5_CHANNELS] * jax.nn.sigmoid(z[:, S5_CHANNELS:])).astype(BF16)
            o_ref[:, pos_out] = _dot(pmt_ref[...], o_tb).astype(BF16).reshape(batch, chunk, S5_CHANNELS)

    first, second = (1, 0) if reverse else (0, 1)
    stage(first, first, buf_a, buf_b, u_a)
    stage(second, second, buf_b, buf_a, u_b)


def _s5_call(u4, rb, pm, pmt, a_ri, b_cat, c_cat, d_skip, w_glu, b_glu, *, layer, reverse, lat_rows):
    batch, n_all, chunk, _ = u4.shape
    rws = chunk * batch
    n_l = lat_rows // chunk
    n_c = n_all - n_l
    assert n_l % 2 == 0 and n_c % 2 == 0
    np_all, np_l, np_c = n_all // 2, n_l // 2, n_c // 2
    dirn = 1 if reverse else 0

    def pair(g):
        g = jnp.clip(g, 0, np_all - 1)
        if reverse:
            return np_all - 1 - g
        return jnp.where(g < np_c, np_l + g, g - np_c)

    def par4(g):
        return (layer, dirn, 0, 0)

    def lyr3(g):
        return (layer, 0, 0)

    in_specs = [pl.BlockSpec((batch, 2, chunk, S5_CHANNELS), lambda g: (0, pair(g), 0, 0))]
    args = [u4]
    if not reverse:
        in_specs.append(pl.BlockSpec((2, rws, S5_CHANNELS), lambda g: (pair(g - 1), 0, 0)))
        args.append(rb)
    in_specs.append(_const_spec((rws, rws), lambda g: (0, 0)))
    args.append(pm)
    if not reverse:
        in_specs.append(_const_spec((rws, rws), lambda g: (0, 0)))
        args.append(pmt)
    in_specs += [
        _const_spec((None, None, 2, batch, S5_LANES), lambda g: (layer, dirn, 0, 0, 0)),
        _const_spec((None, None, S5_CHANNELS, 2 * S5_LANES), par4),
        _const_spec((None, None, 2 * S5_LANES, S5_CHANNELS), par4),
    ]
    args += [a_ri, b_cat, c_cat]
    if reverse:
        out_spec = pl.BlockSpec((2, rws, S5_CHANNELS), lambda g: (pair(g - 1), 0, 0))
        out_shape = jax.ShapeDtypeStruct((np_all * 2, rws, S5_CHANNELS), F32)
    else:
        in_specs += [
            _const_spec((None, 1, S5_CHANNELS), lyr3),
            _const_spec((None, S5_CHANNELS, 2 * S5_CHANNELS), lyr3),
            _const_spec((None, 1, 2 * S5_CHANNELS), lyr3),
        ]
        args += [d_skip, w_glu, b_glu]
        out_spec = pl.BlockSpec((batch, 2, chunk, S5_CHANNELS), lambda g: (0, pair(g - 1), 0, 0))
        out_shape = jax.ShapeDtypeStruct(u4.shape, BF16)
    return pl.pallas_call(
        functools.partial(_s5_kernel, reverse=reverse, chunk=chunk, batch=batch),
        grid=(np_all + 1,),
        in_specs=in_specs,
        out_specs=out_spec,
        out_shape=out_shape,
        scratch_shapes=[
            pltpu.VMEM((rws, 2 * S5_LANES), F32),
            pltpu.VMEM((rws, 2 * S5_LANES), F32),
            pltpu.VMEM((rws, S5_CHANNELS), BF16),
            pltpu.VMEM((rws, S5_CHANNELS), BF16),
            pltpu.VMEM((batch, 2 * S5_LANES), F32),
        ],
        compiler_params=_cparams(("arbitrary",)),
        name="s5_bwd" if reverse else "s5_fwd",
    )(*args)


def _merge_kernel(x_ref, mod_ref, modt_ref, gpre_ref, gpost_ref, a_ref, b_ref, c_ref, d_ref, wg_ref, wb_ref, wo_ref,
                  o_ref, *, tail):
    tm, d = x_ref.shape
    ms = (mod_ref[...], modt_ref[...])
    pieces = _mod_pieces(tm, tail)
    h = jnp.concatenate(
        [(_rms(x_ref[r0:r1], gpre_ref[1:2]) * (1.0 + ms[k][4:5]) + ms[k][3:4]).astype(BF16) for r0, r1, k in pieces],
        axis=0)
    merged = None
    for i, br in enumerate((a_ref, b_ref, c_ref, d_ref)):
        gate = jax.nn.sigmoid(_dot(h, wg_ref[:, i * d:(i + 1) * d]))
        term = gate * _dot(br[...], wb_ref[i])
        merged = term if merged is None else merged + term
    y = _dot(merged.astype(BF16), wo_ref[...])
    for r0, r1, k in pieces:
        o_ref[r0:r1] = x_ref[r0:r1] + ms[k][5:6] * _rms(y[r0:r1], gpost_ref[1:2])


def _merge_call(x3, mods, norm_pre, norm_post, a3, b3, c3, d3, wg_bf, wb_bf, wo_bf, *, layer, tm, n_blk, ctx_rows):
    b, _, d = x3.shape
    bj = _lag_block(b, n_blk, 0)

    def row(q):
        return (*bj(q), 0)

    def lyr3(q):
        return (layer, 0, 0)

    return pl.pallas_call(
        functools.partial(_merge_kernel, tail=ctx_rows),
        grid=(b * n_blk,),
        in_specs=_lag_specs(b, tm, d, n_blk, layer, ctx_rows > 0, 0) + [
            _const_spec((None, 3, d), lyr3),
            _const_spec((None, 3, d), lyr3),
            pl.BlockSpec((None, tm, BRANCH_WIDTH), row),
            pl.BlockSpec((None, tm, BRANCH_WIDTH), row),
            pl.BlockSpec((None, tm, BRANCH_WIDTH), row),
            pl.BlockSpec((None, tm, BRANCH_WIDTH), row),
            _const_spec((None, d, N_BRANCH * d), lyr3),
            _const_spec((None, N_BRANCH, BRANCH_WIDTH, d), lambda q: (layer, 0, 0, 0)),
            _const_spec((None, d, d), lyr3),
        ],
        out_specs=pl.BlockSpec((None, tm, d), row),
        out_shape=jax.ShapeDtypeStruct((b, n_blk * tm, d), F32),
        compiler_params=_cparams(("parallel",)),
        name="merge",
    )(x3, mods, mods, norm_pre, norm_post, a3, b3, c3, d3, wg_bf, wb_bf, wo_bf)


def _rot_cols(w):
    q = w.shape[-1] // 4
    return jnp.concatenate([-w[..., q:2 * q], w[..., 0:q], -w[..., 3 * q:4 * q], w[..., 2 * q:3 * q]], axis=-1)


def _rope_full(rows_n, rot_dim):
    f32 = np.float32
    axis_dim = rot_dim // 2
    inv_freq = (f32(ROPE_THETA) ** (-np.arange(0, axis_dim, 2, dtype=f32) / f32(axis_dim))).astype(f32)
    row = np.repeat(np.arange(rows_n, dtype=f32), GRID_W)
    col = np.tile(np.arange(GRID_W, dtype=f32), rows_n)
    ang_r = row[:, None] * inv_freq[None, :]
    ang_c = col[:, None] * inv_freq[None, :]
    cos = np.concatenate([np.cos(ang_r), np.cos(ang_r), np.cos(ang_c), np.cos(ang_c)], axis=-1)
    sin = np.concatenate([np.sin(ang_r), np.sin(ang_r), np.sin(ang_c), np.sin(ang_c)], axis=-1)
    return cos.astype(f32), sin.astype(f32)


def _rope_table(l, tm):
    f32 = np.float32
    cg, sg = _rope_full(l // GRID_W, HEAD_DIM)
    cm, sm = _rope_full(l // GRID_W, MLA_ROPE)
    scale = f32((MLA_NOPE + MLA_ROPE) ** -0.5)
    pad = MLA_SLOT - MLA_NOPE - MLA_ROPE

    def build(cg, sg, cm, sm):
        n = cg.shape[0]
        ones = np.ones((n, MLA_NOPE), f32)
        zeros = np.zeros((n, MLA_NOPE), f32)
        return np.concatenate([
            cg, cg, sg, sg,
            scale * ones, scale * cm, zeros[:, :pad],
            zeros, scale * sm, zeros[:, :pad],
            cm, sm, zeros,
        ], axis=-1)

    lat = build(cg, sg, cm, sm)
    one_g, zero_g = np.ones((tm, HEAD_DIM), f32), np.zeros((tm, HEAD_DIM), f32)
    ctx = build(one_g, zero_g, one_g[:, :MLA_ROPE], zero_g[:, :MLA_ROPE])
    return jnp.asarray(np.concatenate([lat, ctx], axis=0))


GQA_HEAD_ORDER = (0, 2, 1, 3)


def _proj_cols(w):
    o = 0

    def take(n):
        nonlocal o
        v = w[..., o:o + n]
        o += n
        return v

    kvl, kpe = take(MLA_KV_LORA), take(MLA_ROPE)
    gk, gv = take(GQA_KV_HEADS * HEAD_DIM), take(GQA_KV_HEADS * HEAD_DIM)
    u, ql, gq, z = take(S5_CHANNELS), take(MLA_Q_LORA), take(GQA_Q_HEADS * HEAD_DIM), take(2 * GMLP_WIDTH)
    gate = w[..., o:]
    gqh = [gq[..., i * HEAD_DIM:(i + 1) * HEAD_DIM] for i in range(GQA_Q_HEADS)]
    gkh = [gk[..., i * HEAD_DIM:(i + 1) * HEAD_DIM] for i in range(GQA_KV_HEADS)]
    pad = jnp.zeros(w.shape[:-1] + (PC_U - PC_A - MLA_KV_LORA - 2 * MLA_ROPE,), w.dtype)
    wp = jnp.concatenate([z, kvl, kpe, _rot_cols(kpe), pad, u, ql]
                         + [gqh[i] for i in GQA_HEAD_ORDER] + [_rot_cols(gqh[i]) for i in GQA_HEAD_ORDER]
                         + [gk] + [_rot_cols(h) for h in gkh] + [gv], axis=-1)
    return wp, gate


def _wprep_kernel(wt_ref, wp_ref, wg_ref):
    n = wt_ref.shape[0]
    full = n // LANES * LANES
    pieces = [wt_ref[r0:r0 + LANES, :].T for r0 in range(0, full, LANES)]
    if full < n:
        rest = jnp.concatenate([wt_ref[full:, :], jnp.zeros((LANES - (n - full), wt_ref.shape[1]), F32)], axis=0)
        pieces.append(rest.T[:, :n - full])
    wp, wg = _proj_cols(jnp.concatenate(pieces, axis=1))
    wp_ref[...] = wp.astype(BF16)
    wg_ref[...] = wg.astype(BF16)


def _wprep_call(w_in):
    depth, d, n = w_in.shape
    tr = 256
    n_gate = N_BRANCH * d
    return pl.pallas_call(
        _wprep_kernel,
        grid=(depth, d // tr),
        in_specs=[pl.BlockSpec((None, n, tr), lambda l, i: (l, 0, i))],
        out_specs=[pl.BlockSpec((None, tr, PC_END), lambda l, i: (l, i, 0)),
                   pl.BlockSpec((None, tr, n_gate), lambda l, i: (l, i, 0))],
        out_shape=[jax.ShapeDtypeStruct((depth, d, PC_END), BF16), jax.ShapeDtypeStruct((depth, d, n_gate), BF16)],
        compiler_params=_cparams(("parallel", "parallel")),
        name="w_prep",
    )(jnp.swapaxes(w_in, 1, 2))


def _mla_kv_weight(w_ukv):
    depth = w_ukv.shape[0]
    wh = w_ukv.reshape(depth, MLA_KV_LORA, MLA_HEADS, MLA_NOPE + MLA_V)
    k_nope, v = wh[..., :MLA_NOPE], wh[..., MLA_NOPE:]
    kslot = jnp.concatenate([k_nope, jnp.zeros((depth, MLA_KV_LORA, MLA_HEADS, MLA_SLOT - MLA_NOPE), F32)], axis=-1)
    top = jnp.concatenate([kslot.reshape(depth, MLA_KV_LORA, -1), v.reshape(depth, MLA_KV_LORA, -1)], axis=-1)
    eye = jnp.eye(MLA_ROPE, dtype=F32)
    pe_slot = jnp.concatenate([jnp.zeros((MLA_ROPE, MLA_NOPE), F32), eye,
                               jnp.zeros((MLA_ROPE, MLA_SLOT - MLA_NOPE - MLA_ROPE), F32)], axis=-1)
    pe_rows = jnp.concatenate([jnp.tile(pe_slot, (1, MLA_HEADS)), jnp.zeros((MLA_ROPE, MLA_HEADS * MLA_V), F32)],
                              axis=-1)
    pe_rows = jnp.broadcast_to(pe_rows, (depth,) + pe_rows.shape)
    tail = jnp.zeros((depth, 256 - MLA_KV_LORA - 2 * MLA_ROPE, MLA_KV_COLS), F32)
    return jnp.concatenate([top, pe_rows, pe_rows, tail], axis=1).astype(BF16)


def _mla_q_weight(w_uq):
    depth = w_uq.shape[0]
    wh = w_uq.reshape(depth, MLA_Q_LORA, MLA_HEADS, MLA_NOPE + MLA_ROPE)
    nope, pe = wh[..., :MLA_NOPE], wh[..., MLA_NOPE:]
    pad = jnp.zeros((depth, MLA_Q_LORA, MLA_HEADS, MLA_SLOT - MLA_NOPE - MLA_ROPE), F32)
    full = jnp.concatenate([nope, pe, pad], axis=-1).reshape(depth, MLA_Q_LORA, -1)
    rot = jnp.concatenate([jnp.zeros_like(nope), _rot_cols(pe), pad], axis=-1).reshape(depth, MLA_Q_LORA, -1)
    return jnp.concatenate([full, rot], axis=-1).astype(BF16)


def _block_diag(w):
    g, a, b = w.shape[-3:]
    lead = w.shape[:-3]
    cols = jnp.swapaxes(w, -3, -2).reshape(lead + (a, g * b))
    tiled = jnp.tile(cols, (1,) * len(lead) + (g, 1))
    same = (jnp.arange(g * a)[:, None] // a) == (jnp.arange(g * b)[None, :] // b)
    return jnp.where(same, tiled, jnp.zeros((), w.dtype))


def kernel(x, c, ctx, c_ctx, w_ada, b_ada, norm_pre, norm_post, w_ffn_in, w_ffn_out, w_in, mla_q_norm, mla_w_uq,
           mla_kv_norm, mla_w_ukv, gqa_sink, s5_lam_re, s5_lam_im, s5_log_dt, s5_b_re, s5_b_im, s5_c_re, s5_c_im,
           s5_d, s5_w_glu, s5_b_glu, gmlp_norm, gmlp_w_s, gmlp_b_s, w_branch, w_out):
    b, l, d = x.shape
    cl = ctx.shape[1]
    r = l + cl
    depth = w_ada.shape[0]
    assert b == SUBLANES and b < MOD_ROWS
    assert l % GRID_W == 0 and l >= 3 * W_BLOCK and l % cl == 0
    assert l % GMLP_CHUNK == 0 and WIDE_TILE % GMLP_CHUNK == 0 and l % MLA_Q_TILE == 0 and l % GQA_Q_TILE == 0
    assert l % S5_CHUNK == 0 and cl % S5_CHUNK == 0 and l % LAT_TILE == 0
    assert r % WIDE_TILE == 0 and cl <= WIDE_TILE

    wffn_in_bf = w_ffn_in.astype(BF16)
    wffn_out_bf = w_ffn_out.astype(BF16)
    wp_bf, wg_bf = _wprep_call(w_in)
    wkv_bf = _mla_kv_weight(mla_w_ukv)
    wq_bf = _mla_q_weight(mla_w_uq)
    head_order = jnp.array(GQA_HEAD_ORDER)
    wb1 = w_branch[:, 1].reshape(depth, GQA_Q_HEADS, HEAD_DIM, d)[:, head_order].reshape(depth, BRANCH_WIDTH, d)
    wb_bf = jnp.concatenate([w_branch[:, :1], wb1[:, None], w_branch[:, 2:]], axis=1).astype(BF16)
    wo_bf = w_out.astype(BF16)
    sink_rows = jnp.broadcast_to(
        jnp.concatenate([gqa_sink[:, head_order], jnp.zeros((depth, SUBLANES - GQA_Q_HEADS), F32)], axis=1)[:, :, None],
        (depth, SUBLANES, LANES))
    ws_bf = jnp.transpose(gmlp_w_s, (0, 2, 1, 3)).reshape(depth, GMLP_CHUNK, GMLP_GROUPS * GMLP_CHUNK).astype(BF16)
    bs_f = jnp.repeat(jnp.transpose(gmlp_b_s, (0, 2, 1)), GMLP_WIDTH // GMLP_GROUPS, axis=2)
    tab = _rope_table(l, cl)
    mla_kv_norm3 = mla_kv_norm[:, None, :]
    mla_q_norm3 = mla_q_norm[:, None, :]
    gmlp_norm3 = gmlp_norm[:, None, :]
    s5_d3 = s5_d[:, None, :]
    s5_b_glu3 = s5_b_glu[:, None, :]
    wglu_bf = s5_w_glu.astype(BF16)
    tb = jnp.arange(S5_CHUNK * b)
    pm = (tb[None, :] == ((tb % b) * S5_CHUNK + tb // b)[:, None]).astype(BF16)
    pmt = pm.T

    hg, p, g = S5_GROUP, S5_STATE, S5_GROUPS
    n_par = depth * 2 * g
    rep = lambda t: jnp.repeat(t.reshape(n_par, p), hg, axis=1)
    ldt = jnp.broadcast_to(s5_log_dt[..., None], (depth, 2, g, p))
    a_re_x, a_im_x, bb_re, bb_im = _s5_disc_call(
        rep(s5_lam_re), rep(s5_lam_im), rep(ldt), s5_b_re.reshape(n_par, p * hg), s5_b_im.reshape(n_par, p * hg))
    a_ri = jnp.broadcast_to(
        jnp.stack([a_re_x[:, ::hg].reshape(depth, 2, g * p), a_im_x[:, ::hg].reshape(depth, 2, g * p)], axis=2)
        [:, :, :, None, :], (depth, 2, 2, b, g * p))
    bb_re = jnp.swapaxes(bb_re.reshape(depth, 2, g, p, hg), -1, -2)
    bb_im = jnp.swapaxes(bb_im.reshape(depth, 2, g, p, hg), -1, -2)
    s5_b_cat = jnp.concatenate([_block_diag(bb_re), _block_diag(bb_im)], axis=-1).astype(BF16)
    s5_c_cat = jnp.concatenate([_block_diag(jnp.swapaxes(s5_c_re, -1, -2)),
                                _block_diag(jnp.swapaxes(-s5_c_im, -1, -2))], axis=-2).astype(BF16)

    cs = jnp.concatenate([c, c_ctx[None, :], jnp.zeros((MOD_ROWS - b - 1, d), F32)], axis=0)
    mods = _ada_call(cs, w_ada, b_ada[:, None, :]).reshape(depth, MOD_ROWS, N_MOD, d)

    xs = jnp.concatenate([x, ctx], axis=1)
    n_wide = r // WIDE_TILE
    for layer in range(depth):
        last = layer == depth - 1
        xs = _ffn_call(xs, mods, norm_pre, norm_post, wffn_in_bf, wffn_out_bf,
                       layer=layer, s=0, which=0, tm=WIDE_TILE, n_blk=n_wide, ctx_rows=cl)
        mq, mkv, gq, gkv, u3, d3 = _proj_call(
            xs, mods, norm_pre, wp_bf, tab, mla_kv_norm3, wkv_bf, mla_q_norm3, wq_bf, gmlp_norm3, ws_bf, bs_f,
            layer=layer, lat_rows=l)
        a_dst = None if last else _mla_ctx_call(mq, mkv, lat_rows=l)
        b_dst = None if last else _gqa_ctx_call(gq, gkv, sink_rows, layer=layer, lat_rows=l)
        a3 = _mla_lat_call(mq, mkv, a_dst, lat_rows=l)
        b3 = _gqa_lat_call(gq, gkv, sink_rows, b_dst, layer=layer, lat_rows=l)
        u4 = u3.reshape(b, r // S5_CHUNK, S5_CHUNK, S5_CHANNELS)
        rb = _s5_call(u4, None, pm, None, a_ri, s5_b_cat, s5_c_cat, None, None, None,
                      layer=layer, reverse=True, lat_rows=l)
        c3 = _s5_call(u4, rb, pm, pmt, a_ri, s5_b_cat, s5_c_cat, s5_d3, wglu_bf, s5_b_glu3,
                      layer=layer, reverse=False, lat_rows=l).reshape(b, r, S5_CHANNELS)
        if last:
            tm, n_blk, ctx_rows = LAT_TILE, l // LAT_TILE, 0
        else:
            tm, n_blk, ctx_rows = WIDE_TILE, n_wide, cl
        xs = _merge_call(xs, mods, norm_pre, norm_post, a3, b3, c3, d3, wg_bf, wb_bf, wo_bf,
                         layer=layer, tm=tm, n_blk=n_blk, ctx_rows=ctx_rows)
        xs = _ffn_call(xs, mods, norm_pre, norm_post, wffn_in_bf, wffn_out_bf,
                       layer=layer, s=2, which=1, tm=tm, n_blk=n_blk, ctx_rows=ctx_rows)
    return xs
```

```python
import functools

import jax
import jax.numpy as jnp
import numpy as np
from jax import lax
from jax.experimental import pallas as pl
from jax.experimental.pallas import tpu as pltpu

F32 = jnp.float32
BF16 = jnp.bfloat16

GRID_W = 64
HEAD_DIM = 64
ROPE_THETA = 10000.0
RMS_EPS = 1e-6
LN_EPS = 1e-5
NEG_INF = -1e30
MLA_HEADS = 4
MLA_Q_LORA = 256
MLA_KV_LORA = 128
MLA_NOPE = 64
MLA_ROPE = 32
MLA_V = 64
GQA_Q_HEADS = 4
GQA_KV_HEADS = 2
WINDOW = 128
W_BLOCK = 128
S5_CHANNELS = 256
S5_GROUP = 16
S5_GROUPS = S5_CHANNELS // S5_GROUP
S5_STATE = 64
GMLP_WIDTH = 256
GMLP_CHUNK = 128
GMLP_GROUPS = 4
N_BRANCH = 4
BRANCH_WIDTH = 256
N_MOD = 9

LANES = 128
SUBLANES = 8
MXU_TILE = 256
VMEM_LIMIT_BYTES = 56 * 1024 * 1024

WIDE_TILE = 768
LAT_TILE = 512
MLA_Q_TILE = 512
GQA_Q_TILE = 512
S5_CHUNK = 64
MOD_ROWS = 16
MLA_SLOT = LANES
S5_LANES = S5_GROUPS * S5_STATE

PC_Z = 0
PC_A = PC_Z + 2 * GMLP_WIDTH
PC_U = PC_A + 256
PC_QL = PC_U + S5_CHANNELS
PC_GQ = PC_QL + MLA_Q_LORA
PC_GQR = PC_GQ + GQA_Q_HEADS * HEAD_DIM
PC_GK = PC_GQR + GQA_Q_HEADS * HEAD_DIM
PC_GKR = PC_GK + GQA_KV_HEADS * HEAD_DIM
PC_GV = PC_GKR + GQA_KV_HEADS * HEAD_DIM
PC_END = PC_GV + GQA_KV_HEADS * HEAD_DIM

TC_GC, TC_GS, TC_MQC, TC_MQS, TC_MKT, TC_END = 0, 128, 256, 384, 512, 640

MLA_K_COLS = MLA_HEADS * MLA_SLOT
MLA_KV_COLS = MLA_K_COLS + MLA_HEADS * MLA_V


def _cparams(sem):
    return pltpu.CompilerParams(dimension_semantics=sem, vmem_limit_bytes=VMEM_LIMIT_BYTES)


def _const_spec(shape, index_map):
    return pl.BlockSpec(shape, index_map, pipeline_mode=pl.Buffered(1))


def _rms(x, gain):
    return x * lax.rsqrt(jnp.mean(x * x, axis=-1, keepdims=True) + RMS_EPS) * gain


def _dot(a, b):
    return jnp.dot(a, b, preferred_element_type=F32)


def _dot_nt(a, b):
    return lax.dot_general(a, b, (((1,), (1,)), ((), ())), preferred_element_type=F32)


def _ada_kernel(cs_ref, w_ref, b_ref, o_ref):
    cs = cs_ref[...]
    s = (cs * jax.nn.sigmoid(cs)).astype(BF16)
    o_ref[...] = _dot(s, w_ref[...].astype(BF16)) + b_ref[...]


def _ada_call(cs, w_ada, b_ada3):
    depth, d, n = w_ada.shape
    tn = d
    return pl.pallas_call(
        _ada_kernel,
        grid=(depth, n // tn),
        in_specs=[
            pl.BlockSpec((MOD_ROWS, d), lambda l, j: (0, 0)),
            pl.BlockSpec((None, d, tn), lambda l, j: (l, 0, j)),
            pl.BlockSpec((None, 1, tn), lambda l, j: (l, 0, j)),
        ],
        out_specs=pl.BlockSpec((None, MOD_ROWS, tn), lambda l, j: (l, 0, j)),
        out_shape=jax.ShapeDtypeStruct((depth, MOD_ROWS, n), F32),
        compiler_params=_cparams(("parallel", "parallel")),
        name="ada_mod",
    )(cs, w_ada, b_ada3)


def _s5_disc_kernel(lre_ref, lim_ref, ldt_ref, bre_ref, bim_ref, are_ref, aim_ref, bbre_ref, bbim_ref):
    lam_re = jnp.minimum(lre_ref[...], -1e-4)
    lam_im = lim_ref[...]
    dt = jnp.exp(ldt_ref[...])
    mag = jnp.exp(lam_re * dt)
    a_re = mag * jnp.cos(lam_im * dt)
    a_im = mag * jnp.sin(lam_im * dt)
    nr, ni = a_re - 1.0, a_im
    den = lam_re * lam_re + lam_im * lam_im
    coef_re = (nr * lam_re + ni * lam_im) / den
    coef_im = (ni * lam_re - nr * lam_im) / den
    b_re, b_im = bre_ref[...], bim_ref[...]
    are_ref[...] = a_re
    aim_ref[...] = a_im
    bbre_ref[...] = coef_re * b_re - coef_im * b_im
    bbim_ref[...] = coef_re * b_im + coef_im * b_re


def _s5_disc_call(lre, lim, ldt, bre, bim):
    shp = jax.ShapeDtypeStruct(lre.shape, F32)
    return pl.pallas_call(_s5_disc_kernel, out_shape=(shp, shp, shp, shp), name="s5_disc")(lre, lim, ldt, bre, bim)


def _mod_pieces(tm, tail):
    return ((0, tm - tail, 0), (tm - tail, tm, 1)) if tail else ((0, tm, 0),)


def _lag_block(b, n_blk, lag):
    def bj(q):
        q = jnp.clip(q - lag, 0, b * n_blk - 1)
        return q // n_blk, q % n_blk
    return bj


def _lag_specs(b, tm, d, n_blk, layer, ctx_tail, lag):
    bj = _lag_block(b, n_blk, lag)

    def tail_row(q):
        bi, j = bj(q)
        return jnp.where(j == n_blk - 1, b, bi) if ctx_tail else bi

    return [
        pl.BlockSpec((None, tm, d), lambda q: (*bj(q), 0)),
        pl.BlockSpec((None, None, N_MOD, d), lambda q: (layer, bj(q)[0], 0, 0)),
        pl.BlockSpec((None, None, N_MOD, d), lambda q: (layer, tail_row(q), 0, 0)),
    ]


def _ffn_kernel(xn_ref, modn_ref, modnt_ref, xo_ref, modo_ref, modot_ref, gpre_ref, gpost_ref, wi_ref, wo_ref, *rest,
                s, d_ff, chunks, tail, n_steps, cast_next):
    if cast_next:
        wi_next_ref, wo_next_ref, o_ref, wi_cast_ref, wo_cast_ref, h_ref, y_ref = rest
    else:
        o_ref, h_ref, y_ref = rest
    q = pl.program_id(0)
    pieces = _mod_pieces(xn_ref.shape[0], tail)

    def cast_slabs():
        if cast_next:
            wi_cast_ref[...] = wi_next_ref[...].astype(BF16)
            wo_cast_ref[...] = wo_next_ref[...].astype(BF16)

    def pre_norm():
        ms = (modn_ref[...], modnt_ref[...])
        return jnp.concatenate(
            [(_rms(xn_ref[r0:r1], gpre_ref[s:s + 1]) * (1.0 + ms[k][3 * s + 1:3 * s + 2]) + ms[k][3 * s:3 * s + 1])
             .astype(BF16) for r0, r1, k in pieces], axis=0)

    def post_norm():
        ms = (modo_ref[...], modot_ref[...])
        return jnp.concatenate(
            [xo_ref[r0:r1] + 0.5 * ms[k][3 * s + 2:3 * s + 3] * _rms(y_ref[r0:r1], gpost_ref[s:s + 1])
             for r0, r1, k in pieces], axis=0)

    def up(h, c0, c1):
        a = _dot(h, wi_ref[:, c0:c1])
        g = _dot(h, wi_ref[:, d_ff + c0:d_ff + c1])
        return (a * jax.nn.sigmoid(a) * g).astype(BF16)

    @pl.when(q == 0)
    def _():
        h_ref[...] = pre_norm()
        y_ref[...] = jnp.zeros_like(y_ref)
        cast_slabs()

    @pl.when(jnp.logical_and(q > 0, q < n_steps - 1))
    def _():
        always = q < n_steps
        h = h_ref[...]
        acts = [up(h, *chunks[0])]
        cast_slabs()
        out = post_norm()
        o_ref[...] = out
        h = jnp.where(always, h, out.astype(BF16))
        acts += [up(h, c0, c1) for c0, c1 in chunks[1:]]
        h_new = pre_norm()
        h_ref[...] = h_new
        w0 = chunks[0][1] - chunks[0][0]
        acts[0] = jnp.concatenate([jnp.where(always, acts[0][:, :h_new.shape[1]], h_new), acts[0][:, h_new.shape[1]:]],
                                  axis=1) if w0 > h_new.shape[1] else jnp.where(always, acts[0], h_new[:, :w0])
        y = None
        for act, (c0, c1) in zip(acts, chunks):
            part = _dot(act, wo_ref[c0:c1, :])
            y = part if y is None else y + part
        y_ref[...] = y

    @pl.when(q == n_steps - 1)
    def _():
        o_ref[...] = post_norm()
        cast_slabs()


FFN_CAST_SLABS = 16


def _ffn_call(x3, mods, norm_pre, norm_post, w_in_bf, w_out_bf, w_next, *, layer, s, tm, n_blk, ctx_rows):
    b, _, d = x3.shape
    d_ff = w_out_bf.shape[0]
    half = (d_ff // MXU_TILE + 1) // 2 * MXU_TILE
    chunks = ((0, half), (half, d_ff)) if 0 < half < d_ff else ((0, d_ff),)
    n = b * n_blk
    ns = FFN_CAST_SLABS
    while ns > n + 2:
        ns //= 2
    assert d % (ns * 2 * SUBLANES) == 0 and d_ff % (ns * 2 * SUBLANES) == 0
    kern = functools.partial(_ffn_kernel, s=s, d_ff=d_ff, chunks=chunks, tail=ctx_rows, n_steps=n + 2,
                             cast_next=w_next is not None)

    def specs(lag):
        return _lag_specs(b, tm, d, n_blk, layer, ctx_rows > 0, lag)

    in_specs = specs(0) + specs(2) + [
        _const_spec((None, 3, d), lambda q: (layer, 0, 0)),
        _const_spec((None, 3, d), lambda q: (layer, 0, 0)),
        _const_spec((d, 2 * d_ff), lambda q: (0, 0)),
        _const_spec((d_ff, d), lambda q: (0, 0)),
    ]
    args = [x3, mods, mods, x3, mods, mods, norm_pre, norm_post, w_in_bf, w_out_bf]
    out_specs = [specs(2)[0]]
    out_shape = [jax.ShapeDtypeStruct((b, n_blk * tm, d), F32)]
    if w_next is not None:
        wi_f32, wo_f32, l_next, which_next = w_next

        def slab(q):
            return jnp.minimum(q, ns - 1)

        in_specs += [
            pl.BlockSpec((None, None, d // ns, 2 * d_ff), lambda q: (l_next, which_next, slab(q), 0)),
            pl.BlockSpec((None, None, d_ff // ns, d), lambda q: (l_next, which_next, slab(q), 0)),
        ]
        args += [wi_f32, wo_f32]
        out_specs += [pl.BlockSpec((d // ns, 2 * d_ff), lambda q: (slab(q), 0)),
                      pl.BlockSpec((d_ff // ns, d), lambda q: (slab(q), 0))]
        out_shape += [jax.ShapeDtypeStruct((d, 2 * d_ff), BF16), jax.ShapeDtypeStruct((d_ff, d), BF16)]
    return pl.pallas_call(
        kern,
        grid=(n + 2,),
        in_specs=in_specs,
        out_specs=out_specs,
        out_shape=out_shape,
        scratch_shapes=[pltpu.VMEM((tm, d), BF16), pltpu.VMEM((tm, d), F32)],
        compiler_params=_cparams(("arbitrary",)),
        name="ffn",
    )(*args)


def _proj_kernel(x_ref, mod_ref, modt_ref, gpre_ref, wp_ref, tab_ref, gkvn_ref, wkv_ref, gqn_ref, wq_ref,
                 gmn_ref, ws_ref, bs_ref,
                 mq_ref, mkv_ref, gq_ref, gkv_ref, u_ref, d_ref, *, tail):
    ms = (mod_ref[...], modt_ref[...])
    h = jnp.concatenate(
        [(_rms(x_ref[r0:r1], gpre_ref[1:2]) * (1.0 + ms[k][4:5]) + ms[k][3:4]).astype(BF16)
         for r0, r1, k in _mod_pieces(x_ref.shape[0], tail)], axis=0)
    p = _dot(h, wp_ref[...])
    tab = tab_ref[...]
    gc, gs = tab[:, TC_GC:TC_GS], tab[:, TC_GS:TC_MQC]
    mqc, mqs, mkt = tab[:, TC_MQC:TC_MQS], tab[:, TC_MQS:TC_MKT], tab[:, TC_MKT:TC_END]

    kvl = p[:, PC_A:PC_A + MLA_KV_LORA]
    kvn = _rms(kvl, gkvn_ref[...])
    pe = p[:, PC_A + MLA_KV_LORA:PC_U] * mkt
    a2 = jnp.concatenate([kvn, pe], axis=-1).astype(BF16)
    mkv_ref[...] = _dot(a2, wkv_ref[...]).astype(BF16)

    qn = _rms(p[:, PC_QL:PC_GQ], gqn_ref[...]).astype(BF16)
    q = _dot(qn, wq_ref[...])
    qs = [q[:, hh * MLA_SLOT:(hh + 1) * MLA_SLOT] * mqc
          + q[:, MLA_K_COLS + hh * MLA_SLOT:MLA_K_COLS + (hh + 1) * MLA_SLOT] * mqs
          for hh in range(MLA_HEADS)]
    mq_ref[...] = jnp.concatenate(qs, axis=-1).astype(BF16)

    gq = [(p[:, PC_GQ + j * LANES:PC_GQ + (j + 1) * LANES] * gc
           + p[:, PC_GQR + j * LANES:PC_GQR + (j + 1) * LANES] * gs) * (HEAD_DIM ** -0.5)
          for j in range(GQA_Q_HEADS * HEAD_DIM // LANES)]
    gq_ref[...] = jnp.concatenate(gq, axis=-1).astype(BF16)
    gk = p[:, PC_GK:PC_GKR] * gc + p[:, PC_GKR:PC_GV] * gs
    gkv_ref[...] = jnp.concatenate([gk, p[:, PC_GV:PC_END]], axis=-1).astype(BF16)

    u_ref[...] = p[:, PC_U:PC_QL].astype(BF16)

    zz = jax.nn.gelu(p[:, PC_Z:PC_A])
    ug, v = zz[:, :GMLP_WIDTH], zz[:, GMLP_WIDTH:]
    vc = v - jnp.mean(v, axis=-1, keepdims=True)
    vn = vc * lax.rsqrt(jnp.mean(vc * vc, axis=-1, keepdims=True) + LN_EPS) * gmn_ref[...]
    group_of_lane = lax.broadcasted_iota(jnp.int32, (GMLP_CHUNK, GMLP_WIDTH), 1) // (GMLP_WIDTH // GMLP_GROUPS)
    outs = []
    for ci in range(x_ref.shape[0] // GMLP_CHUNK):
        r0 = ci * GMLP_CHUNK
        vck = vn[r0:r0 + GMLP_CHUNK]
        vbd = jnp.concatenate([jnp.where(group_of_lane == g, vck, 0.0) for g in range(GMLP_GROUPS)], axis=0)
        mixed = _dot(ws_ref[...], vbd.astype(BF16)) + bs_ref[...]
        outs.append(ug[r0:r0 + GMLP_CHUNK] * mixed)
    d_ref[...] = jnp.concatenate(outs, axis=0).astype(BF16)


def _proj_call(x3, mods, norm_pre, wp_bf, tab, mla_kv_norm3, wkv_bf, mla_q_norm3, wq_bf, gmlp_norm3, ws_bf, bs_f,
               *, layer, lat_rows):
    b, r, d = x3.shape
    tm = WIDE_TILE
    n_blk = r // tm
    bj = _lag_block(b, n_blk, 0)

    def row(q):
        return (*bj(q), 0)

    def lyr3(q):
        return (layer, 0, 0)

    widths = (MLA_K_COLS, MLA_KV_COLS, GQA_Q_HEADS * HEAD_DIM, 2 * GQA_KV_HEADS * HEAD_DIM, S5_CHANNELS, GMLP_WIDTH)
    return pl.pallas_call(
        functools.partial(_proj_kernel, tail=r - lat_rows),
        grid=(b * n_blk,),
        in_specs=_lag_specs(b, tm, d, n_blk, layer, True, 0) + [
            _const_spec((None, 3, d), lyr3),
            _const_spec((None, d, PC_END), lyr3),
            pl.BlockSpec((tm, TC_END), lambda q: (bj(q)[1], 0)),
            _const_spec((None, 1, MLA_KV_LORA), lyr3),
            _const_spec((None, 256, MLA_KV_COLS), lyr3),
            _const_spec((None, 1, MLA_Q_LORA), lyr3),
            _const_spec((None, MLA_Q_LORA, 2 * MLA_K_COLS), lyr3),
            _const_spec((None, 1, GMLP_WIDTH), lyr3),
            _const_spec((None, GMLP_CHUNK, GMLP_GROUPS * GMLP_CHUNK), lyr3),
            _const_spec((None, GMLP_CHUNK, GMLP_WIDTH), lyr3),
        ],
        out_specs=[pl.BlockSpec((None, tm, w), row) for w in widths],
        out_shape=[jax.ShapeDtypeStruct((b, r, w), BF16) for w in widths],
        compiler_params=_cparams(("parallel",)),
        name="in_proj",
    )(x3, mods, mods, norm_pre, wp_bf, tab, mla_kv_norm3, wkv_bf, mla_q_norm3, wq_bf, gmlp_norm3, ws_bf, bs_f)


def _mla_attend(q_ref, kv_ref):
    tq = q_ref.shape[0]
    head_of_lane = lax.broadcasted_iota(jnp.int32, (tq, MLA_HEADS * MLA_V), 1) // MLA_V
    acc = jnp.zeros((tq, MLA_HEADS * MLA_V), F32)
    v = kv_ref[:, MLA_K_COLS:]
    sls = [slice(hh * MLA_SLOT, (hh + 1) * MLA_SLOT) for hh in range(MLA_HEADS)]
    scs = [_dot_nt(q_ref[:, sl], kv_ref[:, sl]) for sl in sls]
    ps = [jnp.exp(sc - jnp.max(sc, axis=-1, keepdims=True)) for sc in scs]
    dens = [jnp.sum(p, axis=-1, keepdims=True) for p in ps]
    o_all = _dot(jnp.concatenate([p.astype(BF16) for p in ps], axis=0), v)
    for hh in range(MLA_HEADS):
        acc = jnp.where(head_of_lane == hh, o_all[hh * tq:(hh + 1) * tq] / dens[hh], acc)
    return acc.astype(BF16)


def _mla_lat_kernel(q_ref, kv_ref, *rest):
    rest[-1][...] = _mla_attend(q_ref, kv_ref)


def _mla_ctx_kernel(q_ref, kv_ref, o_ref, *, lat_rows):
    o_ref[:lat_rows] = jnp.zeros((lat_rows, o_ref.shape[1]), o_ref.dtype)
    o_ref[lat_rows:] = _mla_attend(q_ref, kv_ref)


def _mla_ctx_call(mq, mkv, *, lat_rows):
    b, r, _ = mq.shape
    c = r - lat_rows
    blk = lat_rows // c
    w = MLA_HEADS * MLA_V
    return pl.pallas_call(
        functools.partial(_mla_ctx_kernel, lat_rows=lat_rows),
        grid=(b,),
        in_specs=[
            pl.BlockSpec((None, c, MLA_K_COLS), lambda bi: (bi, blk, 0)),
            pl.BlockSpec((None, c, MLA_KV_COLS), lambda bi: (bi, blk, 0)),
        ],
        out_specs=pl.BlockSpec((None, r, w), lambda bi: (bi, 0, 0)),
        out_shape=jax.ShapeDtypeStruct((b, r, w), BF16),
        compiler_params=_cparams(("parallel",)),
        name="mla_context",
    )(mq, mkv)


def _mla_lat_call(mq, mkv, dst, *, lat_rows):
    b, r, _ = mq.shape
    tq = MLA_Q_TILE
    w = MLA_HEADS * MLA_V
    in_specs = [
        pl.BlockSpec((None, tq, MLA_K_COLS), lambda bi, j: (bi, j, 0)),
        pl.BlockSpec((None, r, MLA_KV_COLS), lambda bi, j: (bi, 0, 0)),
    ]
    args = [mq, mkv]
    if dst is not None:
        in_specs.append(pl.BlockSpec(memory_space=pl.ANY))
        args.append(dst)
    return pl.pallas_call(
        _mla_lat_kernel,
        grid=(b, lat_rows // tq),
        in_specs=in_specs,
        out_specs=pl.BlockSpec((None, tq, w), lambda bi, j: (bi, j, 0)),
        out_shape=jax.ShapeDtypeStruct((b, lat_rows if dst is None else r, w), BF16),
        input_output_aliases={} if dst is None else {2: 0},
        compiler_params=_cparams(("parallel", "parallel")),
        name="mla_latent",
    )(*args)


def _gqa_blocks(qs, kc, vc, sink_ref, bands):
    tq = qs[0].shape[0]
    kw = GQA_KV_HEADS * HEAD_DIM
    lo = lax.broadcasted_iota(jnp.int32, (tq, kw), 1) < HEAD_DIM
    sk = jnp.concatenate([jnp.broadcast_to(sink_ref[r:r + 1, 0:1], (tq, 1)) for r in range(GQA_Q_HEADS)], axis=0)
    qsts = []
    for q in qs:
        q0, q1 = q[:, :kw], q[:, kw:]
        zero = jnp.zeros_like(q0)
        qsts.append(jnp.concatenate([jnp.where(lo, q0, zero), jnp.where(lo, zero, q0),
                                     jnp.where(lo, q1, zero), jnp.where(lo, zero, q1)], axis=0))
    scs = [_dot_nt(qst, kc) for qst in qsts]
    mxs = [jnp.maximum(jnp.max(sc, axis=-1, keepdims=True), sk) for sc in scs]
    if bands is not None:
        sbs = [jnp.where(jnp.concatenate([valid] * GQA_Q_HEADS, axis=0), _dot_nt(qst, kb), NEG_INF)
               for qst, (kb, _, valid) in zip(qsts, bands)]
        mxs = [jnp.maximum(mx, jnp.max(sb, axis=-1, keepdims=True)) for mx, sb in zip(mxs, sbs)]
    pcs = [jnp.exp(sc - mx) for sc, mx in zip(scs, mxs)]
    dens = [jnp.sum(pc, axis=-1, keepdims=True) + jnp.exp(sk - mx) for pc, mx in zip(pcs, mxs)]
    os_ = [_dot(pc.astype(BF16), vc) for pc in pcs]
    if bands is not None:
        pbs = [jnp.exp(sb - mx) for sb, mx in zip(sbs, mxs)]
        dens = [den + jnp.sum(pb, axis=-1, keepdims=True) for den, pb in zip(dens, pbs)]
        os_ = [o + _dot(pb.astype(BF16), vb) for o, pb, (_, vb, _) in zip(os_, pbs, bands)]
    outs = []
    for o, den in zip(os_, dens):
        o = o / den
        c0 = jnp.where(lo, o[0:tq], o[tq:2 * tq])
        c1 = jnp.where(lo, o[2 * tq:3 * tq], o[3 * tq:4 * tq])
        outs.append(jnp.concatenate([c0, c1], axis=-1))
    return outs


def _gqa_lat_kernel(q_ref, kv_ref, sink_ref, *rest, lat_rows):
    o_ref = rest[-1]
    kw = GQA_KV_HEADS * HEAD_DIM
    kc, vc = kv_ref[lat_rows:, :kw], kv_ref[lat_rows:, kw:]
    nbk = 3 * W_BLOCK
    n_in = q_ref.shape[0] // W_BLOCK
    qs, bands = [], []
    for i in range(n_in):
        n = pl.program_id(1) * n_in + i
        start = pl.multiple_of(jnp.clip((n - 1) * W_BLOCK, 0, lat_rows - nbk), W_BLOCK)
        kb = kv_ref[pl.ds(start, nbk), :kw]
        vb = kv_ref[pl.ds(start, nbk), kw:]
        qpos = n * W_BLOCK + lax.broadcasted_iota(jnp.int32, (W_BLOCK, nbk), 0)
        kpos = start + lax.broadcasted_iota(jnp.int32, (W_BLOCK, nbk), 1)
        bands.append((kb, vb, jnp.abs(qpos - kpos) <= WINDOW))
        qs.append(q_ref[i * W_BLOCK:(i + 1) * W_BLOCK])
    outs = _gqa_blocks(qs, kc, vc, sink_ref, bands)
    o_ref[...] = jnp.concatenate(outs, axis=0).astype(BF16)


def _gqa_ctx_kernel(q_ref, kv_ref, sink_ref, o_ref, *, lat_rows):
    kw = GQA_KV_HEADS * HEAD_DIM
    tq = q_ref.shape[0] // 2
    outs = _gqa_blocks([q_ref[:tq], q_ref[tq:]], kv_ref[:, :kw], kv_ref[:, kw:], sink_ref, None)
    o_ref[:lat_rows] = jnp.zeros((lat_rows, o_ref.shape[1]), o_ref.dtype)
    o_ref[lat_rows:] = jnp.concatenate(outs, axis=0).astype(BF16)


def _gqa_ctx_call(gq, gkv, sink_rows, *, layer, lat_rows):
    b, r, w = gq.shape
    c = r - lat_rows
    blk = lat_rows // c
    return pl.pallas_call(
        functools.partial(_gqa_ctx_kernel, lat_rows=lat_rows),
        grid=(b,),
        in_specs=[
            pl.BlockSpec((None, c, w), lambda bi: (bi, blk, 0)),
            pl.BlockSpec((None, c, w), lambda bi: (bi, blk, 0)),
            pl.BlockSpec((None, SUBLANES, LANES), lambda bi: (layer, 0, 0)),
        ],
        out_specs=pl.BlockSpec((None, r, w), lambda bi: (bi, 0, 0)),
        out_shape=jax.ShapeDtypeStruct((b, r, w), BF16),
        compiler_params=_cparams(("parallel",)),
        name="gqa_context",
    )(gq, gkv, sink_rows)


def _gqa_lat_call(gq, gkv, sink_rows, dst, *, layer, lat_rows):
    b, r, w = gq.shape
    tq = GQA_Q_TILE
    in_specs = [
        pl.BlockSpec((None, tq, w), lambda bi, j: (bi, j, 0)),
        pl.BlockSpec((None, r, w), lambda bi, j: (bi, 0, 0)),
        pl.BlockSpec((None, SUBLANES, LANES), lambda bi, j: (layer, 0, 0)),
    ]
    args = [gq, gkv, sink_rows]
    if dst is not None:
        in_specs.append(pl.BlockSpec(memory_space=pl.ANY))
        args.append(dst)
    return pl.pallas_call(
        functools.partial(_gqa_lat_kernel, lat_rows=lat_rows),
        grid=(b, lat_rows // tq),
        in_specs=in_specs,
        out_specs=pl.BlockSpec((None, tq, w), lambda bi, j: (bi, j, 0)),
        out_shape=jax.ShapeDtypeStruct((b, lat_rows if dst is None else r, w), BF16),
        input_output_aliases={} if dst is None else {3: 0},
        compiler_params=_cparams(("parallel", "parallel")),
        name="gqa_latent",
    )(*args)


def _s5_kernel(*refs, reverse, chunk, batch):
    if reverse:
        (u_ref, pm_ref, a_ref, bcat_ref, ccat_ref, o_ref, buf_a, buf_b, u_a, u_b, st) = refs
        rb_ref = pmt_ref = dsk_ref = wglu_ref = bglu_ref = None
    else:
        (u_ref, rb_ref, pm_ref, pmt_ref, a_ref, bcat_ref, ccat_ref, dsk_ref, wglu_ref, bglu_ref,
         o_ref, buf_a, buf_b, u_a, u_b, st) = refs
    rws = chunk * batch
    n_tiles = 2 * S5_LANES // MXU_TILE
    per_tile = chunk // n_tiles

    @pl.when(pl.program_id(0) == 0)
    def _():
        buf_a[...] = jnp.zeros_like(buf_a)
        buf_b[...] = jnp.zeros_like(buf_b)
        u_a[...] = jnp.zeros_like(u_a)
        u_b[...] = jnp.zeros_like(u_b)
        st[...] = jnp.zeros_like(st)

    a_r, a_i = a_ref[0], a_ref[1]

    def stage(pos_in, pos_out, cur, oth, u_cur):
        u_old = u_cur[...]
        ub = _dot(pm_ref[...], u_ref[:, pos_in].reshape(rws, S5_CHANNELS)).astype(BF16)
        u_cur[...] = ub
        sr, si = st[:, :S5_LANES], st[:, S5_LANES:]
        racc = None
        for j in range(n_tiles):
            for k in range(per_tile):
                idx = j * per_tile + k
                off = ((chunk - 1 - idx) if reverse else idx) * batch
                nr = a_r * sr - a_i * si + oth[off:off + batch, :S5_LANES]
                ni = a_r * si + a_i * sr + oth[off:off + batch, S5_LANES:]
                oth[off:off + batch, :S5_LANES] = nr
                oth[off:off + batch, S5_LANES:] = ni
                sr, si = nr, ni
            cols = slice(j * MXU_TILE, (j + 1) * MXU_TILE)
            part = _dot(cur[:, cols].astype(BF16), ccat_ref[cols, :])
            racc = part if racc is None else racc + part
            cur[:, cols] = _dot(ub, bcat_ref[:, cols])
        st[:, :S5_LANES] = sr
        st[:, S5_LANES:] = si
        if reverse:
            o_ref[pos_out] = racc
        else:
            y = jax.nn.gelu(racc + rb_ref[pos_out] + dsk_ref[...] * u_old.astype(F32))
            z = _dot(y.astype(BF16), wglu_ref[...]) + bglu_ref[...]
            o_tb = (z[:, :S5_CHANNELS] * jax.nn.sigmoid(z[:, S5_CHANNELS:])).astype(BF16)
            o_ref[:, pos_out] = _dot(pmt_ref[...], o_tb).astype(BF16).reshape(batch, chunk, S5_CHANNELS)

    first, second = (1, 0) if reverse else (0, 1)
    stage(first, first, buf_a, buf_b, u_a)
    stage(second, second, buf_b, buf_a, u_b)


def _s5_call(u4, rb, pm, pmt, a_ri, b_cat, c_cat, d_skip, w_glu, b_glu, *, layer, reverse, lat_rows):
    batch, n_all, chunk, _ = u4.shape
    rws = chunk * batch
    n_l = lat_rows // chunk
    n_c = n_all - n_l
    assert n_l % 2 == 0 and n_c % 2 == 0
    np_all, np_l, np_c = n_all // 2, n_l // 2, n_c // 2
    dirn = 1 if reverse else 0

    def pair(g):
        g = jnp.clip(g, 0, np_all - 1)
        if reverse:
            return np_all - 1 - g
        return jnp.where(g < np_c, np_l + g, g - np_c)

    def par4(g):
        return (layer, dirn, 0, 0)

    def lyr3(g):
        return (layer, 0, 0)

    in_specs = [pl.BlockSpec((batch, 2, chunk, S5_CHANNELS), lambda g: (0, pair(g), 0, 0))]
    args = [u4]
    if not reverse:
        in_specs.append(pl.BlockSpec((2, rws, S5_CHANNELS), lambda g: (pair(g - 1), 0, 0)))
        args.append(rb)
    in_specs.append(_const_spec((rws, rws), lambda g: (0, 0)))
    args.append(pm)
    if not reverse:
        in_specs.append(_const_spec((rws, rws), lambda g: (0, 0)))
        args.append(pmt)
    in_specs += [
        _const_spec((None, None, 2, batch, S5_LANES), lambda g: (layer, dirn, 0, 0, 0)),
        _const_spec((None, None, S5_CHANNELS, 2 * S5_LANES), par4),
        _const_spec((None, None, 2 * S5_LANES, S5_CHANNELS), par4),
    ]
    args += [a_ri, b_cat, c_cat]
    if reverse:
        out_spec = pl.BlockSpec((2, rws, S5_CHANNELS), lambda g: (pair(g - 1), 0, 0))
        out_shape = jax.ShapeDtypeStruct((np_all * 2, rws, S5_CHANNELS), F32)
    else:
        in_specs += [
            _const_spec((None, 1, S5_CHANNELS), lyr3),
            _const_spec((None, S5_CHANNELS, 2 * S5_CHANNELS), lyr3),
            _const_spec((None, 1, 2 * S5_CHANNELS), lyr3),
        ]
        args += [d_skip, w_glu, b_glu]
        out_spec = pl.BlockSpec((batch, 2, chunk, S5_CHANNELS), lambda g: (0, pair(g - 1), 0, 0))
        out_shape = jax.ShapeDtypeStruct(u4.shape, BF16)
    return pl.pallas_call(
        functools.partial(_s5_kernel, reverse=reverse, chunk=chunk, batch=batch),
        grid=(np_all + 1,),
        in_specs=in_specs,
        out_specs=out_spec,
        out_shape=out_shape,
        scratch_shapes=[
            pltpu.VMEM((rws, 2 * S5_LANES), F32),
            pltpu.VMEM((rws, 2 * S5_LANES), F32),
            pltpu.VMEM((rws, S5_CHANNELS), BF16),
            pltpu.VMEM((rws, S5_CHANNELS), BF16),
            pltpu.VMEM((batch, 2 * S5_LANES), F32),
        ],
        compiler_params=_cparams(("arbitrary",)),
        name="s5_bwd" if reverse else "s5_fwd",
    )(*args)


def _merge_kernel(x_ref, mod_ref, modt_ref, gpre_ref, gpost_ref, a_ref, b_ref, c_ref, d_ref, wg_ref, wb_ref, wo_ref,
                  o_ref, *, tail):
    tm, d = x_ref.shape
    ms = (mod_ref[...], modt_ref[...])
    pieces = _mod_pieces(tm, tail)
    h = jnp.concatenate(
        [(_rms(x_ref[r0:r1], gpre_ref[1:2]) * (1.0 + ms[k][4:5]) + ms[k][3:4]).astype(BF16) for r0, r1, k in pieces],
        axis=0)
    merged = None
    for i, br in enumerate((a_ref, b_ref, c_ref, d_ref)):
        gate = jax.nn.sigmoid(_dot(h, wg_ref[:, i * d:(i + 1) * d]))
        term = gate * _dot(br[...], wb_ref[i])
        merged = term if merged is None else merged + term
    y = _dot(merged.astype(BF16), wo_ref[...])
    for r0, r1, k in pieces:
        o_ref[r0:r1] = x_ref[r0:r1] + ms[k][5:6] * _rms(y[r0:r1], gpost_ref[1:2])


def _merge_call(x3, mods, norm_pre, norm_post, a3, b3, c3, d3, wg_bf, wb_bf, wo_bf, *, layer, tm, n_blk, ctx_rows):
    b, _, d = x3.shape
    bj = _lag_block(b, n_blk, 0)

    def row(q):
        return (*bj(q), 0)

    def lyr3(q):
        return (layer, 0, 0)

    return pl.pallas_call(
        functools.partial(_merge_kernel, tail=ctx_rows),
        grid=(b * n_blk,),
        in_specs=_lag_specs(b, tm, d, n_blk, layer, ctx_rows > 0, 0) + [
            _const_spec((None, 3, d), lyr3),
            _const_spec((None, 3, d), lyr3),
            pl.BlockSpec((None, tm, BRANCH_WIDTH), row),
            pl.BlockSpec((None, tm, BRANCH_WIDTH), row),
            pl.BlockSpec((None, tm, BRANCH_WIDTH), row),
            pl.BlockSpec((None, tm, BRANCH_WIDTH), row),
            _const_spec((None, d, N_BRANCH * d), lyr3),
            _const_spec((None, N_BRANCH, BRANCH_WIDTH, d), lambda q: (layer, 0, 0, 0)),
            _const_spec((None, d, d), lyr3),
        ],
        out_specs=pl.BlockSpec((None, tm, d), row),
        out_shape=jax.ShapeDtypeStruct((b, n_blk * tm, d), F32),
        compiler_params=_cparams(("parallel",)),
        name="merge",
    )(x3, mods, mods, norm_pre, norm_post, a3, b3, c3, d3, wg_bf, wb_bf, wo_bf)


def _rot_cols(w):
    q = w.shape[-1] // 4
    return jnp.concatenate([-w[..., q:2 * q], w[..., 0:q], -w[..., 3 * q:4 * q], w[..., 2 * q:3 * q]], axis=-1)


def _rope_full(rows_n, rot_dim):
    f32 = np.float32
    axis_dim = rot_dim // 2
    inv_freq = (f32(ROPE_THETA) ** (-np.arange(0, axis_dim, 2, dtype=f32) / f32(axis_dim))).astype(f32)
    row = np.repeat(np.arange(rows_n, dtype=f32), GRID_W)
    col = np.tile(np.arange(GRID_W, dtype=f32), rows_n)
    ang_r = row[:, None] * inv_freq[None, :]
    ang_c = col[:, None] * inv_freq[None, :]
    cos = np.concatenate([np.cos(ang_r), np.cos(ang_r), np.cos(ang_c), np.cos(ang_c)], axis=-1)
    sin = np.concatenate([np.sin(ang_r), np.sin(ang_r), np.sin(ang_c), np.sin(ang_c)], axis=-1)
    return cos.astype(f32), sin.astype(f32)


def _rope_table(l, tm):
    f32 = np.float32
    cg, sg = _rope_full(l // GRID_W, HEAD_DIM)
    cm, sm = _rope_full(l // GRID_W, MLA_ROPE)
    scale = f32((MLA_NOPE + MLA_ROPE) ** -0.5)
    pad = MLA_SLOT - MLA_NOPE - MLA_ROPE

    def build(cg, sg, cm, sm):
        n = cg.shape[0]
        ones = np.ones((n, MLA_NOPE), f32)
        zeros = np.zeros((n, MLA_NOPE), f32)
        return np.concatenate([
            cg, cg, sg, sg,
            scale * ones, scale * cm, zeros[:, :pad],
            zeros, scale * sm, zeros[:, :pad],
            cm, sm, zeros,
        ], axis=-1)

    lat = build(cg, sg, cm, sm)
    one_g, zero_g = np.ones((tm, HEAD_DIM), f32), np.zeros((tm, HEAD_DIM), f32)
    ctx = build(one_g, zero_g, one_g[:, :MLA_ROPE], zero_g[:, :MLA_ROPE])
    return jnp.asarray(np.concatenate([lat, ctx], axis=0))


GQA_HEAD_ORDER = (0, 2, 1, 3)


def _proj_cols(w):
    o = 0

    def take(n):
        nonlocal o
        v = w[..., o:o + n]
        o += n
        return v

    kvl, kpe = take(MLA_KV_LORA), take(MLA_ROPE)
    gk, gv = take(GQA_KV_HEADS * HEAD_DIM), take(GQA_KV_HEADS * HEAD_DIM)
    u, ql, gq, z = take(S5_CHANNELS), take(MLA_Q_LORA), take(GQA_Q_HEADS * HEAD_DIM), take(2 * GMLP_WIDTH)
    gate = w[..., o:]
    gqh = [gq[..., i * HEAD_DIM:(i + 1) * HEAD_DIM] for i in range(GQA_Q_HEADS)]
    gkh = [gk[..., i * HEAD_DIM:(i + 1) * HEAD_DIM] for i in range(GQA_KV_HEADS)]
    pad = jnp.zeros(w.shape[:-1] + (PC_U - PC_A - MLA_KV_LORA - 2 * MLA_ROPE,), w.dtype)
    wp = jnp.concatenate([z, kvl, kpe, _rot_cols(kpe), pad, u, ql]
                         + [gqh[i] for i in GQA_HEAD_ORDER] + [_rot_cols(gqh[i]) for i in GQA_HEAD_ORDER]
                         + [gk] + [_rot_cols(h) for h in gkh] + [gv], axis=-1)
    return wp, gate


def _wprep_kernel(wt_ref, wp_ref, wg_ref):
    n = wt_ref.shape[0]
    full = n // LANES * LANES
    pieces = [wt_ref[r0:r0 + LANES, :].T for r0 in range(0, full, LANES)]
    if full < n:
        rest = jnp.concatenate([wt_ref[full:, :], jnp.zeros((LANES - (n - full), wt_ref.shape[1]), F32)], axis=0)
        pieces.append(rest.T[:, :n - full])
    wp, wg = _proj_cols(jnp.concatenate(pieces, axis=1))
    wp_ref[...] = wp.astype(BF16)
    wg_ref[...] = wg.astype(BF16)


def _wprep_call(w_in):
    depth, d, n = w_in.shape
    tr = 256
    n_gate = N_BRANCH * d
    return pl.pallas_call(
        _wprep_kernel,
        grid=(depth, d // tr),
        in_specs=[pl.BlockSpec((None, n, tr), lambda l, i: (l, 0, i))],
        out_specs=[pl.BlockSpec((None, tr, PC_END), lambda l, i: (l, i, 0)),
                   pl.BlockSpec((None, tr, n_gate), lambda l, i: (l, i, 0))],
        out_shape=[jax.ShapeDtypeStruct((depth, d, PC_END), BF16), jax.ShapeDtypeStruct((depth, d, n_gate), BF16)],
        compiler_params=_cparams(("parallel", "parallel")),
        name="w_prep",
    )(jnp.swapaxes(w_in, 1, 2))


def _mla_kv_weight(w_ukv):
    depth = w_ukv.shape[0]
    wh = w_ukv.reshape(depth, MLA_KV_LORA, MLA_HEADS, MLA_NOPE + MLA_V)
    k_nope, v = wh[..., :MLA_NOPE], wh[..., MLA_NOPE:]
    kslot = jnp.concatenate([k_nope, jnp.zeros((depth, MLA_KV_LORA, MLA_HEADS, MLA_SLOT - MLA_NOPE), F32)], axis=-1)
    top = jnp.concatenate([kslot.reshape(depth, MLA_KV_LORA, -1), v.reshape(depth, MLA_KV_LORA, -1)], axis=-1)
    eye = jnp.eye(MLA_ROPE, dtype=F32)
    pe_slot = jnp.concatenate([jnp.zeros((MLA_ROPE, MLA_NOPE), F32), eye,
                               jnp.zeros((MLA_ROPE, MLA_SLOT - MLA_NOPE - MLA_ROPE), F32)], axis=-1)
    pe_rows = jnp.concatenate([jnp.tile(pe_slot, (1, MLA_HEADS)), jnp.zeros((MLA_ROPE, MLA_HEADS * MLA_V), F32)],
                              axis=-1)
    pe_rows = jnp.broadcast_to(pe_rows, (depth,) + pe_rows.shape)
    tail = jnp.zeros((depth, 256 - MLA_KV_LORA - 2 * MLA_ROPE, MLA_KV_COLS), F32)
    return jnp.concatenate([top, pe_rows, pe_rows, tail], axis=1).astype(BF16)


def _mla_q_weight(w_uq):
    depth = w_uq.shape[0]
    wh = w_uq.reshape(depth, MLA_Q_LORA, MLA_HEADS, MLA_NOPE + MLA_ROPE)
    nope, pe = wh[..., :MLA_NOPE], wh[..., MLA_NOPE:]
    pad = jnp.zeros((depth, MLA_Q_LORA, MLA_HEADS, MLA_SLOT - MLA_NOPE - MLA_ROPE), F32)
    full = jnp.concatenate([nope, pe, pad], axis=-1).reshape(depth, MLA_Q_LORA, -1)
    rot = jnp.concatenate([jnp.zeros_like(nope), _rot_cols(pe), pad], axis=-1).reshape(depth, MLA_Q_LORA, -1)
    return jnp.concatenate([full, rot], axis=-1).astype(BF16)


def _block_diag(w):
    g, a, b = w.shape[-3:]
    lead = w.shape[:-3]
    cols = jnp.swapaxes(w, -3, -2).reshape(lead + (a, g * b))
    tiled = jnp.tile(cols, (1,) * len(lead) + (g, 1))
    same = (jnp.arange(g * a)[:, None] // a) == (jnp.arange(g * b)[None, :] // b)
    return jnp.where(same, tiled, jnp.zeros((), w.dtype))


def kernel(x, c, ctx, c_ctx, w_ada, b_ada, norm_pre, norm_post, w_ffn_in, w_ffn_out, w_in, mla_q_norm, mla_w_uq,
           mla_kv_norm, mla_w_ukv, gqa_sink, s5_lam_re, s5_lam_im, s5_log_dt, s5_b_re, s5_b_im, s5_c_re, s5_c_im,
           s5_d, s5_w_glu, s5_b_glu, gmlp_norm, gmlp_w_s, gmlp_b_s, w_branch, w_out):
    b, l, d = x.shape
    cl = ctx.shape[1]
    r = l + cl
    depth = w_ada.shape[0]
    assert b == SUBLANES and b < MOD_ROWS
    assert l % GRID_W == 0 and l >= 3 * W_BLOCK and l % cl == 0
    assert l % GMLP_CHUNK == 0 and WIDE_TILE % GMLP_CHUNK == 0 and l % MLA_Q_TILE == 0 and l % GQA_Q_TILE == 0
    assert l % S5_CHUNK == 0 and cl % S5_CHUNK == 0 and l % LAT_TILE == 0
    assert r % WIDE_TILE == 0 and cl <= WIDE_TILE

    wi_bf, wo_bf_ffn = w_ffn_in[0, 0].astype(BF16), w_ffn_out[0, 0].astype(BF16)
    wp_bf, wg_bf = _wprep_call(w_in)
    wkv_bf = _mla_kv_weight(mla_w_ukv)
    wq_bf = _mla_q_weight(mla_w_uq)
    head_order = jnp.array(GQA_HEAD_ORDER)
    wb1 = w_branch[:, 1].reshape(depth, GQA_Q_HEADS, HEAD_DIM, d)[:, head_order].reshape(depth, BRANCH_WIDTH, d)
    wb_bf = jnp.concatenate([w_branch[:, :1], wb1[:, None], w_branch[:, 2:]], axis=1).astype(BF16)
    wo_bf = w_out.astype(BF16)
    sink_rows = jnp.broadcast_to(
        jnp.concatenate([gqa_sink[:, head_order], jnp.zeros((depth, SUBLANES - GQA_Q_HEADS), F32)], axis=1)[:, :, None],
        (depth, SUBLANES, LANES))
    ws_bf = jnp.transpose(gmlp_w_s, (0, 2, 1, 3)).reshape(depth, GMLP_CHUNK, GMLP_GROUPS * GMLP_CHUNK).astype(BF16)
    bs_f = jnp.repeat(jnp.transpose(gmlp_b_s, (0, 2, 1)), GMLP_WIDTH // GMLP_GROUPS, axis=2)
    tab = _rope_table(l, cl)
    mla_kv_norm3 = mla_kv_norm[:, None, :]
    mla_q_norm3 = mla_q_norm[:, None, :]
    gmlp_norm3 = gmlp_norm[:, None, :]
    s5_d3 = s5_d[:, None, :]
    s5_b_glu3 = s5_b_glu[:, None, :]
    wglu_bf = s5_w_glu.astype(BF16)
    tb = jnp.arange(S5_CHUNK * b)
    pm = (tb[None, :] == ((tb % b) * S5_CHUNK + tb // b)[:, None]).astype(BF16)
    pmt = pm.T

    hg, p, g = S5_GROUP, S5_STATE, S5_GROUPS
    n_par = depth * 2 * g
    rep = lambda t: jnp.repeat(t.reshape(n_par, p), hg, axis=1)
    ldt = jnp.broadcast_to(s5_log_dt[..., None], (depth, 2, g, p))
    a_re_x, a_im_x, bb_re, bb_im = _s5_disc_call(
        rep(s5_lam_re), rep(s5_lam_im), rep(ldt), s5_b_re.reshape(n_par, p * hg), s5_b_im.reshape(n_par, p * hg))
    a_ri = jnp.broadcast_to(
        jnp.stack([a_re_x[:, ::hg].reshape(depth, 2, g * p), a_im_x[:, ::hg].reshape(depth, 2, g * p)], axis=2)
        [:, :, :, None, :], (depth, 2, 2, b, g * p))
    bb_re = jnp.swapaxes(bb_re.reshape(depth, 2, g, p, hg), -1, -2)
    bb_im = jnp.swapaxes(bb_im.reshape(depth, 2, g, p, hg), -1, -2)
    s5_b_cat = jnp.concatenate([_block_diag(bb_re), _block_diag(bb_im)], axis=-1).astype(BF16)
    s5_c_cat = jnp.concatenate([_block_diag(jnp.swapaxes(s5_c_re, -1, -2)),
                                _block_diag(jnp.swapaxes(-s5_c_im, -1, -2))], axis=-2).astype(BF16)

    cs = jnp.concatenate([c, c_ctx[None, :], jnp.zeros((MOD_ROWS - b - 1, d), F32)], axis=0)
    mods = _ada_call(cs, w_ada, b_ada[:, None, :]).reshape(depth, MOD_ROWS, N_MOD, d)

    xs = jnp.concatenate([x, ctx], axis=1)
    n_wide = r // WIDE_TILE
    for layer in range(depth):
        last = layer == depth - 1
        xs, wi_bf, wo_bf_ffn = _ffn_call(xs, mods, norm_pre, norm_post, wi_bf, wo_bf_ffn,
                                         (w_ffn_in, w_ffn_out, layer, 1),
                                         layer=layer, s=0, tm=WIDE_TILE, n_blk=n_wide, ctx_rows=cl)
        mq, mkv, gq, gkv, u3, d3 = _proj_call(
            xs, mods, norm_pre, wp_bf, tab, mla_kv_norm3, wkv_bf, mla_q_norm3, wq_bf, gmlp_norm3, ws_bf, bs_f,
            layer=layer, lat_rows=l)
        a_dst = None if last else _mla_ctx_call(mq, mkv, lat_rows=l)
        b_dst = None if last else _gqa_ctx_call(gq, gkv, sink_rows, layer=layer, lat_rows=l)
        a3 = _mla_lat_call(mq, mkv, a_dst, lat_rows=l)
        b3 = _gqa_lat_call(gq, gkv, sink_rows, b_dst, layer=layer, lat_rows=l)
        u4 = u3.reshape(b, r // S5_CHUNK, S5_CHUNK, S5_CHANNELS)
        rb = _s5_call(u4, None, pm, None, a_ri, s5_b_cat, s5_c_cat, None, None, None,
                      layer=layer, reverse=True, lat_rows=l)
        c3 = _s5_call(u4, rb, pm, pmt, a_ri, s5_b_cat, s5_c_cat, s5_d3, wglu_bf, s5_b_glu3,
                      layer=layer, reverse=False, lat_rows=l).reshape(b, r, S5_CHANNELS)
        if last:
            tm, n_blk, ctx_rows = LAT_TILE, l // LAT_TILE, 0
        else:
            tm, n_blk, ctx_rows = WIDE_TILE, n_wide, cl
        xs = _merge_call(xs, mods, norm_pre, norm_post, a3, b3, c3, d3, wg_bf, wb_bf, wo_bf,
                         layer=layer, tm=tm, n_blk=n_blk, ctx_rows=ctx_rows)
        outs = _ffn_call(xs, mods, norm_pre, norm_post, wi_bf, wo_bf_ffn,
                         None if last else (w_ffn_in, w_ffn_out, layer + 1, 0),
                         layer=layer, s=2, tm=tm, n_blk=n_blk, ctx_rows=ctx_rows)
        xs = outs[0]
        if not last:
            wi_bf, wo_bf_ffn = outs[1], outs[2]
    return xs
```

```python
import functools

import jax
import jax.numpy as jnp
import numpy as np
from jax import lax
from jax.experimental import pallas as pl
from jax.experimental.pallas import tpu as pltpu

F32 = jnp.float32
BF16 = jnp.bfloat16

GRID_W = 64
HEAD_DIM = 64
ROPE_THETA = 10000.0
RMS_EPS = 1e-6
LN_EPS = 1e-5
NEG_INF = -1e30
MLA_HEADS = 4
MLA_Q_LORA = 256
MLA_KV_LORA = 128
MLA_NOPE = 64
MLA_ROPE = 32
MLA_V = 64
GQA_Q_HEADS = 4
GQA_KV_HEADS = 2
WINDOW = 128
W_BLOCK = 128
S5_CHANNELS = 256
S5_GROUP = 16
S5_GROUPS = S5_CHANNELS // S5_GROUP
S5_STATE = 64
GMLP_WIDTH = 256
GMLP_CHUNK = 128
GMLP_GROUPS = 4
N_BRANCH = 4
BRANCH_WIDTH = 256
N_MOD = 9

LANES = 128
SUBLANES = 8
MXU_TILE = 256
VMEM_LIMIT_BYTES = 56 * 1024 * 1024

WIDE_TILE = 768
XWIDE_TILE = 1152
LAT_TILE = 512
MLA_Q_TILE = 512
GQA_Q_TILE = 512
S5_CHUNK = 64
MOD_ROWS = 16
MLA_SLOT = LANES
S5_LANES = S5_GROUPS * S5_STATE

PC_Z = 0
PC_A = PC_Z + 2 * GMLP_WIDTH
PC_U = PC_A + 256
PC_QL = PC_U + S5_CHANNELS
PC_GQ = PC_QL + MLA_Q_LORA
PC_GQR = PC_GQ + GQA_Q_HEADS * HEAD_DIM
PC_GK = PC_GQR + GQA_Q_HEADS * HEAD_DIM
PC_GKR = PC_GK + GQA_KV_HEADS * HEAD_DIM
PC_GV = PC_GKR + GQA_KV_HEADS * HEAD_DIM
PC_END = PC_GV + GQA_KV_HEADS * HEAD_DIM

TC_GC, TC_GS, TC_MQC, TC_MQS, TC_MKT, TC_END = 0, 128, 256, 384, 512, 640

MLA_K_COLS = MLA_HEADS * MLA_SLOT
MLA_KV_COLS = MLA_K_COLS + MLA_HEADS * MLA_V


def _cparams(sem):
    return pltpu.CompilerParams(dimension_semantics=sem, vmem_limit_bytes=VMEM_LIMIT_BYTES)


def _const_spec(shape, index_map):
    return pl.BlockSpec(shape, index_map, pipeline_mode=pl.Buffered(1))


def _rms(x, gain):
    return x * lax.rsqrt(jnp.mean(x * x, axis=-1, keepdims=True) + RMS_EPS) * gain


def _dot(a, b):
    return jnp.dot(a, b, preferred_element_type=F32)


def _dot_nt(a, b):
    return lax.dot_general(a, b, (((1,), (1,)), ((), ())), preferred_element_type=F32)


def _ada_kernel(cs_ref, w_ref, b_ref, o_ref):
    cs = cs_ref[...]
    s = (cs * jax.nn.sigmoid(cs)).astype(BF16)
    o_ref[...] = _dot(s, w_ref[...].astype(BF16)) + b_ref[...]


def _ada_call(cs, w_ada, b_ada3):
    depth, d, n = w_ada.shape
    tn = d
    return pl.pallas_call(
        _ada_kernel,
        grid=(depth, n // tn),
        in_specs=[
            pl.BlockSpec((MOD_ROWS, d), lambda l, j: (0, 0)),
            pl.BlockSpec((None, d, tn), lambda l, j: (l, 0, j)),
            pl.BlockSpec((None, 1, tn), lambda l, j: (l, 0, j)),
        ],
        out_specs=pl.BlockSpec((None, MOD_ROWS, tn), lambda l, j: (l, 0, j)),
        out_shape=jax.ShapeDtypeStruct((depth, MOD_ROWS, n), F32),
        compiler_params=_cparams(("parallel", "parallel")),
        name="ada_mod",
    )(cs, w_ada, b_ada3)


def _s5_disc_kernel(lre_ref, lim_ref, ldt_ref, bre_ref, bim_ref, are_ref, aim_ref, bbre_ref, bbim_ref):
    lam_re = jnp.minimum(lre_ref[...], -1e-4)
    lam_im = lim_ref[...]
    dt = jnp.exp(ldt_ref[...])
    mag = jnp.exp(lam_re * dt)
    a_re = mag * jnp.cos(lam_im * dt)
    a_im = mag * jnp.sin(lam_im * dt)
    nr, ni = a_re - 1.0, a_im
    den = lam_re * lam_re + lam_im * lam_im
    coef_re = (nr * lam_re + ni * lam_im) / den
    coef_im = (ni * lam_re - nr * lam_im) / den
    b_re, b_im = bre_ref[...], bim_ref[...]
    are_ref[...] = a_re
    aim_ref[...] = a_im
    bbre_ref[...] = coef_re * b_re - coef_im * b_im
    bbim_ref[...] = coef_re * b_im + coef_im * b_re


def _s5_disc_call(lre, lim, ldt, bre, bim):
    shp = jax.ShapeDtypeStruct(lre.shape, F32)
    return pl.pallas_call(_s5_disc_kernel, out_shape=(shp, shp, shp, shp), name="s5_disc")(lre, lim, ldt, bre, bim)


def _mod_pieces(tm, tail):
    return ((0, tm - tail, 0), (tm - tail, tm, 1)) if tail else ((0, tm, 0),)


def _lag_block(b, n_blk, lag):
    def bj(q):
        q = jnp.clip(q - lag, 0, b * n_blk - 1)
        return q // n_blk, q % n_blk
    return bj


def _lag_specs(b, tm, d, n_blk, layer, ctx_tail, lag):
    bj = _lag_block(b, n_blk, lag)

    def tail_row(q):
        bi, j = bj(q)
        return jnp.where(j == n_blk - 1, b, bi) if ctx_tail else bi

    return [
        pl.BlockSpec((None, tm, d), lambda q: (*bj(q), 0)),
        pl.BlockSpec((None, None, N_MOD, d), lambda q: (layer, bj(q)[0], 0, 0)),
        pl.BlockSpec((None, None, N_MOD, d), lambda q: (layer, tail_row(q), 0, 0)),
    ]


def _ffn_kernel(xn_ref, modn_ref, modnt_ref, xo_ref, modo_ref, modot_ref, gpre_ref, gpost_ref, wi_ref, wo_ref, *rest,
                s, d_ff, chunks, tail, n_steps, cast_next):
    if cast_next:
        wi_next_ref, wo_next_ref, o_ref, wi_cast_ref, wo_cast_ref, h_ref, y_ref = rest
    else:
        o_ref, h_ref, y_ref = rest
    q = pl.program_id(0)
    pieces = _mod_pieces(xn_ref.shape[0], tail)

    def cast_slabs():
        if cast_next:
            wi_cast_ref[...] = wi_next_ref[...].astype(BF16)
            wo_cast_ref[...] = wo_next_ref[...].astype(BF16)

    def pre_norm():
        ms = (modn_ref[...], modnt_ref[...])
        return jnp.concatenate(
            [(_rms(xn_ref[r0:r1], gpre_ref[s:s + 1]) * (1.0 + ms[k][3 * s + 1:3 * s + 2]) + ms[k][3 * s:3 * s + 1])
             .astype(BF16) for r0, r1, k in pieces], axis=0)

    def post_norm():
        ms = (modo_ref[...], modot_ref[...])
        return jnp.concatenate(
            [xo_ref[r0:r1] + 0.5 * ms[k][3 * s + 2:3 * s + 3] * _rms(y_ref[r0:r1], gpost_ref[s:s + 1])
             for r0, r1, k in pieces], axis=0)

    def up(h, c0, c1):
        a = _dot(h, wi_ref[:, c0:c1])
        g = _dot(h, wi_ref[:, d_ff + c0:d_ff + c1])
        return (a * jax.nn.sigmoid(a) * g).astype(BF16)

    @pl.when(q == 0)
    def _():
        h_ref[...] = pre_norm()
        y_ref[...] = jnp.zeros_like(y_ref)
        cast_slabs()

    @pl.when(jnp.logical_and(q > 0, q < n_steps - 1))
    def _():
        always = q < n_steps
        h = h_ref[...]
        acts = [up(h, *chunks[0])]
        cast_slabs()
        out = post_norm()
        o_ref[...] = out
        h = jnp.where(always, h, out.astype(BF16))
        acts += [up(h, c0, c1) for c0, c1 in chunks[1:]]
        h_new = pre_norm()
        h_ref[...] = h_new
        w0 = chunks[0][1] - chunks[0][0]
        acts[0] = jnp.concatenate([jnp.where(always, acts[0][:, :h_new.shape[1]], h_new), acts[0][:, h_new.shape[1]:]],
                                  axis=1) if w0 > h_new.shape[1] else jnp.where(always, acts[0], h_new[:, :w0])
        y = None
        for act, (c0, c1) in zip(acts, chunks):
            part = _dot(act, wo_ref[c0:c1, :])
            y = part if y is None else y + part
        y_ref[...] = y

    @pl.when(q == n_steps - 1)
    def _():
        o_ref[...] = post_norm()
        cast_slabs()


FFN_CAST_SLABS = 16


def _ffn_call(x3, mods, norm_pre, norm_post, w_in_bf, w_out_bf, w_next, *, layer, s, tm, n_blk, ctx_rows):
    b, _, d = x3.shape
    d_ff = w_out_bf.shape[0]
    half = (d_ff // MXU_TILE + 1) // 2 * MXU_TILE
    chunks = ((0, half), (half, d_ff)) if 0 < half < d_ff else ((0, d_ff),)
    n = b * n_blk
    ns = FFN_CAST_SLABS
    while ns > n + 2:
        ns //= 2
    assert d % (ns * 2 * SUBLANES) == 0 and d_ff % (ns * 2 * SUBLANES) == 0
    kern = functools.partial(_ffn_kernel, s=s, d_ff=d_ff, chunks=chunks, tail=ctx_rows, n_steps=n + 2,
                             cast_next=w_next is not None)

    def specs(lag):
        return _lag_specs(b, tm, d, n_blk, layer, ctx_rows > 0, lag)

    in_specs = specs(0) + specs(2) + [
        _const_spec((None, 3, d), lambda q: (layer, 0, 0)),
        _const_spec((None, 3, d), lambda q: (layer, 0, 0)),
        _const_spec((d, 2 * d_ff), lambda q: (0, 0)),
        _const_spec((d_ff, d), lambda q: (0, 0)),
    ]
    args = [x3, mods, mods, x3, mods, mods, norm_pre, norm_post, w_in_bf, w_out_bf]
    out_specs = [specs(2)[0]]
    out_shape = [jax.ShapeDtypeStruct((b, n_blk * tm, d), F32)]
    if w_next is not None:
        wi_f32, wo_f32, l_next, which_next = w_next

        def slab(q):
            return jnp.minimum(q, ns - 1)

        in_specs += [
            pl.BlockSpec((None, None, d // ns, 2 * d_ff), lambda q: (l_next, which_next, slab(q), 0)),
            pl.BlockSpec((None, None, d_ff // ns, d), lambda q: (l_next, which_next, slab(q), 0)),
        ]
        args += [wi_f32, wo_f32]
        out_specs += [pl.BlockSpec((d // ns, 2 * d_ff), lambda q: (slab(q), 0)),
                      pl.BlockSpec((d_ff // ns, d), lambda q: (slab(q), 0))]
        out_shape += [jax.ShapeDtypeStruct((d, 2 * d_ff), BF16), jax.ShapeDtypeStruct((d_ff, d), BF16)]
    return pl.pallas_call(
        kern,
        grid=(n + 2,),
        in_specs=in_specs,
        out_specs=out_specs,
        out_shape=out_shape,
        scratch_shapes=[pltpu.VMEM((tm, d), BF16), pltpu.VMEM((tm, d), F32)],
        compiler_params=_cparams(("arbitrary",)),
        name="ffn",
    )(*args)


def _proj_kernel(x_ref, mod_ref, modt_ref, gpre_ref, wp_ref, tab_ref, gkvn_ref, wkv_ref, gqn_ref, wq_ref,
                 gmn_ref, ws_ref, bs_ref,
                 mq_ref, mkv_ref, gq_ref, gkv_ref, u_ref, d_ref, *, tail):
    ms = (mod_ref[...], modt_ref[...])
    h = jnp.concatenate(
        [(_rms(x_ref[r0:r1], gpre_ref[1:2]) * (1.0 + ms[k][4:5]) + ms[k][3:4]).astype(BF16)
         for r0, r1, k in _mod_pieces(x_ref.shape[0], tail)], axis=0)
    p = _dot(h, wp_ref[...])
    tab = tab_ref[...]
    gc, gs = tab[:, TC_GC:TC_GS], tab[:, TC_GS:TC_MQC]
    mqc, mqs, mkt = tab[:, TC_MQC:TC_MQS], tab[:, TC_MQS:TC_MKT], tab[:, TC_MKT:TC_END]

    kvl = p[:, PC_A:PC_A + MLA_KV_LORA]
    kvn = _rms(kvl, gkvn_ref[...])
    pe = p[:, PC_A + MLA_KV_LORA:PC_U] * mkt
    a2 = jnp.concatenate([kvn, pe], axis=-1).astype(BF16)
    mkv_ref[...] = _dot(a2, wkv_ref[...]).astype(BF16)

    qn = _rms(p[:, PC_QL:PC_GQ], gqn_ref[...]).astype(BF16)
    q = _dot(qn, wq_ref[...])
    qs = [q[:, hh * MLA_SLOT:(hh + 1) * MLA_SLOT] * mqc
          + q[:, MLA_K_COLS + hh * MLA_SLOT:MLA_K_COLS + (hh + 1) * MLA_SLOT] * mqs
          for hh in range(MLA_HEADS)]
    mq_ref[...] = jnp.concatenate(qs, axis=-1).astype(BF16)

    gq = [(p[:, PC_GQ + j * LANES:PC_GQ + (j + 1) * LANES] * gc
           + p[:, PC_GQR + j * LANES:PC_GQR + (j + 1) * LANES] * gs) * (HEAD_DIM ** -0.5)
          for j in range(GQA_Q_HEADS * HEAD_DIM // LANES)]
    gq_ref[...] = jnp.concatenate(gq, axis=-1).astype(BF16)
    gk = p[:, PC_GK:PC_GKR] * gc + p[:, PC_GKR:PC_GV] * gs
    gkv_ref[...] = jnp.concatenate([gk, p[:, PC_GV:PC_END]], axis=-1).astype(BF16)

    u_ref[...] = p[:, PC_U:PC_QL].astype(BF16)

    zz = jax.nn.gelu(p[:, PC_Z:PC_A])
    ug, v = zz[:, :GMLP_WIDTH], zz[:, GMLP_WIDTH:]
    vc = v - jnp.mean(v, axis=-1, keepdims=True)
    vn = vc * lax.rsqrt(jnp.mean(vc * vc, axis=-1, keepdims=True) + LN_EPS) * gmn_ref[...]
    group_of_lane = lax.broadcasted_iota(jnp.int32, (GMLP_CHUNK, GMLP_WIDTH), 1) // (GMLP_WIDTH // GMLP_GROUPS)
    outs = []
    for ci in range(x_ref.shape[0] // GMLP_CHUNK):
        r0 = ci * GMLP_CHUNK
        vck = vn[r0:r0 + GMLP_CHUNK]
        vbd = jnp.concatenate([jnp.where(group_of_lane == g, vck, 0.0) for g in range(GMLP_GROUPS)], axis=0)
        mixed = _dot(ws_ref[...], vbd.astype(BF16)) + bs_ref[...]
        outs.append(ug[r0:r0 + GMLP_CHUNK] * mixed)
    d_ref[...] = jnp.concatenate(outs, axis=0).astype(BF16)


def _proj_call(x3, mods, norm_pre, wp_bf, tab, mla_kv_norm3, wkv_bf, mla_q_norm3, wq_bf, gmlp_norm3, ws_bf, bs_f,
               *, layer, lat_rows):
    b, r, d = x3.shape
    tm = XWIDE_TILE
    n_blk = r // tm
    bj = _lag_block(b, n_blk, 0)

    def row(q):
        return (*bj(q), 0)

    def lyr3(q):
        return (layer, 0, 0)

    widths = (MLA_K_COLS, MLA_KV_COLS, GQA_Q_HEADS * HEAD_DIM, 2 * GQA_KV_HEADS * HEAD_DIM, S5_CHANNELS, GMLP_WIDTH)
    return pl.pallas_call(
        functools.partial(_proj_kernel, tail=r - lat_rows),
        grid=(b * n_blk,),
        in_specs=_lag_specs(b, tm, d, n_blk, layer, True, 0) + [
            _const_spec((None, 3, d), lyr3),
            _const_spec((None, d, PC_END), lyr3),
            pl.BlockSpec((tm, TC_END), lambda q: (bj(q)[1], 0)),
            _const_spec((None, 1, MLA_KV_LORA), lyr3),
            _const_spec((None, 256, MLA_KV_COLS), lyr3),
            _const_spec((None, 1, MLA_Q_LORA), lyr3),
            _const_spec((None, MLA_Q_LORA, 2 * MLA_K_COLS), lyr3),
            _const_spec((None, 1, GMLP_WIDTH), lyr3),
            _const_spec((None, GMLP_CHUNK, GMLP_GROUPS * GMLP_CHUNK), lyr3),
            _const_spec((None, GMLP_CHUNK, GMLP_WIDTH), lyr3),
        ],
        out_specs=[pl.BlockSpec((None, tm, w), row) for w in widths],
        out_shape=[jax.ShapeDtypeStruct((b, r, w), BF16) for w in widths],
        compiler_params=_cparams(("parallel",)),
        name="in_proj",
    )(x3, mods, mods, norm_pre, wp_bf, tab, mla_kv_norm3, wkv_bf, mla_q_norm3, wq_bf, gmlp_norm3, ws_bf, bs_f)


def _mla_attend(q_ref, kv_ref):
    tq = q_ref.shape[0]
    head_of_lane = lax.broadcasted_iota(jnp.int32, (tq, MLA_HEADS * MLA_V), 1) // MLA_V
    acc = jnp.zeros((tq, MLA_HEADS * MLA_V), F32)
    v = kv_ref[:, MLA_K_COLS:]
    sls = [slice(hh * MLA_SLOT, (hh + 1) * MLA_SLOT) for hh in range(MLA_HEADS)]
    scs = [_dot_nt(q_ref[:, sl], kv_ref[:, sl]) for sl in sls]
    ps = [jnp.exp(sc - jnp.max(sc, axis=-1, keepdims=True)) for sc in scs]
    dens = [jnp.sum(p, axis=-1, keepdims=True) for p in ps]
    o_all = _dot(jnp.concatenate([p.astype(BF16) for p in ps], axis=0), v)
    for hh in range(MLA_HEADS):
        acc = jnp.where(head_of_lane == hh, o_all[hh * tq:(hh + 1) * tq] / dens[hh], acc)
    return acc.astype(BF16)


def _mla_lat_kernel(q_ref, kv_ref, *rest):
    rest[-1][...] = _mla_attend(q_ref, kv_ref)


def _mla_lat_call(mq, mkv, dst, *, lat_rows):
    b, r, _ = mq.shape
    tq = MLA_Q_TILE
    w = MLA_HEADS * MLA_V
    in_specs = [
        pl.BlockSpec((None, tq, MLA_K_COLS), lambda bi, j: (bi, j, 0)),
        pl.BlockSpec((None, r, MLA_KV_COLS), lambda bi, j: (bi, 0, 0)),
    ]
    args = [mq, mkv]
    if dst is not None:
        in_specs.append(pl.BlockSpec(memory_space=pl.ANY))
        args.append(dst)
    return pl.pallas_call(
        _mla_lat_kernel,
        grid=(b, lat_rows // tq),
        in_specs=in_specs,
        out_specs=pl.BlockSpec((None, tq, w), lambda bi, j: (bi, j, 0)),
        out_shape=jax.ShapeDtypeStruct((b, lat_rows if dst is None else r, w), BF16),
        input_output_aliases={} if dst is None else {2: 0},
        compiler_params=_cparams(("parallel", "parallel")),
        name="mla_latent",
    )(*args)


def _gqa_blocks(qs, kc, vc, sink_ref, bands):
    tq = qs[0].shape[0]
    kw = GQA_KV_HEADS * HEAD_DIM
    lo = lax.broadcasted_iota(jnp.int32, (tq, kw), 1) < HEAD_DIM
    sk = jnp.concatenate([jnp.broadcast_to(sink_ref[r:r + 1, 0:1], (tq, 1)) for r in range(GQA_Q_HEADS)], axis=0)
    qsts = []
    for q in qs:
        q0, q1 = q[:, :kw], q[:, kw:]
        zero = jnp.zeros_like(q0)
        qsts.append(jnp.concatenate([jnp.where(lo, q0, zero), jnp.where(lo, zero, q0),
                                     jnp.where(lo, q1, zero), jnp.where(lo, zero, q1)], axis=0))
    scs = [_dot_nt(qst, kc) for qst in qsts]
    mxs = [jnp.maximum(jnp.max(sc, axis=-1, keepdims=True), sk) for sc in scs]
    if bands is not None:
        sbs = [jnp.where(jnp.concatenate([valid] * GQA_Q_HEADS, axis=0), _dot_nt(qst, kb), NEG_INF)
               for qst, (kb, _, valid) in zip(qsts, bands)]
        mxs = [jnp.maximum(mx, jnp.max(sb, axis=-1, keepdims=True)) for mx, sb in zip(mxs, sbs)]
    pcs = [jnp.exp(sc - mx) for sc, mx in zip(scs, mxs)]
    dens = [jnp.sum(pc, axis=-1, keepdims=True) + jnp.exp(sk - mx) for pc, mx in zip(pcs, mxs)]
    os_ = [_dot(pc.astype(BF16), vc) for pc in pcs]
    if bands is not None:
        pbs = [jnp.exp(sb - mx) for sb, mx in zip(sbs, mxs)]
        dens = [den + jnp.sum(pb, axis=-1, keepdims=True) for den, pb in zip(dens, pbs)]
        os_ = [o + _dot(pb.astype(BF16), vb) for o, pb, (_, vb, _) in zip(os_, pbs, bands)]
    outs = []
    for o, den in zip(os_, dens):
        o = o / den
        c0 = jnp.where(lo, o[0:tq], o[tq:2 * tq])
        c1 = jnp.where(lo, o[2 * tq:3 * tq], o[3 * tq:4 * tq])
        outs.append(jnp.concatenate([c0, c1], axis=-1))
    return outs


def _gqa_lat_kernel(q_ref, kv_ref, sink_ref, *rest, lat_rows):
    o_ref = rest[-1]
    kw = GQA_KV_HEADS * HEAD_DIM
    kc, vc = kv_ref[lat_rows:, :kw], kv_ref[lat_rows:, kw:]
    nbk = 3 * W_BLOCK
    n_in = q_ref.shape[0] // W_BLOCK
    qs, bands = [], []
    for i in range(n_in):
        n = pl.program_id(1) * n_in + i
        start = pl.multiple_of(jnp.clip((n - 1) * W_BLOCK, 0, lat_rows - nbk), W_BLOCK)
        kb = kv_ref[pl.ds(start, nbk), :kw]
        vb = kv_ref[pl.ds(start, nbk), kw:]
        qpos = n * W_BLOCK + lax.broadcasted_iota(jnp.int32, (W_BLOCK, nbk), 0)
        kpos = start + lax.broadcasted_iota(jnp.int32, (W_BLOCK, nbk), 1)
        bands.append((kb, vb, jnp.abs(qpos - kpos) <= WINDOW))
        qs.append(q_ref[i * W_BLOCK:(i + 1) * W_BLOCK])
    outs = _gqa_blocks(qs, kc, vc, sink_ref, bands)
    o_ref[...] = jnp.concatenate(outs, axis=0).astype(BF16)


def _ctx_attn_kernel(mq_ref, mkv_ref, gq_ref, gkv_ref, sink_ref, a_ref, b_ref, *, lat_rows):
    kw = GQA_KV_HEADS * HEAD_DIM
    tq = gq_ref.shape[0] // 2
    a_ctx = _mla_attend(mq_ref, mkv_ref)
    b_ctx = _gqa_blocks([gq_ref[:tq], gq_ref[tq:]], gkv_ref[:, :kw], gkv_ref[:, kw:], sink_ref, None)
    for o_ref, val in ((a_ref, a_ctx), (b_ref, jnp.concatenate(b_ctx, axis=0).astype(BF16))):
        o_ref[:lat_rows] = jnp.zeros((lat_rows, o_ref.shape[1]), o_ref.dtype)
        o_ref[lat_rows:] = val


def _ctx_attn_call(mq, mkv, gq, gkv, sink_rows, *, layer, lat_rows):
    b, r, w = gq.shape
    c = r - lat_rows
    blk = lat_rows // c
    wa = MLA_HEADS * MLA_V

    def ctx(bi):
        return (bi, blk, 0)

    return pl.pallas_call(
        functools.partial(_ctx_attn_kernel, lat_rows=lat_rows),
        grid=(b,),
        in_specs=[
            pl.BlockSpec((None, c, MLA_K_COLS), ctx),
            pl.BlockSpec((None, c, MLA_KV_COLS), ctx),
            pl.BlockSpec((None, c, w), ctx),
            pl.BlockSpec((None, c, w), ctx),
            pl.BlockSpec((None, SUBLANES, LANES), lambda bi: (layer, 0, 0)),
        ],
        out_specs=[pl.BlockSpec((None, r, wa), lambda bi: (bi, 0, 0)), pl.BlockSpec((None, r, w), lambda bi: (bi, 0, 0))],
        out_shape=[jax.ShapeDtypeStruct((b, r, wa), BF16), jax.ShapeDtypeStruct((b, r, w), BF16)],
        compiler_params=_cparams(("parallel",)),
        name="ctx_attention",
    )(mq, mkv, gq, gkv, sink_rows)


def _gqa_lat_call(gq, gkv, sink_rows, dst, *, layer, lat_rows):
    b, r, w = gq.shape
    tq = GQA_Q_TILE
    in_specs = [
        pl.BlockSpec((None, tq, w), lambda bi, j: (bi, j, 0)),
        pl.BlockSpec((None, r, w), lambda bi, j: (bi, 0, 0)),
        pl.BlockSpec((None, SUBLANES, LANES), lambda bi, j: (layer, 0, 0)),
    ]
    args = [gq, gkv, sink_rows]
    if dst is not None:
        in_specs.append(pl.BlockSpec(memory_space=pl.ANY))
        args.append(dst)
    return pl.pallas_call(
        functools.partial(_gqa_lat_kernel, lat_rows=lat_rows),
        grid=(b, lat_rows // tq),
        in_specs=in_specs,
        out_specs=pl.BlockSpec((None, tq, w), lambda bi, j: (bi, j, 0)),
        out_shape=jax.ShapeDtypeStruct((b, lat_rows if dst is None else r, w), BF16),
        input_output_aliases={} if dst is None else {3: 0},
        compiler_params=_cparams(("parallel", "parallel")),
        name="gqa_latent",
    )(*args)


def _s5_kernel(*refs, reverse, chunk, batch):
    if reverse:
        (u_ref, pm_ref, a_ref, bcat_ref, ccat_ref, o_ref, buf_a, buf_b, u_a, u_b, st) = refs
        rb_ref = pmt_ref = dsk_ref = wglu_ref = bglu_ref = None
    else:
        (u_ref, rb_ref, pm_ref, pmt_ref, a_ref, bcat_ref, ccat_ref, dsk_ref, wglu_ref, bglu_ref,
         o_ref, buf_a, buf_b, u_a, u_b, st) = refs
    rws = chunk * batch
    n_tiles = 2 * S5_LANES // MXU_TILE
    per_tile = chunk // n_tiles

    @pl.when(pl.program_id(0) == 0)
    def _():
        buf_a[...] = jnp.zeros_like(buf_a)
        buf_b[...] = jnp.zeros_like(buf_b)
        u_a[...] = jnp.zeros_like(u_a)
        u_b[...] = jnp.zeros_like(u_b)
        st[...] = jnp.zeros_like(st)

    a_r, a_i = a_ref[0], a_ref[1]

    def stage(pos_in, pos_out, cur, oth, u_cur):
        u_old = u_cur[...]
        ub = _dot(pm_ref[...], u_ref[:, pos_in].reshape(rws, S5_CHANNELS)).astype(BF16)
        u_cur[...] = ub
        sr, si = st[:, :S5_LANES], st[:, S5_LANES:]
        racc = None
        for j in range(n_tiles):
            for k in range(per_tile):
                idx = j * per_tile + k
                off = ((chunk - 1 - idx) if reverse else idx) * batch
                nr = a_r * sr - a_i * si + oth[off:off + batch, :S5_LANES]
                ni = a_r * si + a_i * sr + oth[off:off + batch, S5_LANES:]
                oth[off:off + batch, :S5_LANES] = nr
                oth[off:off + batch, S5_LANES:] = ni
                sr, si = nr, ni
            cols = slice(j * MXU_TILE, (j + 1) * MXU_TILE)
            part = _dot(cur[:, cols].astype(BF16), ccat_ref[cols, :])
            racc = part if racc is None else racc + part
            cur[:, cols] = _dot(ub, bcat_ref[:, cols])
        st[:, :S5_LANES] = sr
        st[:, S5_LANES:] = si
        if reverse:
            o_ref[pos_out] = racc
        else:
            y = jax.nn.gelu(racc + rb_ref[pos_out] + dsk_ref[...] * u_old.astype(F32))
            z = _dot(y.astype(BF16), wglu_ref[...]) + bglu_ref[...]
            o_tb = (z[:, :S5_CHANNELS] * jax.nn.sigmoid(z[:, S5_CHANNELS:])).astype(BF16)
            o_ref[:, pos_out] = _dot(pmt_ref[...], o_tb).astype(BF16).reshape(batch, chunk, S5_CHANNELS)

    first, second = (1, 0) if reverse else (0, 1)
    stage(first, first, buf_a, buf_b, u_a)
    stage(second, second, buf_b, buf_a, u_b)


def _s5_call(u4, rb, pm, pmt, a_ri, b_cat, c_cat, d_skip, w_glu, b_glu, *, layer, reverse, lat_rows):
    batch, n_all, chunk, _ = u4.shape
    rws = chunk * batch
    n_l = lat_rows // chunk
    n_c = n_all - n_l
    assert n_l % 2 == 0 and n_c % 2 == 0
    np_all, np_l, np_c = n_all // 2, n_l // 2, n_c // 2
    dirn = 1 if reverse else 0

    def pair(g):
        g = jnp.clip(g, 0, np_all - 1)
        if reverse:
            return np_all - 1 - g
        return jnp.where(g < np_c, np_l + g, g - np_c)

    def par4(g):
        return (layer, dirn, 0, 0)

    def lyr3(g):
        return (layer, 0, 0)

    in_specs = [pl.BlockSpec((batch, 2, chunk, S5_CHANNELS), lambda g: (0, pair(g), 0, 0))]
    args = [u4]
    if not reverse:
        in_specs.append(pl.BlockSpec((2, rws, S5_CHANNELS), lambda g: (pair(g - 1), 0, 0)))
        args.append(rb)
    in_specs.append(_const_spec((rws, rws), lambda g: (0, 0)))
    args.append(pm)
    if not reverse:
        in_specs.append(_const_spec((rws, rws), lambda g: (0, 0)))
        args.append(pmt)
    in_specs += [
        _const_spec((None, None, 2, batch, S5_LANES), lambda g: (layer, dirn, 0, 0, 0)),
        _const_spec((None, None, S5_CHANNELS, 2 * S5_LANES), par4),
        _const_spec((None, None, 2 * S5_LANES, S5_CHANNELS), par4),
    ]
    args += [a_ri, b_cat, c_cat]
    if reverse:
        out_spec = pl.BlockSpec((2, rws, S5_CHANNELS), lambda g: (pair(g - 1), 0, 0))
        out_shape = jax.ShapeDtypeStruct((np_all * 2, rws, S5_CHANNELS), F32)
    else:
        in_specs += [
            _const_spec((None, 1, S5_CHANNELS), lyr3),
            _const_spec((None, S5_CHANNELS, 2 * S5_CHANNELS), lyr3),
            _const_spec((None, 1, 2 * S5_CHANNELS), lyr3),
        ]
        args += [d_skip, w_glu, b_glu]
        out_spec = pl.BlockSpec((batch, 2, chunk, S5_CHANNELS), lambda g: (0, pair(g - 1), 0, 0))
        out_shape = jax.ShapeDtypeStruct(u4.shape, BF16)
    return pl.pallas_call(
        functools.partial(_s5_kernel, reverse=reverse, chunk=chunk, batch=batch),
        grid=(np_all + 1,),
        in_specs=in_specs,
        out_specs=out_spec,
        out_shape=out_shape,
        scratch_shapes=[
            pltpu.VMEM((rws, 2 * S5_LANES), F32),
            pltpu.VMEM((rws, 2 * S5_LANES), F32),
            pltpu.VMEM((rws, S5_CHANNELS), BF16),
            pltpu.VMEM((rws, S5_CHANNELS), BF16),
            pltpu.VMEM((batch, 2 * S5_LANES), F32),
        ],
        compiler_params=_cparams(("arbitrary",)),
        name="s5_bwd" if reverse else "s5_fwd",
    )(*args)


def _merge_kernel(x_ref, mod_ref, modt_ref, gpre_ref, gpost_ref, a_ref, b_ref, c_ref, d_ref, wg_ref, wb_ref, wo_ref,
                  o_ref, *, tail):
    tm, d = x_ref.shape
    ms = (mod_ref[...], modt_ref[...])
    pieces = _mod_pieces(tm, tail)
    h = jnp.concatenate(
        [(_rms(x_ref[r0:r1], gpre_ref[1:2]) * (1.0 + ms[k][4:5]) + ms[k][3:4]).astype(BF16) for r0, r1, k in pieces],
        axis=0)
    merged = None
    for i, br in enumerate((a_ref, b_ref, c_ref, d_ref)):
        gate = jax.nn.sigmoid(_dot(h, wg_ref[:, i * d:(i + 1) * d]))
        term = gate * _dot(br[...], wb_ref[i])
        merged = term if merged is None else merged + term
    y = _dot(merged.astype(BF16), wo_ref[...])
    for r0, r1, k in pieces:
        o_ref[r0:r1] = x_ref[r0:r1] + ms[k][5:6] * _rms(y[r0:r1], gpost_ref[1:2])


def _merge_call(x3, mods, norm_pre, norm_post, a3, b3, c3, d3, wg_bf, wb_bf, wo_bf, *, layer, tm, n_blk, ctx_rows):
    b, _, d = x3.shape
    bj = _lag_block(b, n_blk, 0)

    def row(q):
        return (*bj(q), 0)

    def lyr3(q):
        return (layer, 0, 0)

    return pl.pallas_call(
        functools.partial(_merge_kernel, tail=ctx_rows),
        grid=(b * n_blk,),
        in_specs=_lag_specs(b, tm, d, n_blk, layer, ctx_rows > 0, 0) + [
            _const_spec((None, 3, d), lyr3),
            _const_spec((None, 3, d), lyr3),
            pl.BlockSpec((None, tm, BRANCH_WIDTH), row),
            pl.BlockSpec((None, tm, BRANCH_WIDTH), row),
            pl.BlockSpec((None, tm, BRANCH_WIDTH), row),
            pl.BlockSpec((None, tm, BRANCH_WIDTH), row),
            _const_spec((None, d, N_BRANCH * d), lyr3),
            _const_spec((None, N_BRANCH, BRANCH_WIDTH, d), lambda q: (layer, 0, 0, 0)),
            _const_spec((None, d, d), lyr3),
        ],
        out_specs=pl.BlockSpec((None, tm, d), row),
        out_shape=jax.ShapeDtypeStruct((b, n_blk * tm, d), F32),
        compiler_params=_cparams(("parallel",)),
        name="merge",
    )(x3, mods, mods, norm_pre, norm_post, a3, b3, c3, d3, wg_bf, wb_bf, wo_bf)


def _rot_cols(w):
    q = w.shape[-1] // 4
    return jnp.concatenate([-w[..., q:2 * q], w[..., 0:q], -w[..., 3 * q:4 * q], w[..., 2 * q:3 * q]], axis=-1)


def _rope_full(rows_n, rot_dim):
    f32 = np.float32
    axis_dim = rot_dim // 2
    inv_freq = (f32(ROPE_THETA) ** (-np.arange(0, axis_dim, 2, dtype=f32) / f32(axis_dim))).astype(f32)
    row = np.repeat(np.arange(rows_n, dtype=f32), GRID_W)
    col = np.tile(np.arange(GRID_W, dtype=f32), rows_n)
    ang_r = row[:, None] * inv_freq[None, :]
    ang_c = col[:, None] * inv_freq[None, :]
    cos = np.concatenate([np.cos(ang_r), np.cos(ang_r), np.cos(ang_c), np.cos(ang_c)], axis=-1)
    sin = np.concatenate([np.sin(ang_r), np.sin(ang_r), np.sin(ang_c), np.sin(ang_c)], axis=-1)
    return cos.astype(f32), sin.astype(f32)


def _rope_table(l, tm):
    f32 = np.float32
    cg, sg = _rope_full(l // GRID_W, HEAD_DIM)
    cm, sm = _rope_full(l // GRID_W, MLA_ROPE)
    scale = f32((MLA_NOPE + MLA_ROPE) ** -0.5)
    pad = MLA_SLOT - MLA_NOPE - MLA_ROPE

    def build(cg, sg, cm, sm):
        n = cg.shape[0]
        ones = np.ones((n, MLA_NOPE), f32)
        zeros = np.zeros((n, MLA_NOPE), f32)
        return np.concatenate([
            cg, cg, sg, sg,
            scale * ones, scale * cm, zeros[:, :pad],
            zeros, scale * sm, zeros[:, :pad],
            cm, sm, zeros,
        ], axis=-1)

    lat = build(cg, sg, cm, sm)
    one_g, zero_g = np.ones((tm, HEAD_DIM), f32), np.zeros((tm, HEAD_DIM), f32)
    ctx = build(one_g, zero_g, one_g[:, :MLA_ROPE], zero_g[:, :MLA_ROPE])
    return jnp.asarray(np.concatenate([lat, ctx], axis=0))


GQA_HEAD_ORDER = (0, 2, 1, 3)


def _proj_cols(w):
    o = 0

    def take(n):
        nonlocal o
        v = w[..., o:o + n]
        o += n
        return v

    kvl, kpe = take(MLA_KV_LORA), take(MLA_ROPE)
    gk, gv = take(GQA_KV_HEADS * HEAD_DIM), take(GQA_KV_HEADS * HEAD_DIM)
    u, ql, gq, z = take(S5_CHANNELS), take(MLA_Q_LORA), take(GQA_Q_HEADS * HEAD_DIM), take(2 * GMLP_WIDTH)
    gate = w[..., o:]
    gqh = [gq[..., i * HEAD_DIM:(i + 1) * HEAD_DIM] for i in range(GQA_Q_HEADS)]
    gkh = [gk[..., i * HEAD_DIM:(i + 1) * HEAD_DIM] for i in range(GQA_KV_HEADS)]
    pad = jnp.zeros(w.shape[:-1] + (PC_U - PC_A - MLA_KV_LORA - 2 * MLA_ROPE,), w.dtype)
    wp = jnp.concatenate([z, kvl, kpe, _rot_cols(kpe), pad, u, ql]
                         + [gqh[i] for i in GQA_HEAD_ORDER] + [_rot_cols(gqh[i]) for i in GQA_HEAD_ORDER]
                         + [gk] + [_rot_cols(h) for h in gkh] + [gv], axis=-1)
    return wp, gate


def _wprep_kernel(wt_ref, wp_ref, wg_ref):
    n = wt_ref.shape[0]
    full = n // LANES * LANES
    pieces = [wt_ref[r0:r0 + LANES, :].T for r0 in range(0, full, LANES)]
    if full < n:
        rest = jnp.concatenate([wt_ref[full:, :], jnp.zeros((LANES - (n - full), wt_ref.shape[1]), F32)], axis=0)
        pieces.append(rest.T[:, :n - full])
    wp, wg = _proj_cols(jnp.concatenate(pieces, axis=1))
    wp_ref[...] = wp.astype(BF16)
    wg_ref[...] = wg.astype(BF16)


def _wprep_call(w_in):
    depth, d, n = w_in.shape
    tr = 256
    n_gate = N_BRANCH * d
    return pl.pallas_call(
        _wprep_kernel,
        grid=(depth, d // tr),
        in_specs=[pl.BlockSpec((None, n, tr), lambda l, i: (l, 0, i))],
        out_specs=[pl.BlockSpec((None, tr, PC_END), lambda l, i: (l, i, 0)),
                   pl.BlockSpec((None, tr, n_gate), lambda l, i: (l, i, 0))],
        out_shape=[jax.ShapeDtypeStruct((depth, d, PC_END), BF16), jax.ShapeDtypeStruct((depth, d, n_gate), BF16)],
        compiler_params=_cparams(("parallel", "parallel")),
        name="w_prep",
    )(jnp.swapaxes(w_in, 1, 2))


def _mla_kv_weight(w_ukv):
    depth = w_ukv.shape[0]
    wh = w_ukv.reshape(depth, MLA_KV_LORA, MLA_HEADS, MLA_NOPE + MLA_V)
    k_nope, v = wh[..., :MLA_NOPE], wh[..., MLA_NOPE:]
    kslot = jnp.concatenate([k_nope, jnp.zeros((depth, MLA_KV_LORA, MLA_HEADS, MLA_SLOT - MLA_NOPE), F32)], axis=-1)
    top = jnp.concatenate([kslot.reshape(depth, MLA_KV_LORA, -1), v.reshape(depth, MLA_KV_LORA, -1)], axis=-1)
    eye = jnp.eye(MLA_ROPE, dtype=F32)
    pe_slot = jnp.concatenate([jnp.zeros((MLA_ROPE, MLA_NOPE), F32), eye,
                               jnp.zeros((MLA_ROPE, MLA_SLOT - MLA_NOPE - MLA_ROPE), F32)], axis=-1)
    pe_rows = jnp.concatenate([jnp.tile(pe_slot, (1, MLA_HEADS)), jnp.zeros((MLA_ROPE, MLA_HEADS * MLA_V), F32)],
                              axis=-1)
    pe_rows = jnp.broadcast_to(pe_rows, (depth,) + pe_rows.shape)
    tail = jnp.zeros((depth, 256 - MLA_KV_LORA - 2 * MLA_ROPE, MLA_KV_COLS), F32)
    return jnp.concatenate([top, pe_rows, pe_rows, tail], axis=1).astype(BF16)


def _mla_q_weight(w_uq):
    depth = w_uq.shape[0]
    wh = w_uq.reshape(depth, MLA_Q_LORA, MLA_HEADS, MLA_NOPE + MLA_ROPE)
    nope, pe = wh[..., :MLA_NOPE], wh[..., MLA_NOPE:]
    pad = jnp.zeros((depth, MLA_Q_LORA, MLA_HEADS, MLA_SLOT - MLA_NOPE - MLA_ROPE), F32)
    full = jnp.concatenate([nope, pe, pad], axis=-1).reshape(depth, MLA_Q_LORA, -1)
    rot = jnp.concatenate([jnp.zeros_like(nope), _rot_cols(pe), pad], axis=-1).reshape(depth, MLA_Q_LORA, -1)
    return jnp.concatenate([full, rot], axis=-1).astype(BF16)


def _block_diag(w):
    g, a, b = w.shape[-3:]
    lead = w.shape[:-3]
    cols = jnp.swapaxes(w, -3, -2).reshape(lead + (a, g * b))
    tiled = jnp.tile(cols, (1,) * len(lead) + (g, 1))
    same = (jnp.arange(g * a)[:, None] // a) == (jnp.arange(g * b)[None, :] // b)
    return jnp.where(same, tiled, jnp.zeros((), w.dtype))


def kernel(x, c, ctx, c_ctx, w_ada, b_ada, norm_pre, norm_post, w_ffn_in, w_ffn_out, w_in, mla_q_norm, mla_w_uq,
           mla_kv_norm, mla_w_ukv, gqa_sink, s5_lam_re, s5_lam_im, s5_log_dt, s5_b_re, s5_b_im, s5_c_re, s5_c_im,
           s5_d, s5_w_glu, s5_b_glu, gmlp_norm, gmlp_w_s, gmlp_b_s, w_branch, w_out):
    b, l, d = x.shape
    cl = ctx.shape[1]
    r = l + cl
    depth = w_ada.shape[0]
    assert b == SUBLANES and b < MOD_ROWS
    assert l % GRID_W == 0 and l >= 3 * W_BLOCK and l % cl == 0
    assert l % GMLP_CHUNK == 0 and WIDE_TILE % GMLP_CHUNK == 0 and l % MLA_Q_TILE == 0 and l % GQA_Q_TILE == 0
    assert l % S5_CHUNK == 0 and cl % S5_CHUNK == 0 and l % LAT_TILE == 0
    assert r % WIDE_TILE == 0 and r % XWIDE_TILE == 0 and XWIDE_TILE % GMLP_CHUNK == 0 and cl <= WIDE_TILE

    wi_bf, wo_bf_ffn = w_ffn_in[0, 0].astype(BF16), w_ffn_out[0, 0].astype(BF16)
    wp_bf, wg_bf = _wprep_call(w_in)
    wkv_bf = _mla_kv_weight(mla_w_ukv)
    wq_bf = _mla_q_weight(mla_w_uq)
    head_order = jnp.array(GQA_HEAD_ORDER)
    wb1 = w_branch[:, 1].reshape(depth, GQA_Q_HEADS, HEAD_DIM, d)[:, head_order].reshape(depth, BRANCH_WIDTH, d)
    wb_bf = jnp.concatenate([w_branch[:, :1], wb1[:, None], w_branch[:, 2:]], axis=1).astype(BF16)
    wo_bf = w_out.astype(BF16)
    sink_rows = jnp.broadcast_to(
        jnp.concatenate([gqa_sink[:, head_order], jnp.zeros((depth, SUBLANES - GQA_Q_HEADS), F32)], axis=1)[:, :, None],
        (depth, SUBLANES, LANES))
    ws_bf = jnp.transpose(gmlp_w_s, (0, 2, 1, 3)).reshape(depth, GMLP_CHUNK, GMLP_GROUPS * GMLP_CHUNK).astype(BF16)
    bs_f = jnp.repeat(jnp.transpose(gmlp_b_s, (0, 2, 1)), GMLP_WIDTH // GMLP_GROUPS, axis=2)
    tab = _rope_table(l, cl)
    mla_kv_norm3 = mla_kv_norm[:, None, :]
    mla_q_norm3 = mla_q_norm[:, None, :]
    gmlp_norm3 = gmlp_norm[:, None, :]
    s5_d3 = s5_d[:, None, :]
    s5_b_glu3 = s5_b_glu[:, None, :]
    wglu_bf = s5_w_glu.astype(BF16)
    tb = jnp.arange(S5_CHUNK * b)
    pm = (tb[None, :] == ((tb % b) * S5_CHUNK + tb // b)[:, None]).astype(BF16)
    pmt = pm.T

    hg, p, g = S5_GROUP, S5_STATE, S5_GROUPS
    n_par = depth * 2 * g
    rep = lambda t: jnp.repeat(t.reshape(n_par, p), hg, axis=1)
    ldt = jnp.broadcast_to(s5_log_dt[..., None], (depth, 2, g, p))
    a_re_x, a_im_x, bb_re, bb_im = _s5_disc_call(
        rep(s5_lam_re), rep(s5_lam_im), rep(ldt), s5_b_re.reshape(n_par, p * hg), s5_b_im.reshape(n_par, p * hg))
    a_ri = jnp.broadcast_to(
        jnp.stack([a_re_x[:, ::hg].reshape(depth, 2, g * p), a_im_x[:, ::hg].reshape(depth, 2, g * p)], axis=2)
        [:, :, :, None, :], (depth, 2, 2, b, g * p))
    bb_re = jnp.swapaxes(bb_re.reshape(depth, 2, g, p, hg), -1, -2)
    bb_im = jnp.swapaxes(bb_im.reshape(depth, 2, g, p, hg), -1, -2)
    s5_b_cat = jnp.concatenate([_block_diag(bb_re), _block_diag(bb_im)], axis=-1).astype(BF16)
    s5_c_cat = jnp.concatenate([_block_diag(jnp.swapaxes(s5_c_re, -1, -2)),
                                _block_diag(jnp.swapaxes(-s5_c_im, -1, -2))], axis=-2).astype(BF16)

    cs = jnp.concatenate([c, c_ctx[None, :], jnp.zeros((MOD_ROWS - b - 1, d), F32)], axis=0)
    mods = _ada_call(cs, w_ada, b_ada[:, None, :]).reshape(depth, MOD_ROWS, N_MOD, d)

    xs = jnp.concatenate([x, ctx], axis=1)
    n_wide = r // WIDE_TILE
    for layer in range(depth):
        last = layer == depth - 1
        xs, wi_bf, wo_bf_ffn = _ffn_call(xs, mods, norm_pre, norm_post, wi_bf, wo_bf_ffn,
                                         (w_ffn_in, w_ffn_out, layer, 1),
                                         layer=layer, s=0, tm=WIDE_TILE, n_blk=n_wide, ctx_rows=cl)
        mq, mkv, gq, gkv, u3, d3 = _proj_call(
            xs, mods, norm_pre, wp_bf, tab, mla_kv_norm3, wkv_bf, mla_q_norm3, wq_bf, gmlp_norm3, ws_bf, bs_f,
            layer=layer, lat_rows=l)
        a_dst, b_dst = (None, None) if last else _ctx_attn_call(mq, mkv, gq, gkv, sink_rows, layer=layer, lat_rows=l)
        a3 = _mla_lat_call(mq, mkv, a_dst, lat_rows=l)
        b3 = _gqa_lat_call(gq, gkv, sink_rows, b_dst, layer=layer, lat_rows=l)
        u4 = u3.reshape(b, r // S5_CHUNK, S5_CHUNK, S5_CHANNELS)
        rb = _s5_call(u4, None, pm, None, a_ri, s5_b_cat, s5_c_cat, None, None, None,
                      layer=layer, reverse=True, lat_rows=l)
        c3 = _s5_call(u4, rb, pm, pmt, a_ri, s5_b_cat, s5_c_cat, s5_d3, wglu_bf, s5_b_glu3,
                      layer=layer, reverse=False, lat_rows=l).reshape(b, r, S5_CHANNELS)
        if last:
            tm, n_blk, ctx_rows = LAT_TILE, l // LAT_TILE, 0
            tm_merge, n_merge = tm, n_blk
        else:
            tm, n_blk, ctx_rows = WIDE_TILE, n_wide, cl
            tm_merge, n_merge = XWIDE_TILE, r // XWIDE_TILE
        xs = _merge_call(xs, mods, norm_pre, norm_post, a3, b3, c3, d3, wg_bf, wb_bf, wo_bf,
                         layer=layer, tm=tm_merge, n_blk=n_merge, ctx_rows=ctx_rows)
        outs = _ffn_call(xs, mods, norm_pre, norm_post, wi_bf, wo_bf_ffn,
                         None if last else (w_ffn_in, w_ffn_out, layer + 1, 0),
                         layer=layer, s=2, tm=tm, n_blk=n_blk, ctx_rows=ctx_rows)
        xs = outs[0]
        if not last:
            wi_bf, wo_bf_ffn = outs[1], outs[2]
    return xs
```

```python
import functools

import jax
import jax.numpy as jnp
import numpy as np
from jax import lax
from jax.experimental import pallas as pl
from jax.experimental.pallas import tpu as pltpu

F32 = jnp.float32
BF16 = jnp.bfloat16

GRID_W = 64
HEAD_DIM = 64
ROPE_THETA = 10000.0
RMS_EPS = 1e-6
LN_EPS = 1e-5
NEG_INF = -1e30
MLA_HEADS = 4
MLA_Q_LORA = 256
MLA_KV_LORA = 128
MLA_NOPE = 64
MLA_ROPE = 32
MLA_V = 64
GQA_Q_HEADS = 4
GQA_KV_HEADS = 2
WINDOW = 128
W_BLOCK = 128
S5_CHANNELS = 256
S5_GROUP = 16
S5_GROUPS = S5_CHANNELS // S5_GROUP
S5_STATE = 64
GMLP_WIDTH = 256
GMLP_CHUNK = 128
GMLP_GROUPS = 4
N_BRANCH = 4
BRANCH_WIDTH = 256
N_MOD = 9

LANES = 128
SUBLANES = 8
MXU_TILE = 256
VMEM_LIMIT_BYTES = 56 * 1024 * 1024

WIDE_TILE = 768
XWIDE_TILE = 1152
LAT_TILE = 512
MLA_Q_TILE = 512
GQA_Q_TILE = 512
S5_CHUNK = 64
MOD_ROWS = 16
MLA_SLOT = LANES
S5_LANES = S5_GROUPS * S5_STATE

PC_Z = 0
PC_A = PC_Z + 2 * GMLP_WIDTH
PC_U = PC_A + 256
PC_QL = PC_U + S5_CHANNELS
PC_GQ = PC_QL + MLA_Q_LORA
PC_GQR = PC_GQ + GQA_Q_HEADS * HEAD_DIM
PC_GK = PC_GQR + GQA_Q_HEADS * HEAD_DIM
PC_GKR = PC_GK + GQA_KV_HEADS * HEAD_DIM
PC_GV = PC_GKR + GQA_KV_HEADS * HEAD_DIM
PC_END = PC_GV + GQA_KV_HEADS * HEAD_DIM

TC_GC, TC_GS, TC_MQC, TC_MQS, TC_MKT, TC_END = 0, 128, 256, 384, 512, 640

MLA_K_COLS = MLA_HEADS * MLA_SLOT
MLA_KV_COLS = MLA_K_COLS + MLA_HEADS * MLA_V


def _cparams(sem):
    return pltpu.CompilerParams(dimension_semantics=sem, vmem_limit_bytes=VMEM_LIMIT_BYTES)


def _const_spec(shape, index_map):
    return pl.BlockSpec(shape, index_map, pipeline_mode=pl.Buffered(1))


def _rms(x, gain):
    return x * lax.rsqrt(jnp.mean(x * x, axis=-1, keepdims=True) + RMS_EPS) * gain


def _dot(a, b):
    return jnp.dot(a, b, preferred_element_type=F32)


def _dot_nt(a, b):
    return lax.dot_general(a, b, (((1,), (1,)), ((), ())), preferred_element_type=F32)


def _ada_kernel(cs_ref, w_ref, b_ref, o_ref):
    cs = cs_ref[...]
    s = (cs * jax.nn.sigmoid(cs)).astype(BF16)
    o_ref[...] = _dot(s, w_ref[...].astype(BF16)) + b_ref[...]


def _ada_call(cs, w_ada, b_ada3):
    depth, d, n = w_ada.shape
    tn = d
    return pl.pallas_call(
        _ada_kernel,
        grid=(depth, n // tn),
        in_specs=[
            pl.BlockSpec((MOD_ROWS, d), lambda l, j: (0, 0)),
            pl.BlockSpec((None, d, tn), lambda l, j: (l, 0, j)),
            pl.BlockSpec((None, 1, tn), lambda l, j: (l, 0, j)),
        ],
        out_specs=pl.BlockSpec((None, MOD_ROWS, tn), lambda l, j: (l, 0, j)),
        out_shape=jax.ShapeDtypeStruct((depth, MOD_ROWS, n), F32),
        compiler_params=_cparams(("parallel", "parallel")),
        name="ada_mod",
    )(cs, w_ada, b_ada3)


def _s5_disc_kernel(lre_ref, lim_ref, ldt_ref, bre_ref, bim_ref, are_ref, aim_ref, bbre_ref, bbim_ref):
    lam_re = jnp.minimum(lre_ref[...], -1e-4)
    lam_im = lim_ref[...]
    dt = jnp.exp(ldt_ref[...])
    mag = jnp.exp(lam_re * dt)
    a_re = mag * jnp.cos(lam_im * dt)
    a_im = mag * jnp.sin(lam_im * dt)
    nr, ni = a_re - 1.0, a_im
    den = lam_re * lam_re + lam_im * lam_im
    coef_re = (nr * lam_re + ni * lam_im) / den
    coef_im = (ni * lam_re - nr * lam_im) / den
    b_re, b_im = bre_ref[...], bim_ref[...]
    are_ref[...] = a_re
    aim_ref[...] = a_im
    bbre_ref[...] = coef_re * b_re - coef_im * b_im
    bbim_ref[...] = coef_re * b_im + coef_im * b_re


def _s5_disc_call(lre, lim, ldt, bre, bim):
    shp = jax.ShapeDtypeStruct(lre.shape, F32)
    return pl.pallas_call(_s5_disc_kernel, out_shape=(shp, shp, shp, shp), name="s5_disc")(lre, lim, ldt, bre, bim)


def _mod_pieces(tm, tail):
    return ((0, tm - tail, 0), (tm - tail, tm, 1)) if tail else ((0, tm, 0),)


def _lag_block(b, n_blk, lag):
    def bj(q):
        q = jnp.clip(q - lag, 0, b * n_blk - 1)
        return q // n_blk, q % n_blk
    return bj


def _lag_specs(b, tm, d, n_blk, layer, ctx_tail, lag):
    bj = _lag_block(b, n_blk, lag)

    def tail_row(q):
        bi, j = bj(q)
        return jnp.where(j == n_blk - 1, b, bi) if ctx_tail else bi

    return [
        pl.BlockSpec((None, tm, d), lambda q: (*bj(q), 0)),
        pl.BlockSpec((None, None, N_MOD, d), lambda q: (layer, bj(q)[0], 0, 0)),
        pl.BlockSpec((None, None, N_MOD, d), lambda q: (layer, tail_row(q), 0, 0)),
    ]


def _ffn_kernel(xn_ref, modn_ref, modnt_ref, xo_ref, modo_ref, modot_ref, gpre_ref, gpost_ref, wi_ref, wo_ref, *rest,
                s, d_ff, chunks, tail, n_steps, cast_next):
    if cast_next:
        wi_next_ref, wo_next_ref, o_ref, wi_cast_ref, wo_cast_ref, h_ref, y_ref = rest
    else:
        o_ref, h_ref, y_ref = rest
    q = pl.program_id(0)
    pieces = _mod_pieces(xn_ref.shape[0], tail)

    def cast_slabs():
        if cast_next:
            wi_cast_ref[...] = wi_next_ref[...].astype(BF16)
            wo_cast_ref[...] = wo_next_ref[...].astype(BF16)

    def pre_norm():
        ms = (modn_ref[...], modnt_ref[...])
        return jnp.concatenate(
            [(_rms(xn_ref[r0:r1], gpre_ref[s:s + 1]) * (1.0 + ms[k][3 * s + 1:3 * s + 2]) + ms[k][3 * s:3 * s + 1])
             .astype(BF16) for r0, r1, k in pieces], axis=0)

    def post_norm():
        ms = (modo_ref[...], modot_ref[...])
        return jnp.concatenate(
            [xo_ref[r0:r1] + 0.5 * ms[k][3 * s + 2:3 * s + 3] * _rms(y_ref[r0:r1], gpost_ref[s:s + 1])
             for r0, r1, k in pieces], axis=0)

    def up(h, c0, c1):
        a = _dot(h, wi_ref[:, c0:c1])
        g = _dot(h, wi_ref[:, d_ff + c0:d_ff + c1])
        return (a * jax.nn.sigmoid(a) * g).astype(BF16)

    @pl.when(q == 0)
    def _():
        h_ref[...] = pre_norm()
        y_ref[...] = jnp.zeros_like(y_ref)
        cast_slabs()

    @pl.when(jnp.logical_and(q > 0, q < n_steps - 1))
    def _():
        always = q < n_steps
        h = h_ref[...]
        acts = [up(h, *chunks[0])]
        cast_slabs()
        out = post_norm()
        o_ref[...] = out
        h = jnp.where(always, h, out.astype(BF16))
        acts += [up(h, c0, c1) for c0, c1 in chunks[1:]]
        h_new = pre_norm()
        h_ref[...] = h_new
        w0 = chunks[0][1] - chunks[0][0]
        acts[0] = jnp.concatenate([jnp.where(always, acts[0][:, :h_new.shape[1]], h_new), acts[0][:, h_new.shape[1]:]],
                                  axis=1) if w0 > h_new.shape[1] else jnp.where(always, acts[0], h_new[:, :w0])
        y = None
        for act, (c0, c1) in zip(acts, chunks):
            part = _dot(act, wo_ref[c0:c1, :])
            y = part if y is None else y + part
        y_ref[...] = y

    @pl.when(q == n_steps - 1)
    def _():
        o_ref[...] = post_norm()
        cast_slabs()


FFN_CAST_SLABS = 16


def _ffn_call(x3, mods, norm_pre, norm_post, w_in_bf, w_out_bf, w_next, *, layer, s, tm, n_blk, ctx_rows):
    b, _, d = x3.shape
    d_ff = w_out_bf.shape[0]
    half = (d_ff // MXU_TILE + 1) // 2 * MXU_TILE
    chunks = ((0, half), (half, d_ff)) if 0 < half < d_ff else ((0, d_ff),)
    n = b * n_blk
    ns = FFN_CAST_SLABS
    while ns > n + 2:
        ns //= 2
    assert d % (ns * 2 * SUBLANES) == 0 and d_ff % (ns * 2 * SUBLANES) == 0
    kern = functools.partial(_ffn_kernel, s=s, d_ff=d_ff, chunks=chunks, tail=ctx_rows, n_steps=n + 2,
                             cast_next=w_next is not None)

    def specs(lag):
        return _lag_specs(b, tm, d, n_blk, layer, ctx_rows > 0, lag)

    in_specs = specs(0) + specs(2) + [
        _const_spec((None, 3, d), lambda q: (layer, 0, 0)),
        _const_spec((None, 3, d), lambda q: (layer, 0, 0)),
        _const_spec((d, 2 * d_ff), lambda q: (0, 0)),
        _const_spec((d_ff, d), lambda q: (0, 0)),
    ]
    args = [x3, mods, mods, x3, mods, mods, norm_pre, norm_post, w_in_bf, w_out_bf]
    out_specs = [specs(2)[0]]
    out_shape = [jax.ShapeDtypeStruct((b, n_blk * tm, d), F32)]
    if w_next is not None:
        wi_f32, wo_f32, l_next, which_next = w_next

        def slab(q):
            return jnp.minimum(q, ns - 1)

        in_specs += [
            pl.BlockSpec((None, None, d // ns, 2 * d_ff), lambda q: (l_next, which_next, slab(q), 0)),
            pl.BlockSpec((None, None, d_ff // ns, d), lambda q: (l_next, which_next, slab(q), 0)),
        ]
        args += [wi_f32, wo_f32]
        out_specs += [pl.BlockSpec((d // ns, 2 * d_ff), lambda q: (slab(q), 0)),
                      pl.BlockSpec((d_ff // ns, d), lambda q: (slab(q), 0))]
        out_shape += [jax.ShapeDtypeStruct((d, 2 * d_ff), BF16), jax.ShapeDtypeStruct((d_ff, d), BF16)]
    return pl.pallas_call(
        kern,
        grid=(n + 2,),
        in_specs=in_specs,
        out_specs=out_specs,
        out_shape=out_shape,
        scratch_shapes=[pltpu.VMEM((tm, d), BF16), pltpu.VMEM((tm, d), F32)],
        compiler_params=_cparams(("arbitrary",)),
        name="ffn",
    )(*args)


def _proj_kernel(x_ref, mod_ref, modt_ref, gpre_ref, wp_ref, tab_ref, gkvn_ref, wkv_ref, gqn_ref, wq_ref,
                 gmn_ref, ws_ref, bs_ref,
                 mq_ref, mkv_ref, gq_ref, gkv_ref, u_ref, d_ref, *, tail):
    ms = (mod_ref[...], modt_ref[...])
    h = jnp.concatenate(
        [(_rms(x_ref[r0:r1], gpre_ref[1:2]) * (1.0 + ms[k][4:5]) + ms[k][3:4]).astype(BF16)
         for r0, r1, k in _mod_pieces(x_ref.shape[0], tail)], axis=0)
    p = _dot(h, wp_ref[...])
    tab = tab_ref[...]
    gc, gs = tab[:, TC_GC:TC_GS], tab[:, TC_GS:TC_MQC]
    mqc, mqs, mkt = tab[:, TC_MQC:TC_MQS], tab[:, TC_MQS:TC_MKT], tab[:, TC_MKT:TC_END]

    kvl = p[:, PC_A:PC_A + MLA_KV_LORA]
    kvn = _rms(kvl, gkvn_ref[...])
    pe = p[:, PC_A + MLA_KV_LORA:PC_U] * mkt
    a2 = jnp.concatenate([kvn, pe], axis=-1).astype(BF16)
    mkv_ref[...] = _dot(a2, wkv_ref[...]).astype(BF16)

    qn = _rms(p[:, PC_QL:PC_GQ], gqn_ref[...]).astype(BF16)
    q = _dot(qn, wq_ref[...])
    qs = [q[:, hh * MLA_SLOT:(hh + 1) * MLA_SLOT] * mqc
          + q[:, MLA_K_COLS + hh * MLA_SLOT:MLA_K_COLS + (hh + 1) * MLA_SLOT] * mqs
          for hh in range(MLA_HEADS)]
    mq_ref[...] = jnp.concatenate(qs, axis=-1).astype(BF16)

    gq = [(p[:, PC_GQ + j * LANES:PC_GQ + (j + 1) * LANES] * gc
           + p[:, PC_GQR + j * LANES:PC_GQR + (j + 1) * LANES] * gs) * (HEAD_DIM ** -0.5)
          for j in range(GQA_Q_HEADS * HEAD_DIM // LANES)]
    gq_ref[...] = jnp.concatenate(gq, axis=-1).astype(BF16)
    gk = p[:, PC_GK:PC_GKR] * gc + p[:, PC_GKR:PC_GV] * gs
    gkv_ref[...] = jnp.concatenate([gk, p[:, PC_GV:PC_END]], axis=-1).astype(BF16)

    u_ref[...] = p[:, PC_U:PC_QL].astype(BF16)

    zz = jax.nn.gelu(p[:, PC_Z:PC_A])
    ug, v = zz[:, :GMLP_WIDTH], zz[:, GMLP_WIDTH:]
    vc = v - jnp.mean(v, axis=-1, keepdims=True)
    vn = vc * lax.rsqrt(jnp.mean(vc * vc, axis=-1, keepdims=True) + LN_EPS) * gmn_ref[...]
    group_of_lane = lax.broadcasted_iota(jnp.int32, (GMLP_CHUNK, GMLP_WIDTH), 1) // (GMLP_WIDTH // GMLP_GROUPS)
    outs = []
    for ci in range(x_ref.shape[0] // GMLP_CHUNK):
        r0 = ci * GMLP_CHUNK
        vck = vn[r0:r0 + GMLP_CHUNK]
        vbd = jnp.concatenate([jnp.where(group_of_lane == g, vck, 0.0) for g in range(GMLP_GROUPS)], axis=0)
        mixed = _dot(ws_ref[...], vbd.astype(BF16)) + bs_ref[...]
        outs.append(ug[r0:r0 + GMLP_CHUNK] * mixed)
    d_ref[...] = jnp.concatenate(outs, axis=0).astype(BF16)


def _proj_call(x3, mods, norm_pre, wp_bf, tab, mla_kv_norm3, wkv_bf, mla_q_norm3, wq_bf, gmlp_norm3, ws_bf, bs_f,
               *, layer, lat_rows):
    b, r, d = x3.shape
    tm = XWIDE_TILE
    n_blk = r // tm
    bj = _lag_block(b, n_blk, 0)

    def row(q):
        return (*bj(q), 0)

    def lyr3(q):
        return (layer, 0, 0)

    widths = (MLA_K_COLS, MLA_KV_COLS, GQA_Q_HEADS * HEAD_DIM, 2 * GQA_KV_HEADS * HEAD_DIM, S5_CHANNELS, GMLP_WIDTH)
    return pl.pallas_call(
        functools.partial(_proj_kernel, tail=r - lat_rows),
        grid=(b * n_blk,),
        in_specs=_lag_specs(b, tm, d, n_blk, layer, True, 0) + [
            _const_spec((None, 3, d), lyr3),
            _const_spec((d, PC_END), lambda q: (0, 0)),
            pl.BlockSpec((tm, TC_END), lambda q: (bj(q)[1], 0)),
            _const_spec((None, 1, MLA_KV_LORA), lyr3),
            _const_spec((None, 256, MLA_KV_COLS), lyr3),
            _const_spec((None, 1, MLA_Q_LORA), lyr3),
            _const_spec((None, MLA_Q_LORA, 2 * MLA_K_COLS), lyr3),
            _const_spec((None, 1, GMLP_WIDTH), lyr3),
            _const_spec((None, GMLP_CHUNK, GMLP_GROUPS * GMLP_CHUNK), lyr3),
            _const_spec((None, GMLP_CHUNK, GMLP_WIDTH), lyr3),
        ],
        out_specs=[pl.BlockSpec((None, tm, w), row) for w in widths],
        out_shape=[jax.ShapeDtypeStruct((b, r, w), BF16) for w in widths],
        compiler_params=_cparams(("parallel",)),
        name="in_proj",
    )(x3, mods, mods, norm_pre, wp_bf, tab, mla_kv_norm3, wkv_bf, mla_q_norm3, wq_bf, gmlp_norm3, ws_bf, bs_f)


def _mla_attend(q_ref, kv_ref):
    tq = q_ref.shape[0]
    head_of_lane = lax.broadcasted_iota(jnp.int32, (tq, MLA_HEADS * MLA_V), 1) // MLA_V
    acc = jnp.zeros((tq, MLA_HEADS * MLA_V), F32)
    v = kv_ref[:, MLA_K_COLS:]
    sls = [slice(hh * MLA_SLOT, (hh + 1) * MLA_SLOT) for hh in range(MLA_HEADS)]
    scs = [_dot_nt(q_ref[:, sl], kv_ref[:, sl]) for sl in sls]
    ps = [jnp.exp(sc - jnp.max(sc, axis=-1, keepdims=True)) for sc in scs]
    dens = [jnp.sum(p, axis=-1, keepdims=True) for p in ps]
    o_all = _dot(jnp.concatenate([p.astype(BF16) for p in ps], axis=0), v)
    for hh in range(MLA_HEADS):
        acc = jnp.where(head_of_lane == hh, o_all[hh * tq:(hh + 1) * tq] / dens[hh], acc)
    return acc.astype(BF16)


def _mla_lat_kernel(q_ref, kv_ref, *rest):
    rest[-1][...] = _mla_attend(q_ref, kv_ref)


def _mla_lat_call(mq, mkv, dst, *, lat_rows):
    b, r, _ = mq.shape
    tq = MLA_Q_TILE
    w = MLA_HEADS * MLA_V
    in_specs = [
        pl.BlockSpec((None, tq, MLA_K_COLS), lambda bi, j: (bi, j, 0)),
        pl.BlockSpec((None, r, MLA_KV_COLS), lambda bi, j: (bi, 0, 0)),
    ]
    args = [mq, mkv]
    if dst is not None:
        in_specs.append(pl.BlockSpec(memory_space=pl.ANY))
        args.append(dst)
    return pl.pallas_call(
        _mla_lat_kernel,
        grid=(b, lat_rows // tq),
        in_specs=in_specs,
        out_specs=pl.BlockSpec((None, tq, w), lambda bi, j: (bi, j, 0)),
        out_shape=jax.ShapeDtypeStruct((b, lat_rows if dst is None else r, w), BF16),
        input_output_aliases={} if dst is None else {2: 0},
        compiler_params=_cparams(("parallel", "parallel")),
        name="mla_latent",
    )(*args)


def _gqa_blocks(qs, kc, vc, sink_ref, bands):
    tq = qs[0].shape[0]
    kw = GQA_KV_HEADS * HEAD_DIM
    lo = lax.broadcasted_iota(jnp.int32, (tq, kw), 1) < HEAD_DIM
    sk = jnp.concatenate([jnp.broadcast_to(sink_ref[r:r + 1, 0:1], (tq, 1)) for r in range(GQA_Q_HEADS)], axis=0)
    qsts = []
    for q in qs:
        q0, q1 = q[:, :kw], q[:, kw:]
        zero = jnp.zeros_like(q0)
        qsts.append(jnp.concatenate([jnp.where(lo, q0, zero), jnp.where(lo, zero, q0),
                                     jnp.where(lo, q1, zero), jnp.where(lo, zero, q1)], axis=0))
    scs = [_dot_nt(qst, kc) for qst in qsts]
    mxs = [jnp.maximum(jnp.max(sc, axis=-1, keepdims=True), sk) for sc in scs]
    if bands is not None:
        sbs = [jnp.where(jnp.concatenate([valid] * GQA_Q_HEADS, axis=0), _dot_nt(qst, kb), NEG_INF)
               for qst, (kb, _, valid) in zip(qsts, bands)]
        mxs = [jnp.maximum(mx, jnp.max(sb, axis=-1, keepdims=True)) for mx, sb in zip(mxs, sbs)]
    pcs = [jnp.exp(sc - mx) for sc, mx in zip(scs, mxs)]
    dens = [jnp.sum(pc, axis=-1, keepdims=True) + jnp.exp(sk - mx) for pc, mx in zip(pcs, mxs)]
    os_ = [_dot(pc.astype(BF16), vc) for pc in pcs]
    if bands is not None:
        pbs = [jnp.exp(sb - mx) for sb, mx in zip(sbs, mxs)]
        dens = [den + jnp.sum(pb, axis=-1, keepdims=True) for den, pb in zip(dens, pbs)]
        os_ = [o + _dot(pb.astype(BF16), vb) for o, pb, (_, vb, _) in zip(os_, pbs, bands)]
    outs = []
    for o, den in zip(os_, dens):
        o = o / den
        c0 = jnp.where(lo, o[0:tq], o[tq:2 * tq])
        c1 = jnp.where(lo, o[2 * tq:3 * tq], o[3 * tq:4 * tq])
        outs.append(jnp.concatenate([c0, c1], axis=-1))
    return outs


def _gqa_lat_kernel(q_ref, kv_ref, sink_ref, *rest, lat_rows):
    o_ref = rest[-1]
    kw = GQA_KV_HEADS * HEAD_DIM
    kc, vc = kv_ref[lat_rows:, :kw], kv_ref[lat_rows:, kw:]
    nbk = 3 * W_BLOCK
    n_in = q_ref.shape[0] // W_BLOCK
    qs, bands = [], []
    for i in range(n_in):
        n = pl.program_id(1) * n_in + i
        start = pl.multiple_of(jnp.clip((n - 1) * W_BLOCK, 0, lat_rows - nbk), W_BLOCK)
        kb = kv_ref[pl.ds(start, nbk), :kw]
        vb = kv_ref[pl.ds(start, nbk), kw:]
        qpos = n * W_BLOCK + lax.broadcasted_iota(jnp.int32, (W_BLOCK, nbk), 0)
        kpos = start + lax.broadcasted_iota(jnp.int32, (W_BLOCK, nbk), 1)
        bands.append((kb, vb, jnp.abs(qpos - kpos) <= WINDOW))
        qs.append(q_ref[i * W_BLOCK:(i + 1) * W_BLOCK])
    outs = _gqa_blocks(qs, kc, vc, sink_ref, bands)
    o_ref[...] = jnp.concatenate(outs, axis=0).astype(BF16)


def _ctx_attn_kernel(mq_ref, mkv_ref, gq_ref, gkv_ref, sink_ref, a_ref, b_ref, *, lat_rows):
    kw = GQA_KV_HEADS * HEAD_DIM
    tq = gq_ref.shape[0] // 2
    a_ctx = _mla_attend(mq_ref, mkv_ref)
    b_ctx = _gqa_blocks([gq_ref[:tq], gq_ref[tq:]], gkv_ref[:, :kw], gkv_ref[:, kw:], sink_ref, None)
    for o_ref, val in ((a_ref, a_ctx), (b_ref, jnp.concatenate(b_ctx, axis=0).astype(BF16))):
        o_ref[:lat_rows] = jnp.zeros((lat_rows, o_ref.shape[1]), o_ref.dtype)
        o_ref[lat_rows:] = val


def _ctx_attn_call(mq, mkv, gq, gkv, sink_rows, *, layer, lat_rows):
    b, r, w = gq.shape
    c = r - lat_rows
    blk = lat_rows // c
    wa = MLA_HEADS * MLA_V

    def ctx(bi):
        return (bi, blk, 0)

    return pl.pallas_call(
        functools.partial(_ctx_attn_kernel, lat_rows=lat_rows),
        grid=(b,),
        in_specs=[
            pl.BlockSpec((None, c, MLA_K_COLS), ctx),
            pl.BlockSpec((None, c, MLA_KV_COLS), ctx),
            pl.BlockSpec((None, c, w), ctx),
            pl.BlockSpec((None, c, w), ctx),
            pl.BlockSpec((None, SUBLANES, LANES), lambda bi: (layer, 0, 0)),
        ],
        out_specs=[pl.BlockSpec((None, r, wa), lambda bi: (bi, 0, 0)), pl.BlockSpec((None, r, w), lambda bi: (bi, 0, 0))],
        out_shape=[jax.ShapeDtypeStruct((b, r, wa), BF16), jax.ShapeDtypeStruct((b, r, w), BF16)],
        compiler_params=_cparams(("parallel",)),
        name="ctx_attention",
    )(mq, mkv, gq, gkv, sink_rows)


def _gqa_lat_call(gq, gkv, sink_rows, dst, *, layer, lat_rows):
    b, r, w = gq.shape
    tq = GQA_Q_TILE
    in_specs = [
        pl.BlockSpec((None, tq, w), lambda bi, j: (bi, j, 0)),
        pl.BlockSpec((None, r, w), lambda bi, j: (bi, 0, 0)),
        pl.BlockSpec((None, SUBLANES, LANES), lambda bi, j: (layer, 0, 0)),
    ]
    args = [gq, gkv, sink_rows]
    if dst is not None:
        in_specs.append(pl.BlockSpec(memory_space=pl.ANY))
        args.append(dst)
    return pl.pallas_call(
        functools.partial(_gqa_lat_kernel, lat_rows=lat_rows),
        grid=(b, lat_rows // tq),
        in_specs=in_specs,
        out_specs=pl.BlockSpec((None, tq, w), lambda bi, j: (bi, j, 0)),
        out_shape=jax.ShapeDtypeStruct((b, lat_rows if dst is None else r, w), BF16),
        input_output_aliases={} if dst is None else {3: 0},
        compiler_params=_cparams(("parallel", "parallel")),
        name="gqa_latent",
    )(*args)


def _s5_kernel(*refs, reverse, chunk, batch):
    if reverse:
        (u_ref, pm_ref, a_ref, bcat_ref, ccat_ref, o_ref, buf_a, buf_b, u_a, u_b, st) = refs
        rb_ref = pmt_ref = dsk_ref = wglu_ref = bglu_ref = None
    else:
        (u_ref, rb_ref, pm_ref, pmt_ref, a_ref, bcat_ref, ccat_ref, dsk_ref, wglu_ref, bglu_ref,
         o_ref, buf_a, buf_b, u_a, u_b, st) = refs
    rws = chunk * batch
    n_tiles = 2 * S5_LANES // MXU_TILE
    per_tile = chunk // n_tiles

    @pl.when(pl.program_id(0) == 0)
    def _():
        buf_a[...] = jnp.zeros_like(buf_a)
        buf_b[...] = jnp.zeros_like(buf_b)
        u_a[...] = jnp.zeros_like(u_a)
        u_b[...] = jnp.zeros_like(u_b)
        st[...] = jnp.zeros_like(st)

    a_r, a_i = a_ref[0], a_ref[1]

    def stage(pos_in, pos_out, cur, oth, u_cur):
        u_old = u_cur[...]
        ub = _dot(pm_ref[...], u_ref[:, pos_in].reshape(rws, S5_CHANNELS)).astype(BF16)
        u_cur[...] = ub
        sr, si = st[:, :S5_LANES], st[:, S5_LANES:]
        racc = None
        for j in range(n_tiles):
            for k in range(per_tile):
                idx = j * per_tile + k
                off = ((chunk - 1 - idx) if reverse else idx) * batch
                nr = a_r * sr - a_i * si + oth[off:off + batch, :S5_LANES]
                ni = a_r * si + a_i * sr + oth[off:off + batch, S5_LANES:]
                oth[off:off + batch, :S5_LANES] = nr
                oth[off:off + batch, S5_LANES:] = ni
                sr, si = nr, ni
            cols = slice(j * MXU_TILE, (j + 1) * MXU_TILE)
            part = _dot(cur[:, cols].astype(BF16), ccat_ref[cols, :])
            racc = part if racc is None else racc + part
            cur[:, cols] = _dot(ub, bcat_ref[:, cols])
        st[:, :S5_LANES] = sr
        st[:, S5_LANES:] = si
        if reverse:
            o_ref[pos_out] = racc
        else:
            y = jax.nn.gelu(racc + rb_ref[pos_out] + dsk_ref[...] * u_old.astype(F32))
            z = _dot(y.astype(BF16), wglu_ref[...]) + bglu_ref[...]
            o_tb = (z[:, :S5_CHANNELS] * jax.nn.sigmoid(z[:, S5_CHANNELS:])).astype(BF16)
            o_ref[:, pos_out] = _dot(pmt_ref[...], o_tb).astype(BF16).reshape(batch, chunk, S5_CHANNELS)

    first, second = (1, 0) if reverse else (0, 1)
    stage(first, first, buf_a, buf_b, u_a)
    stage(second, second, buf_b, buf_a, u_b)


def _s5_call(u4, rb, pm, pmt, a_ri, b_cat, c_cat, d_skip, w_glu, b_glu, *, layer, reverse, lat_rows):
    batch, n_all, chunk, _ = u4.shape
    rws = chunk * batch
    n_l = lat_rows // chunk
    n_c = n_all - n_l
    assert n_l % 2 == 0 and n_c % 2 == 0
    np_all, np_l, np_c = n_all // 2, n_l // 2, n_c // 2
    dirn = 1 if reverse else 0

    def pair(g):
        g = jnp.clip(g, 0, np_all - 1)
        if reverse:
            return np_all - 1 - g
        return jnp.where(g < np_c, np_l + g, g - np_c)

    def par4(g):
        return (layer, dirn, 0, 0)

    def lyr3(g):
        return (layer, 0, 0)

    in_specs = [pl.BlockSpec((batch, 2, chunk, S5_CHANNELS), lambda g: (0, pair(g), 0, 0))]
    args = [u4]
    if not reverse:
        in_specs.append(pl.BlockSpec((2, rws, S5_CHANNELS), lambda g: (pair(g - 1), 0, 0)))
        args.append(rb)
    in_specs.append(_const_spec((rws, rws), lambda g: (0, 0)))
    args.append(pm)
    if not reverse:
        in_specs.append(_const_spec((rws, rws), lambda g: (0, 0)))
        args.append(pmt)
    in_specs += [
        _const_spec((None, None, 2, batch, S5_LANES), lambda g: (layer, dirn, 0, 0, 0)),
        _const_spec((None, None, S5_CHANNELS, 2 * S5_LANES), par4),
        _const_spec((None, None, 2 * S5_LANES, S5_CHANNELS), par4),
    ]
    args += [a_ri, b_cat, c_cat]
    if reverse:
        out_spec = pl.BlockSpec((2, rws, S5_CHANNELS), lambda g: (pair(g - 1), 0, 0))
        out_shape = jax.ShapeDtypeStruct((np_all * 2, rws, S5_CHANNELS), F32)
    else:
        in_specs += [
            _const_spec((None, 1, S5_CHANNELS), lyr3),
            _const_spec((None, S5_CHANNELS, 2 * S5_CHANNELS), lyr3),
            _const_spec((None, 1, 2 * S5_CHANNELS), lyr3),
        ]
        args += [d_skip, w_glu, b_glu]
        out_spec = pl.BlockSpec((batch, 2, chunk, S5_CHANNELS), lambda g: (0, pair(g - 1), 0, 0))
        out_shape = jax.ShapeDtypeStruct(u4.shape, BF16)
    return pl.pallas_call(
        functools.partial(_s5_kernel, reverse=reverse, chunk=chunk, batch=batch),
        grid=(np_all + 1,),
        in_specs=in_specs,
        out_specs=out_spec,
        out_shape=out_shape,
        scratch_shapes=[
            pltpu.VMEM((rws, 2 * S5_LANES), F32),
            pltpu.VMEM((rws, 2 * S5_LANES), F32),
            pltpu.VMEM((rws, S5_CHANNELS), BF16),
            pltpu.VMEM((rws, S5_CHANNELS), BF16),
            pltpu.VMEM((batch, 2 * S5_LANES), F32),
        ],
        compiler_params=_cparams(("arbitrary",)),
        name="s5_bwd" if reverse else "s5_fwd",
    )(*args)


def _merge_kernel(x_ref, mod_ref, modt_ref, gpre_ref, gpost_ref, a_ref, b_ref, c_ref, d_ref, wg_ref, wb_ref, wo_ref,
                  *rest, tail, prep_next):
    if prep_next:
        wt_next_ref, o_ref, wp_next_ref, wg_next_ref = rest
        _wprep_kernel(wt_next_ref, wp_next_ref, wg_next_ref)
    else:
        o_ref, = rest
    tm, d = x_ref.shape
    ms = (mod_ref[...], modt_ref[...])
    pieces = _mod_pieces(tm, tail)
    h = jnp.concatenate(
        [(_rms(x_ref[r0:r1], gpre_ref[1:2]) * (1.0 + ms[k][4:5]) + ms[k][3:4]).astype(BF16) for r0, r1, k in pieces],
        axis=0)
    merged = None
    for i, br in enumerate((a_ref, b_ref, c_ref, d_ref)):
        gate = jax.nn.sigmoid(_dot(h, wg_ref[:, i * d:(i + 1) * d]))
        term = gate * _dot(br[...], wb_ref[i])
        merged = term if merged is None else merged + term
    y = _dot(merged.astype(BF16), wo_ref[...])
    for r0, r1, k in pieces:
        o_ref[r0:r1] = x_ref[r0:r1] + ms[k][5:6] * _rms(y[r0:r1], gpost_ref[1:2])


def _merge_call(x3, mods, norm_pre, norm_post, a3, b3, c3, d3, wg_bf, wb_bf, wo_bf, w_next, *, layer, tm, n_blk,
                ctx_rows):
    b, _, d = x3.shape
    bj = _lag_block(b, n_blk, 0)

    def row(q):
        return (*bj(q), 0)

    def lyr3(q):
        return (layer, 0, 0)

    in_specs = _lag_specs(b, tm, d, n_blk, layer, ctx_rows > 0, 0) + [
        _const_spec((None, 3, d), lyr3),
        _const_spec((None, 3, d), lyr3),
        pl.BlockSpec((None, tm, BRANCH_WIDTH), row),
        pl.BlockSpec((None, tm, BRANCH_WIDTH), row),
        pl.BlockSpec((None, tm, BRANCH_WIDTH), row),
        pl.BlockSpec((None, tm, BRANCH_WIDTH), row),
        _const_spec((d, N_BRANCH * d), lambda q: (0, 0)),
        _const_spec((None, N_BRANCH, BRANCH_WIDTH, d), lambda q: (layer, 0, 0, 0)),
        _const_spec((None, d, d), lyr3),
    ]
    args = [x3, mods, mods, norm_pre, norm_post, a3, b3, c3, d3, wg_bf, wb_bf, wo_bf]
    out_specs = [pl.BlockSpec((None, tm, d), row)]
    out_shape = [jax.ShapeDtypeStruct((b, n_blk * tm, d), F32)]
    if w_next is not None:
        wt, l_next = w_next
        n_in, tr = wt.shape[1], LANES
        n_slab = d // tr
        assert b * n_blk >= n_slab

        def slab(q):
            return jnp.minimum(q, n_slab - 1)

        in_specs.append(pl.BlockSpec((None, n_in, tr), lambda q: (l_next, 0, slab(q))))
        args.append(wt)
        out_specs += [pl.BlockSpec((tr, PC_END), lambda q: (slab(q), 0)),
                      pl.BlockSpec((tr, N_BRANCH * d), lambda q: (slab(q), 0))]
        out_shape += [jax.ShapeDtypeStruct((d, PC_END), BF16), jax.ShapeDtypeStruct((d, N_BRANCH * d), BF16)]
    return pl.pallas_call(
        functools.partial(_merge_kernel, tail=ctx_rows, prep_next=w_next is not None),
        grid=(b * n_blk,),
        in_specs=in_specs,
        out_specs=out_specs,
        out_shape=out_shape,
        compiler_params=_cparams(("arbitrary",)),
        name="merge",
    )(*args)


def _rot_cols(w):
    q = w.shape[-1] // 4
    return jnp.concatenate([-w[..., q:2 * q], w[..., 0:q], -w[..., 3 * q:4 * q], w[..., 2 * q:3 * q]], axis=-1)


def _rope_full(rows_n, rot_dim):
    f32 = np.float32
    axis_dim = rot_dim // 2
    inv_freq = (f32(ROPE_THETA) ** (-np.arange(0, axis_dim, 2, dtype=f32) / f32(axis_dim))).astype(f32)
    row = np.repeat(np.arange(rows_n, dtype=f32), GRID_W)
    col = np.tile(np.arange(GRID_W, dtype=f32), rows_n)
    ang_r = row[:, None] * inv_freq[None, :]
    ang_c = col[:, None] * inv_freq[None, :]
    cos = np.concatenate([np.cos(ang_r), np.cos(ang_r), np.cos(ang_c), np.cos(ang_c)], axis=-1)
    sin = np.concatenate([np.sin(ang_r), np.sin(ang_r), np.sin(ang_c), np.sin(ang_c)], axis=-1)
    return cos.astype(f32), sin.astype(f32)


def _rope_table(l, tm):
    f32 = np.float32
    cg, sg = _rope_full(l // GRID_W, HEAD_DIM)
    cm, sm = _rope_full(l // GRID_W, MLA_ROPE)
    scale = f32((MLA_NOPE + MLA_ROPE) ** -0.5)
    pad = MLA_SLOT - MLA_NOPE - MLA_ROPE

    def build(cg, sg, cm, sm):
        n = cg.shape[0]
        ones = np.ones((n, MLA_NOPE), f32)
        zeros = np.zeros((n, MLA_NOPE), f32)
        return np.concatenate([
            cg, cg, sg, sg,
            scale * ones, scale * cm, zeros[:, :pad],
            zeros, scale * sm, zeros[:, :pad],
            cm, sm, zeros,
        ], axis=-1)

    lat = build(cg, sg, cm, sm)
    one_g, zero_g = np.ones((tm, HEAD_DIM), f32), np.zeros((tm, HEAD_DIM), f32)
    ctx = build(one_g, zero_g, one_g[:, :MLA_ROPE], zero_g[:, :MLA_ROPE])
    return jnp.asarray(np.concatenate([lat, ctx], axis=0))


GQA_HEAD_ORDER = (0, 2, 1, 3)


def _proj_cols(w):
    o = 0

    def take(n):
        nonlocal o
        v = w[..., o:o + n]
        o += n
        return v

    kvl, kpe = take(MLA_KV_LORA), take(MLA_ROPE)
    gk, gv = take(GQA_KV_HEADS * HEAD_DIM), take(GQA_KV_HEADS * HEAD_DIM)
    u, ql, gq, z = take(S5_CHANNELS), take(MLA_Q_LORA), take(GQA_Q_HEADS * HEAD_DIM), take(2 * GMLP_WIDTH)
    gate = w[..., o:]
    gqh = [gq[..., i * HEAD_DIM:(i + 1) * HEAD_DIM] for i in range(GQA_Q_HEADS)]
    gkh = [gk[..., i * HEAD_DIM:(i + 1) * HEAD_DIM] for i in range(GQA_KV_HEADS)]
    pad = jnp.zeros(w.shape[:-1] + (PC_U - PC_A - MLA_KV_LORA - 2 * MLA_ROPE,), w.dtype)
    wp = jnp.concatenate([z, kvl, kpe, _rot_cols(kpe), pad, u, ql]
                         + [gqh[i] for i in GQA_HEAD_ORDER] + [_rot_cols(gqh[i]) for i in GQA_HEAD_ORDER]
                         + [gk] + [_rot_cols(h) for h in gkh] + [gv], axis=-1)
    return wp, gate


def _wprep_kernel(wt_ref, wp_ref, wg_ref):
    n = wt_ref.shape[0]
    full = n // LANES * LANES
    pieces = [wt_ref[r0:r0 + LANES, :].T for r0 in range(0, full, LANES)]
    if full < n:
        rest = jnp.concatenate([wt_ref[full:, :], jnp.zeros((LANES - (n - full), wt_ref.shape[1]), F32)], axis=0)
        pieces.append(rest.T[:, :n - full])
    wp, wg = _proj_cols(jnp.concatenate(pieces, axis=1))
    wp_ref[...] = wp.astype(BF16)
    wg_ref[...] = wg.astype(BF16)


def _wprep_call(wt, layer):
    _, n, d = wt.shape
    tr = 256
    n_gate = N_BRANCH * d
    return pl.pallas_call(
        _wprep_kernel,
        grid=(d // tr,),
        in_specs=[pl.BlockSpec((None, n, tr), lambda i: (layer, 0, i))],
        out_specs=[pl.BlockSpec((tr, PC_END), lambda i: (i, 0)), pl.BlockSpec((tr, n_gate), lambda i: (i, 0))],
        out_shape=[jax.ShapeDtypeStruct((d, PC_END), BF16), jax.ShapeDtypeStruct((d, n_gate), BF16)],
        compiler_params=_cparams(("parallel",)),
        name="w_prep",
    )(wt)


def _mla_kv_weight(w_ukv):
    depth = w_ukv.shape[0]
    wh = w_ukv.reshape(depth, MLA_KV_LORA, MLA_HEADS, MLA_NOPE + MLA_V)
    k_nope, v = wh[..., :MLA_NOPE], wh[..., MLA_NOPE:]
    kslot = jnp.concatenate([k_nope, jnp.zeros((depth, MLA_KV_LORA, MLA_HEADS, MLA_SLOT - MLA_NOPE), F32)], axis=-1)
    top = jnp.concatenate([kslot.reshape(depth, MLA_KV_LORA, -1), v.reshape(depth, MLA_KV_LORA, -1)], axis=-1)
    eye = jnp.eye(MLA_ROPE, dtype=F32)
    pe_slot = jnp.concatenate([jnp.zeros((MLA_ROPE, MLA_NOPE), F32), eye,
                               jnp.zeros((MLA_ROPE, MLA_SLOT - MLA_NOPE - MLA_ROPE), F32)], axis=-1)
    pe_rows = jnp.concatenate([jnp.tile(pe_slot, (1, MLA_HEADS)), jnp.zeros((MLA_ROPE, MLA_HEADS * MLA_V), F32)],
                              axis=-1)
    pe_rows = jnp.broadcast_to(pe_rows, (depth,) + pe_rows.shape)
    tail = jnp.zeros((depth, 256 - MLA_KV_LORA - 2 * MLA_ROPE, MLA_KV_COLS), F32)
    return jnp.concatenate([top, pe_rows, pe_rows, tail], axis=1).astype(BF16)


def _mla_q_weight(w_uq):
    depth = w_uq.shape[0]
    wh = w_uq.reshape(depth, MLA_Q_LORA, MLA_HEADS, MLA_NOPE + MLA_ROPE)
    nope, pe = wh[..., :MLA_NOPE], wh[..., MLA_NOPE:]
    pad = jnp.zeros((depth, MLA_Q_LORA, MLA_HEADS, MLA_SLOT - MLA_NOPE - MLA_ROPE), F32)
    full = jnp.concatenate([nope, pe, pad], axis=-1).reshape(depth, MLA_Q_LORA, -1)
    rot = jnp.concatenate([jnp.zeros_like(nope), _rot_cols(pe), pad], axis=-1).reshape(depth, MLA_Q_LORA, -1)
    return jnp.concatenate([full, rot], axis=-1).astype(BF16)


def _block_diag(w):
    g, a, b = w.shape[-3:]
    lead = w.shape[:-3]
    cols = jnp.swapaxes(w, -3, -2).reshape(lead + (a, g * b))
    tiled = jnp.tile(cols, (1,) * len(lead) + (g, 1))
    same = (jnp.arange(g * a)[:, None] // a) == (jnp.arange(g * b)[None, :] // b)
    return jnp.where(same, tiled, jnp.zeros((), w.dtype))


def kernel(x, c, ctx, c_ctx, w_ada, b_ada, norm_pre, norm_post, w_ffn_in, w_ffn_out, w_in, mla_q_norm, mla_w_uq,
           mla_kv_norm, mla_w_ukv, gqa_sink, s5_lam_re, s5_lam_im, s5_log_dt, s5_b_re, s5_b_im, s5_c_re, s5_c_im,
           s5_d, s5_w_glu, s5_b_glu, gmlp_norm, gmlp_w_s, gmlp_b_s, w_branch, w_out):
    b, l, d = x.shape
    cl = ctx.shape[1]
    r = l + cl
    depth = w_ada.shape[0]
    assert b == SUBLANES and b < MOD_ROWS
    assert l % GRID_W == 0 and l >= 3 * W_BLOCK and l % cl == 0
    assert l % GMLP_CHUNK == 0 and WIDE_TILE % GMLP_CHUNK == 0 and l % MLA_Q_TILE == 0 and l % GQA_Q_TILE == 0
    assert l % S5_CHUNK == 0 and cl % S5_CHUNK == 0 and l % LAT_TILE == 0
    assert r % WIDE_TILE == 0 and r % XWIDE_TILE == 0 and XWIDE_TILE % GMLP_CHUNK == 0 and cl <= WIDE_TILE

    wi_bf, wo_bf_ffn = w_ffn_in[0, 0].astype(BF16), w_ffn_out[0, 0].astype(BF16)
    w_in_t = jnp.swapaxes(w_in, 1, 2)
    wp_bf, wg_bf = _wprep_call(w_in_t, 0)
    wkv_bf = _mla_kv_weight(mla_w_ukv)
    wq_bf = _mla_q_weight(mla_w_uq)
    head_order = jnp.array(GQA_HEAD_ORDER)
    wb1 = w_branch[:, 1].reshape(depth, GQA_Q_HEADS, HEAD_DIM, d)[:, head_order].reshape(depth, BRANCH_WIDTH, d)
    wb_bf = jnp.concatenate([w_branch[:, :1], wb1[:, None], w_branch[:, 2:]], axis=1).astype(BF16)
    wo_bf = w_out.astype(BF16)
    sink_rows = jnp.broadcast_to(
        jnp.concatenate([gqa_sink[:, head_order], jnp.zeros((depth, SUBLANES - GQA_Q_HEADS), F32)], axis=1)[:, :, None],
        (depth, SUBLANES, LANES))
    ws_bf = jnp.transpose(gmlp_w_s, (0, 2, 1, 3)).reshape(depth, GMLP_CHUNK, GMLP_GROUPS * GMLP_CHUNK).astype(BF16)
    bs_f = jnp.repeat(jnp.transpose(gmlp_b_s, (0, 2, 1)), GMLP_WIDTH // GMLP_GROUPS, axis=2)
    tab = _rope_table(l, cl)
    mla_kv_norm3 = mla_kv_norm[:, None, :]
    mla_q_norm3 = mla_q_norm[:, None, :]
    gmlp_norm3 = gmlp_norm[:, None, :]
    s5_d3 = s5_d[:, None, :]
    s5_b_glu3 = s5_b_glu[:, None, :]
    wglu_bf = s5_w_glu.astype(BF16)
    tb = jnp.arange(S5_CHUNK * b)
    pm = (tb[None, :] == ((tb % b) * S5_CHUNK + tb // b)[:, None]).astype(BF16)
    pmt = pm.T

    hg, p, g = S5_GROUP, S5_STATE, S5_GROUPS
    n_par = depth * 2 * g
    rep = lambda t: jnp.repeat(t.reshape(n_par, p), hg, axis=1)
    ldt = jnp.broadcast_to(s5_log_dt[..., None], (depth, 2, g, p))
    a_re_x, a_im_x, bb_re, bb_im = _s5_disc_call(
        rep(s5_lam_re), rep(s5_lam_im), rep(ldt), s5_b_re.reshape(n_par, p * hg), s5_b_im.reshape(n_par, p * hg))
    a_ri = jnp.broadcast_to(
        jnp.stack([a_re_x[:, ::hg].reshape(depth, 2, g * p), a_im_x[:, ::hg].reshape(depth, 2, g * p)], axis=2)
        [:, :, :, None, :], (depth, 2, 2, b, g * p))
    bb_re = jnp.swapaxes(bb_re.reshape(depth, 2, g, p, hg), -1, -2)
    bb_im = jnp.swapaxes(bb_im.reshape(depth, 2, g, p, hg), -1, -2)
    s5_b_cat = jnp.concatenate([_block_diag(bb_re), _block_diag(bb_im)], axis=-1).astype(BF16)
    s5_c_cat = jnp.concatenate([_block_diag(jnp.swapaxes(s5_c_re, -1, -2)),
                                _block_diag(jnp.swapaxes(-s5_c_im, -1, -2))], axis=-2).astype(BF16)

    cs = jnp.concatenate([c, c_ctx[None, :], jnp.zeros((MOD_ROWS - b - 1, d), F32)], axis=0)
    mods = _ada_call(cs, w_ada, b_ada[:, None, :]).reshape(depth, MOD_ROWS, N_MOD, d)

    xs = jnp.concatenate([x, ctx], axis=1)
    n_wide = r // WIDE_TILE
    for layer in range(depth):
        last = layer == depth - 1
        xs, wi_bf, wo_bf_ffn = _ffn_call(xs, mods, norm_pre, norm_post, wi_bf, wo_bf_ffn,
                                         (w_ffn_in, w_ffn_out, layer, 1),
                                         layer=layer, s=0, tm=WIDE_TILE, n_blk=n_wide, ctx_rows=cl)
        mq, mkv, gq, gkv, u3, d3 = _proj_call(
            xs, mods, norm_pre, wp_bf, tab, mla_kv_norm3, wkv_bf, mla_q_norm3, wq_bf, gmlp_norm3, ws_bf, bs_f,
            layer=layer, lat_rows=l)
        a_dst, b_dst = (None, None) if last else _ctx_attn_call(mq, mkv, gq, gkv, sink_rows, layer=layer, lat_rows=l)
        a3 = _mla_lat_call(mq, mkv, a_dst, lat_rows=l)
        b3 = _gqa_lat_call(gq, gkv, sink_rows, b_dst, layer=layer, lat_rows=l)
        u4 = u3.reshape(b, r // S5_CHUNK, S5_CHUNK, S5_CHANNELS)
        rb = _s5_call(u4, None, pm, None, a_ri, s5_b_cat, s5_c_cat, None, None, None,
                      layer=layer, reverse=True, lat_rows=l)
        c3 = _s5_call(u4, rb, pm, pmt, a_ri, s5_b_cat, s5_c_cat, s5_d3, wglu_bf, s5_b_glu3,
                      layer=layer, reverse=False, lat_rows=l).reshape(b, r, S5_CHANNELS)
        if last:
            tm, n_blk, ctx_rows = LAT_TILE, l // LAT_TILE, 0
            tm_merge, n_merge = tm, n_blk
        else:
            tm, n_blk, ctx_rows = WIDE_TILE, n_wide, cl
            tm_merge, n_merge = tm, n_blk
        outs = _merge_call(xs, mods, norm_pre, norm_post, a3, b3, c3, d3, wg_bf, wb_bf, wo_bf,
                           None if last else (w_in_t, layer + 1),
                           layer=layer, tm=tm_merge, n_blk=n_merge, ctx_rows=ctx_rows)
        xs = outs[0]
        if not last:
            wp_bf, wg_bf = outs[1], outs[2]
        outs = _ffn_call(xs, mods, norm_pre, norm_post, wi_bf, wo_bf_ffn,
                         None if last else (w_ffn_in, w_ffn_out, layer + 1, 0),
                         layer=layer, s=2, tm=tm, n_blk=n_blk, ctx_rows=ctx_rows)
        xs = outs[0]
        if not last:
            wi_bf, wo_bf_ffn = outs[1], outs[2]
    return xs
```

```python
import functools

import jax
import jax.numpy as jnp
import numpy as np
from jax import lax
from jax.experimental import pallas as pl
from jax.experimental.pallas import tpu as pltpu

F32 = jnp.float32
BF16 = jnp.bfloat16

GRID_W = 64
HEAD_DIM = 64
ROPE_THETA = 10000.0
RMS_EPS = 1e-6
LN_EPS = 1e-5
NEG_INF = -1e30
MLA_HEADS = 4
MLA_Q_LORA = 256
MLA_KV_LORA = 128
MLA_NOPE = 64
MLA_ROPE = 32
MLA_V = 64
GQA_Q_HEADS = 4
GQA_KV_HEADS = 2
WINDOW = 128
W_BLOCK = 128
S5_CHANNELS = 256
S5_GROUP = 16
S5_GROUPS = S5_CHANNELS // S5_GROUP
S5_STATE = 64
GMLP_WIDTH = 256
GMLP_CHUNK = 128
GMLP_GROUPS = 4
N_BRANCH = 4
BRANCH_WIDTH = 256
N_MOD = 9

LANES = 128
SUBLANES = 8
MXU_TILE = 256
VMEM_LIMIT_BYTES = 56 * 1024 * 1024

WIDE_TILE = 768
XWIDE_TILE = 1152
LAT_TILE = 512
MLA_Q_TILE = 512
GQA_Q_TILE = 512
S5_CHUNK = 64
MOD_ROWS = 16
MLA_SLOT = LANES
S5_LANES = S5_GROUPS * S5_STATE

PC_Z = 0
PC_A = PC_Z + 2 * GMLP_WIDTH
PC_U = PC_A + 256
PC_QL = PC_U + S5_CHANNELS
PC_GQ = PC_QL + MLA_Q_LORA
PC_GQR = PC_GQ + GQA_Q_HEADS * HEAD_DIM
PC_GK = PC_GQR + GQA_Q_HEADS * HEAD_DIM
PC_GKR = PC_GK + GQA_KV_HEADS * HEAD_DIM
PC_GV = PC_GKR + GQA_KV_HEADS * HEAD_DIM
PC_END = PC_GV + GQA_KV_HEADS * HEAD_DIM

TC_GC, TC_GS, TC_MQC, TC_MQS, TC_MKT, TC_END = 0, 128, 256, 384, 512, 640

MLA_K_COLS = MLA_HEADS * MLA_SLOT
MLA_KV_COLS = MLA_K_COLS + MLA_HEADS * MLA_V


def _cparams(sem):
    return pltpu.CompilerParams(dimension_semantics=sem, vmem_limit_bytes=VMEM_LIMIT_BYTES)


def _const_spec(shape, index_map):
    return pl.BlockSpec(shape, index_map, pipeline_mode=pl.Buffered(1))


def _rms(x, gain):
    return x * lax.rsqrt(jnp.mean(x * x, axis=-1, keepdims=True) + RMS_EPS) * gain


def _dot(a, b):
    return jnp.dot(a, b, preferred_element_type=F32)


def _dot_nt(a, b):
    return lax.dot_general(a, b, (((1,), (1,)), ((), ())), preferred_element_type=F32)


def _ada_kernel(cs_ref, w_ref, b_ref, o_ref):
    cs = cs_ref[...]
    s = (cs * jax.nn.sigmoid(cs)).astype(BF16)
    o_ref[...] = _dot(s, w_ref[...].astype(BF16)) + b_ref[...]


def _ada_call(cs, w_ada, b_ada3):
    depth, d, n = w_ada.shape
    tn = d
    return pl.pallas_call(
        _ada_kernel,
        grid=(depth, n // tn),
        in_specs=[
            pl.BlockSpec((MOD_ROWS, d), lambda l, j: (0, 0)),
            pl.BlockSpec((None, d, tn), lambda l, j: (l, 0, j)),
            pl.BlockSpec((None, 1, tn), lambda l, j: (l, 0, j)),
        ],
        out_specs=pl.BlockSpec((None, MOD_ROWS, tn), lambda l, j: (l, 0, j)),
        out_shape=jax.ShapeDtypeStruct((depth, MOD_ROWS, n), F32),
        compiler_params=_cparams(("parallel", "parallel")),
        name="ada_mod",
    )(cs, w_ada, b_ada3)


def _s5_disc_kernel(lre_ref, lim_ref, ldt_ref, bre_ref, bim_ref, are_ref, aim_ref, bbre_ref, bbim_ref):
    lam_re = jnp.minimum(lre_ref[...], -1e-4)
    lam_im = lim_ref[...]
    dt = jnp.exp(ldt_ref[...])
    mag = jnp.exp(lam_re * dt)
    a_re = mag * jnp.cos(lam_im * dt)
    a_im = mag * jnp.sin(lam_im * dt)
    nr, ni = a_re - 1.0, a_im
    den = lam_re * lam_re + lam_im * lam_im
    coef_re = (nr * lam_re + ni * lam_im) / den
    coef_im = (ni * lam_re - nr * lam_im) / den
    b_re, b_im = bre_ref[...], bim_ref[...]
    are_ref[...] = a_re
    aim_ref[...] = a_im
    bbre_ref[...] = coef_re * b_re - coef_im * b_im
    bbim_ref[...] = coef_re * b_im + coef_im * b_re


def _s5_disc_call(lre, lim, ldt, bre, bim):
    shp = jax.ShapeDtypeStruct(lre.shape, F32)
    return pl.pallas_call(_s5_disc_kernel, out_shape=(shp, shp, shp, shp), name="s5_disc")(lre, lim, ldt, bre, bim)


def _mod_pieces(tm, tail):
    return ((0, tm - tail, 0), (tm - tail, tm, 1)) if tail else ((0, tm, 0),)


def _lag_block(b, n_blk, lag):
    def bj(q):
        q = jnp.clip(q - lag, 0, b * n_blk - 1)
        return q // n_blk, q % n_blk
    return bj


def _lag_specs(b, tm, d, n_blk, layer, ctx_tail, lag):
    bj = _lag_block(b, n_blk, lag)

    def tail_row(q):
        bi, j = bj(q)
        return jnp.where(j == n_blk - 1, b, bi) if ctx_tail else bi

    return [
        pl.BlockSpec((None, tm, d), lambda q: (*bj(q), 0)),
        pl.BlockSpec((None, None, N_MOD, d), lambda q: (layer, bj(q)[0], 0, 0)),
        pl.BlockSpec((None, None, N_MOD, d), lambda q: (layer, tail_row(q), 0, 0)),
    ]


def _ffn_kernel(xn_ref, modn_ref, modnt_ref, xo_ref, modo_ref, modot_ref, gpre_ref, gpost_ref, wi_ref, wo_ref, *rest,
                s, d_ff, chunks, tail, n_steps, cast_next):
    if cast_next:
        wi_next_ref, wo_next_ref, o_ref, wi_cast_ref, wo_cast_ref, h_ref, y_ref = rest
    else:
        o_ref, h_ref, y_ref = rest
    q = pl.program_id(0)
    pieces = _mod_pieces(xn_ref.shape[0], tail)

    def cast_slabs():
        if cast_next:
            wi_cast_ref[...] = wi_next_ref[...].astype(BF16)
            wo_cast_ref[...] = wo_next_ref[...].astype(BF16)

    def pre_norm():
        ms = (modn_ref[...], modnt_ref[...])
        return jnp.concatenate(
            [(_rms(xn_ref[r0:r1], gpre_ref[s:s + 1]) * (1.0 + ms[k][3 * s + 1:3 * s + 2]) + ms[k][3 * s:3 * s + 1])
             .astype(BF16) for r0, r1, k in pieces], axis=0)

    def post_norm():
        ms = (modo_ref[...], modot_ref[...])
        return jnp.concatenate(
            [xo_ref[r0:r1] + 0.5 * ms[k][3 * s + 2:3 * s + 3] * _rms(y_ref[r0:r1], gpost_ref[s:s + 1])
             for r0, r1, k in pieces], axis=0)

    def up(h, c0, c1):
        a = _dot(h, wi_ref[:, c0:c1])
        g = _dot(h, wi_ref[:, d_ff + c0:d_ff + c1])
        return (a * jax.nn.sigmoid(a) * g).astype(BF16)

    @pl.when(q == 0)
    def _():
        h_ref[...] = pre_norm()
        y_ref[...] = jnp.zeros_like(y_ref)
        cast_slabs()

    @pl.when(jnp.logical_and(q > 0, q < n_steps - 1))
    def _():
        always = q < n_steps
        h = h_ref[...]
        acts = [up(h, *chunks[0])]
        cast_slabs()
        out = post_norm()
        o_ref[...] = out
        h = jnp.where(always, h, out.astype(BF16))
        acts += [up(h, c0, c1) for c0, c1 in chunks[1:]]
        h_new = pre_norm()
        h_ref[...] = h_new
        w0 = chunks[0][1] - chunks[0][0]
        acts[0] = jnp.concatenate([jnp.where(always, acts[0][:, :h_new.shape[1]], h_new), acts[0][:, h_new.shape[1]:]],
                                  axis=1) if w0 > h_new.shape[1] else jnp.where(always, acts[0], h_new[:, :w0])
        y = None
        for act, (c0, c1) in zip(acts, chunks):
            part = _dot(act, wo_ref[c0:c1, :])
            y = part if y is None else y + part
        y_ref[...] = y

    @pl.when(q == n_steps - 1)
    def _():
        o_ref[...] = post_norm()
        cast_slabs()


FFN_CAST_SLABS = 16


def _ffn_call(x3, mods, norm_pre, norm_post, w_in_bf, w_out_bf, w_next, *, layer, s, tm, n_blk, ctx_rows):
    b, _, d = x3.shape
    d_ff = w_out_bf.shape[0]
    half = (d_ff // MXU_TILE + 1) // 2 * MXU_TILE
    chunks = ((0, half), (half, d_ff)) if 0 < half < d_ff else ((0, d_ff),)
    n = b * n_blk
    ns = FFN_CAST_SLABS
    while ns > n + 2:
        ns //= 2
    assert d % (ns * 2 * SUBLANES) == 0 and d_ff % (ns * 2 * SUBLANES) == 0
    kern = functools.partial(_ffn_kernel, s=s, d_ff=d_ff, chunks=chunks, tail=ctx_rows, n_steps=n + 2,
                             cast_next=w_next is not None)

    def specs(lag):
        return _lag_specs(b, tm, d, n_blk, layer, ctx_rows > 0, lag)

    in_specs = specs(0) + specs(2) + [
        _const_spec((None, 3, d), lambda q: (layer, 0, 0)),
        _const_spec((None, 3, d), lambda q: (layer, 0, 0)),
        _const_spec((d, 2 * d_ff), lambda q: (0, 0)),
        _const_spec((d_ff, d), lambda q: (0, 0)),
    ]
    args = [x3, mods, mods, x3, mods, mods, norm_pre, norm_post, w_in_bf, w_out_bf]
    out_specs = [specs(2)[0]]
    out_shape = [jax.ShapeDtypeStruct((b, n_blk * tm, d), F32)]
    if w_next is not None:
        wi_f32, wo_f32, l_next, which_next = w_next

        def slab(q):
            return jnp.minimum(q, ns - 1)

        in_specs += [
            pl.BlockSpec((None, None, d // ns, 2 * d_ff), lambda q: (l_next, which_next, slab(q), 0)),
            pl.BlockSpec((None, None, d_ff // ns, d), lambda q: (l_next, which_next, slab(q), 0)),
        ]
        args += [wi_f32, wo_f32]
        out_specs += [pl.BlockSpec((d // ns, 2 * d_ff), lambda q: (slab(q), 0)),
                      pl.BlockSpec((d_ff // ns, d), lambda q: (slab(q), 0))]
        out_shape += [jax.ShapeDtypeStruct((d, 2 * d_ff), BF16), jax.ShapeDtypeStruct((d_ff, d), BF16)]
    return pl.pallas_call(
        kern,
        grid=(n + 2,),
        in_specs=in_specs,
        out_specs=out_specs,
        out_shape=out_shape,
        scratch_shapes=[pltpu.VMEM((tm, d), BF16), pltpu.VMEM((tm, d), F32)],
        compiler_params=_cparams(("arbitrary",)),
        name="ffn",
    )(*args)


def _proj_kernel(x_ref, mod_ref, modt_ref, gpre_ref, wp_ref, tab_ref, gkvn_ref, wkv_ref, gqn_ref, wq_ref,
                 gmn_ref, ws_ref, bs_ref,
                 mq_ref, mkv_ref, gq_ref, gkv_ref, u_ref, d_ref, *, tail):
    ms = (mod_ref[...], modt_ref[...])
    h = jnp.concatenate(
        [(_rms(x_ref[r0:r1], gpre_ref[1:2]) * (1.0 + ms[k][4:5]) + ms[k][3:4]).astype(BF16)
         for r0, r1, k in _mod_pieces(x_ref.shape[0], tail)], axis=0)
    p = _dot(h, wp_ref[...])
    tab = tab_ref[...]
    gc, gs = tab[:, TC_GC:TC_GS], tab[:, TC_GS:TC_MQC]
    mqc, mqs, mkt = tab[:, TC_MQC:TC_MQS], tab[:, TC_MQS:TC_MKT], tab[:, TC_MKT:TC_END]

    kvl = p[:, PC_A:PC_A + MLA_KV_LORA]
    kvn = _rms(kvl, gkvn_ref[...])
    pe = p[:, PC_A + MLA_KV_LORA:PC_U] * mkt
    a2 = jnp.concatenate([kvn, pe], axis=-1).astype(BF16)
    mkv_ref[...] = _dot(a2, wkv_ref[...]).astype(BF16)

    qn = _rms(p[:, PC_QL:PC_GQ], gqn_ref[...]).astype(BF16)
    q = _dot(qn, wq_ref[...])
    qs = [q[:, hh * MLA_SLOT:(hh + 1) * MLA_SLOT] * mqc
          + q[:, MLA_K_COLS + hh * MLA_SLOT:MLA_K_COLS + (hh + 1) * MLA_SLOT] * mqs
          for hh in range(MLA_HEADS)]
    mq_ref[...] = jnp.concatenate(qs, axis=-1).astype(BF16)

    gq = [(p[:, PC_GQ + j * LANES:PC_GQ + (j + 1) * LANES] * gc
           + p[:, PC_GQR + j * LANES:PC_GQR + (j + 1) * LANES] * gs) * (HEAD_DIM ** -0.5)
          for j in range(GQA_Q_HEADS * HEAD_DIM // LANES)]
    gq_ref[...] = jnp.concatenate(gq, axis=-1).astype(BF16)
    gk = p[:, PC_GK:PC_GKR] * gc + p[:, PC_GKR:PC_GV] * gs
    gkv_ref[...] = jnp.concatenate([gk, p[:, PC_GV:PC_END]], axis=-1).astype(BF16)

    u_ref[...] = p[:, PC_U:PC_QL].astype(BF16)

    zz = jax.nn.gelu(p[:, PC_Z:PC_A])
    ug, v = zz[:, :GMLP_WIDTH], zz[:, GMLP_WIDTH:]
    vc = v - jnp.mean(v, axis=-1, keepdims=True)
    vn = vc * lax.rsqrt(jnp.mean(vc * vc, axis=-1, keepdims=True) + LN_EPS) * gmn_ref[...]
    group_of_lane = lax.broadcasted_iota(jnp.int32, (GMLP_CHUNK, GMLP_WIDTH), 1) // (GMLP_WIDTH // GMLP_GROUPS)
    outs = []
    for ci in range(x_ref.shape[0] // GMLP_CHUNK):
        r0 = ci * GMLP_CHUNK
        vck = vn[r0:r0 + GMLP_CHUNK]
        vbd = jnp.concatenate([jnp.where(group_of_lane == g, vck, 0.0) for g in range(GMLP_GROUPS)], axis=0)
        mixed = _dot(ws_ref[...], vbd.astype(BF16)) + bs_ref[...]
        outs.append(ug[r0:r0 + GMLP_CHUNK] * mixed)
    d_ref[...] = jnp.concatenate(outs, axis=0).astype(BF16)


def _proj_call(x3, mods, norm_pre, wp_bf, tab, mla_kv_norm3, wkv_bf, mla_q_norm3, wq_bf, gmlp_norm3, ws_bf, bs_f,
               *, layer, lat_rows):
    b, r, d = x3.shape
    tm = XWIDE_TILE
    n_blk = r // tm
    bj = _lag_block(b, n_blk, 0)

    def row(q):
        return (*bj(q), 0)

    def lyr3(q):
        return (layer, 0, 0)

    widths = (MLA_K_COLS, MLA_KV_COLS, GQA_Q_HEADS * HEAD_DIM, 2 * GQA_KV_HEADS * HEAD_DIM, S5_CHANNELS, GMLP_WIDTH)
    return pl.pallas_call(
        functools.partial(_proj_kernel, tail=r - lat_rows),
        grid=(b * n_blk,),
        in_specs=_lag_specs(b, tm, d, n_blk, layer, True, 0) + [
            _const_spec((None, 3, d), lyr3),
            _const_spec((d, PC_END), lambda q: (0, 0)),
            pl.BlockSpec((tm, TC_END), lambda q: (bj(q)[1], 0)),
            _const_spec((None, 1, MLA_KV_LORA), lyr3),
            _const_spec((None, 256, MLA_KV_COLS), lyr3),
            _const_spec((None, 1, MLA_Q_LORA), lyr3),
            _const_spec((None, MLA_Q_LORA, 2 * MLA_K_COLS), lyr3),
            _const_spec((None, 1, GMLP_WIDTH), lyr3),
            _const_spec((None, GMLP_CHUNK, GMLP_GROUPS * GMLP_CHUNK), lyr3),
            _const_spec((None, GMLP_CHUNK, GMLP_WIDTH), lyr3),
        ],
        out_specs=[pl.BlockSpec((None, tm, w), row) for w in widths],
        out_shape=[jax.ShapeDtypeStruct((b, r, w), BF16) for w in widths],
        compiler_params=_cparams(("parallel",)),
        name="in_proj",
    )(x3, mods, mods, norm_pre, wp_bf, tab, mla_kv_norm3, wkv_bf, mla_q_norm3, wq_bf, gmlp_norm3, ws_bf, bs_f)


def _mla_attend(q_ref, kv_ref):
    tq = q_ref.shape[0]
    head_of_lane = lax.broadcasted_iota(jnp.int32, (tq, MLA_HEADS * MLA_V), 1) // MLA_V
    acc = jnp.zeros((tq, MLA_HEADS * MLA_V), F32)
    v = kv_ref[:, MLA_K_COLS:]
    sls = [slice(hh * MLA_SLOT, (hh + 1) * MLA_SLOT) for hh in range(MLA_HEADS)]
    scs = [_dot_nt(q_ref[:, sl], kv_ref[:, sl]) for sl in sls]
    ps = [jnp.exp(sc - jnp.max(sc, axis=-1, keepdims=True)) for sc in scs]
    dens = [jnp.sum(p, axis=-1, keepdims=True) for p in ps]
    o_all = _dot(jnp.concatenate([p.astype(BF16) for p in ps], axis=0), v)
    for hh in range(MLA_HEADS):
        acc = jnp.where(head_of_lane == hh, o_all[hh * tq:(hh + 1) * tq] / dens[hh], acc)
    return acc.astype(BF16)


def _gqa_blocks(qs, kc, vc, sink_ref, bands):
    tq = qs[0].shape[0]
    kw = GQA_KV_HEADS * HEAD_DIM
    lo = lax.broadcasted_iota(jnp.int32, (tq, kw), 1) < HEAD_DIM
    sk = jnp.concatenate([jnp.broadcast_to(sink_ref[r:r + 1, 0:1], (tq, 1)) for r in range(GQA_Q_HEADS)], axis=0)
    qsts = []
    for q in qs:
        q0, q1 = q[:, :kw], q[:, kw:]
        zero = jnp.zeros_like(q0)
        qsts.append(jnp.concatenate([jnp.where(lo, q0, zero), jnp.where(lo, zero, q0),
                                     jnp.where(lo, q1, zero), jnp.where(lo, zero, q1)], axis=0))
    scs = [_dot_nt(qst, kc) for qst in qsts]
    mxs = [jnp.maximum(jnp.max(sc, axis=-1, keepdims=True), sk) for sc in scs]
    if bands is not None:
        sbs = [jnp.where(jnp.concatenate([valid] * GQA_Q_HEADS, axis=0), _dot_nt(qst, kb), NEG_INF)
               for qst, (kb, _, valid) in zip(qsts, bands)]
        mxs = [jnp.maximum(mx, jnp.max(sb, axis=-1, keepdims=True)) for mx, sb in zip(mxs, sbs)]
    pcs = [jnp.exp(sc - mx) for sc, mx in zip(scs, mxs)]
    dens = [jnp.sum(pc, axis=-1, keepdims=True) + jnp.exp(sk - mx) for pc, mx in zip(pcs, mxs)]
    os_ = [_dot(pc.astype(BF16), vc) for pc in pcs]
    if bands is not None:
        pbs = [jnp.exp(sb - mx) for sb, mx in zip(sbs, mxs)]
        dens = [den + jnp.sum(pb, axis=-1, keepdims=True) for den, pb in zip(dens, pbs)]
        os_ = [o + _dot(pb.astype(BF16), vb) for o, pb, (_, vb, _) in zip(os_, pbs, bands)]
    outs = []
    for o, den in zip(os_, dens):
        o = o / den
        c0 = jnp.where(lo, o[0:tq], o[tq:2 * tq])
        c1 = jnp.where(lo, o[2 * tq:3 * tq], o[3 * tq:4 * tq])
        outs.append(jnp.concatenate([c0, c1], axis=-1))
    return outs


def _gqa_lat_kernel(q_ref, kv_ref, sink_ref, *rest, lat_rows):
    o_ref = rest[-1]
    kw = GQA_KV_HEADS * HEAD_DIM
    kc, vc = kv_ref[lat_rows:, :kw], kv_ref[lat_rows:, kw:]
    nbk = 3 * W_BLOCK
    n_in = q_ref.shape[0] // W_BLOCK
    qs, bands = [], []
    for i in range(n_in):
        n = pl.program_id(1) * n_in + i
        start = pl.multiple_of(jnp.clip((n - 1) * W_BLOCK, 0, lat_rows - nbk), W_BLOCK)
        kb = kv_ref[pl.ds(start, nbk), :kw]
        vb = kv_ref[pl.ds(start, nbk), kw:]
        qpos = n * W_BLOCK + lax.broadcasted_iota(jnp.int32, (W_BLOCK, nbk), 0)
        kpos = start + lax.broadcasted_iota(jnp.int32, (W_BLOCK, nbk), 1)
        bands.append((kb, vb, jnp.abs(qpos - kpos) <= WINDOW))
        qs.append(q_ref[i * W_BLOCK:(i + 1) * W_BLOCK])
    outs = _gqa_blocks(qs, kc, vc, sink_ref, bands)
    o_ref[...] = jnp.concatenate(outs, axis=0).astype(BF16)


def _ctx_attn_kernel(mq_ref, mkv_ref, gq_ref, gkv_ref, sink_ref, a_ref, b_ref, *, lat_rows):
    kw = GQA_KV_HEADS * HEAD_DIM
    tq = gq_ref.shape[0] // 2
    a_ctx = _mla_attend(mq_ref, mkv_ref)
    b_ctx = _gqa_blocks([gq_ref[:tq], gq_ref[tq:]], gkv_ref[:, :kw], gkv_ref[:, kw:], sink_ref, None)
    for o_ref, val in ((a_ref, a_ctx), (b_ref, jnp.concatenate(b_ctx, axis=0).astype(BF16))):
        o_ref[:lat_rows] = jnp.zeros((lat_rows, o_ref.shape[1]), o_ref.dtype)
        o_ref[lat_rows:] = val


def _ctx_attn_call(mq, mkv, gq, gkv, sink_rows, *, layer, lat_rows):
    b, r, w = gq.shape
    c = r - lat_rows
    blk = lat_rows // c
    wa = MLA_HEADS * MLA_V

    def ctx(bi):
        return (bi, blk, 0)

    return pl.pallas_call(
        functools.partial(_ctx_attn_kernel, lat_rows=lat_rows),
        grid=(b,),
        in_specs=[
            pl.BlockSpec((None, c, MLA_K_COLS), ctx),
            pl.BlockSpec((None, c, MLA_KV_COLS), ctx),
            pl.BlockSpec((None, c, w), ctx),
            pl.BlockSpec((None, c, w), ctx),
            pl.BlockSpec((None, SUBLANES, LANES), lambda bi: (layer, 0, 0)),
        ],
        out_specs=[pl.BlockSpec((None, r, wa), lambda bi: (bi, 0, 0)), pl.BlockSpec((None, r, w), lambda bi: (bi, 0, 0))],
        out_shape=[jax.ShapeDtypeStruct((b, r, wa), BF16), jax.ShapeDtypeStruct((b, r, w), BF16)],
        compiler_params=_cparams(("parallel",)),
        name="ctx_attention",
    )(mq, mkv, gq, gkv, sink_rows)


def _lat_attn_kernel(mq_ref, mkv_ref, gq_ref, gkv_ref, sink_ref, *rest, lat_rows):
    a_ref, b_ref = rest[-2:]
    _gqa_lat_kernel(gq_ref, gkv_ref, sink_ref, b_ref, lat_rows=lat_rows)
    a_ref[...] = _mla_attend(mq_ref, mkv_ref)


def _lat_attn_call(mq, mkv, gq, gkv, sink_rows, a_dst, b_dst, *, layer, lat_rows):
    b, r, w = gq.shape
    assert MLA_Q_TILE == GQA_Q_TILE
    tq = GQA_Q_TILE
    wa = MLA_HEADS * MLA_V

    def tile(bi, j):
        return (bi, j, 0)

    def whole(bi, j):
        return (bi, 0, 0)

    in_specs = [
        pl.BlockSpec((None, tq, MLA_K_COLS), tile),
        pl.BlockSpec((None, r, MLA_KV_COLS), whole),
        pl.BlockSpec((None, tq, w), tile),
        pl.BlockSpec((None, r, w), whole),
        pl.BlockSpec((None, SUBLANES, LANES), lambda bi, j: (layer, 0, 0)),
    ]
    args = [mq, mkv, gq, gkv, sink_rows]
    if a_dst is not None:
        in_specs += [pl.BlockSpec(memory_space=pl.ANY), pl.BlockSpec(memory_space=pl.ANY)]
        args += [a_dst, b_dst]
    rows = lat_rows if a_dst is None else r
    return pl.pallas_call(
        functools.partial(_lat_attn_kernel, lat_rows=lat_rows),
        grid=(b, lat_rows // tq),
        in_specs=in_specs,
        out_specs=[pl.BlockSpec((None, tq, wa), tile), pl.BlockSpec((None, tq, w), tile)],
        out_shape=[jax.ShapeDtypeStruct((b, rows, wa), BF16), jax.ShapeDtypeStruct((b, rows, w), BF16)],
        input_output_aliases={} if a_dst is None else {5: 0, 6: 1},
        compiler_params=_cparams(("parallel", "parallel")),
        name="lat_attention",
    )(*args)


def _s5_kernel(*refs, reverse, chunk, batch):
    if reverse:
        (u_ref, pm_ref, a_ref, bcat_ref, ccat_ref, o_ref, buf_a, buf_b, u_a, u_b, st) = refs
        rb_ref = pmt_ref = dsk_ref = wglu_ref = bglu_ref = None
    else:
        (u_ref, rb_ref, pm_ref, pmt_ref, a_ref, bcat_ref, ccat_ref, dsk_ref, wglu_ref, bglu_ref,
         o_ref, buf_a, buf_b, u_a, u_b, st) = refs
    rws = chunk * batch
    n_tiles = 2 * S5_LANES // MXU_TILE
    per_tile = chunk // n_tiles

    @pl.when(pl.program_id(0) == 0)
    def _():
        buf_a[...] = jnp.zeros_like(buf_a)
        buf_b[...] = jnp.zeros_like(buf_b)
        u_a[...] = jnp.zeros_like(u_a)
        u_b[...] = jnp.zeros_like(u_b)
        st[...] = jnp.zeros_like(st)

    a_r, a_i = a_ref[0], a_ref[1]

    def stage(pos_in, pos_out, cur, oth, u_cur):
        u_old = u_cur[...]
        ub = _dot(pm_ref[...], u_ref[:, pos_in].reshape(rws, S5_CHANNELS)).astype(BF16)
        u_cur[...] = ub
        sr, si = st[:, :S5_LANES], st[:, S5_LANES:]
        racc = None
        for j in range(n_tiles):
            for k in range(per_tile):
                idx = j * per_tile + k
                off = ((chunk - 1 - idx) if reverse else idx) * batch
                nr = a_r * sr - a_i * si + oth[off:off + batch, :S5_LANES]
                ni = a_r * si + a_i * sr + oth[off:off + batch, S5_LANES:]
                oth[off:off + batch, :S5_LANES] = nr
                oth[off:off + batch, S5_LANES:] = ni
                sr, si = nr, ni
            cols = slice(j * MXU_TILE, (j + 1) * MXU_TILE)
            part = _dot(cur[:, cols].astype(BF16), ccat_ref[cols, :])
            racc = part if racc is None else racc + part
            cur[:, cols] = _dot(ub, bcat_ref[:, cols])
        st[:, :S5_LANES] = sr
        st[:, S5_LANES:] = si
        if reverse:
            o_ref[pos_out] = racc
        else:
            y = jax.nn.gelu(racc + rb_ref[pos_out] + dsk_ref[...] * u_old.astype(F32))
            z = _dot(y.astype(BF16), wglu_ref[...]) + bglu_ref[...]
            o_tb = (z[:, :S5_CHANNELS] * jax.nn.sigmoid(z[:, S5_CHANNELS:])).astype(BF16)
            o_ref[:, pos_out] = _dot(pmt_ref[...], o_tb).astype(BF16).reshape(batch, chunk, S5_CHANNELS)

    first, second = (1, 0) if reverse else (0, 1)
    stage(first, first, buf_a, buf_b, u_a)
    stage(second, second, buf_b, buf_a, u_b)


def _s5_call(u4, rb, pm, pmt, a_ri, b_cat, c_cat, d_skip, w_glu, b_glu, *, layer, reverse, lat_rows):
    batch, n_all, chunk, _ = u4.shape
    rws = chunk * batch
    n_l = lat_rows // chunk
    n_c = n_all - n_l
    assert n_l % 2 == 0 and n_c % 2 == 0
    np_all, np_l, np_c = n_all // 2, n_l // 2, n_c // 2
    dirn = 1 if reverse else 0

    def pair(g):
        g = jnp.clip(g, 0, np_all - 1)
        if reverse:
            return np_all - 1 - g
        return jnp.where(g < np_c, np_l + g, g - np_c)

    def par4(g):
        return (layer, dirn, 0, 0)

    def lyr3(g):
        return (layer, 0, 0)

    in_specs = [pl.BlockSpec((batch, 2, chunk, S5_CHANNELS), lambda g: (0, pair(g), 0, 0))]
    args = [u4]
    if not reverse:
        in_specs.append(pl.BlockSpec((2, rws, S5_CHANNELS), lambda g: (pair(g - 1), 0, 0)))
        args.append(rb)
    in_specs.append(_const_spec((rws, rws), lambda g: (0, 0)))
    args.append(pm)
    if not reverse:
        in_specs.append(_const_spec((rws, rws), lambda g: (0, 0)))
        args.append(pmt)
    in_specs += [
        _const_spec((None, None, 2, batch, S5_LANES), lambda g: (layer, dirn, 0, 0, 0)),
        _const_spec((None, None, S5_CHANNELS, 2 * S5_LANES), par4),
        _const_spec((None, None, 2 * S5_LANES, S5_CHANNELS), par4),
    ]
    args += [a_ri, b_cat, c_cat]
    if reverse:
        out_spec = pl.BlockSpec((2, rws, S5_CHANNELS), lambda g: (pair(g - 1), 0, 0))
        out_shape = jax.ShapeDtypeStruct((np_all * 2, rws, S5_CHANNELS), F32)
    else:
        in_specs += [
            _const_spec((None, 1, S5_CHANNELS), lyr3),
            _const_spec((None, S5_CHANNELS, 2 * S5_CHANNELS), lyr3),
            _const_spec((None, 1, 2 * S5_CHANNELS), lyr3),
        ]
        args += [d_skip, w_glu, b_glu]
        out_spec = pl.BlockSpec((batch, 2, chunk, S5_CHANNELS), lambda g: (0, pair(g - 1), 0, 0))
        out_shape = jax.ShapeDtypeStruct(u4.shape, BF16)
    return pl.pallas_call(
        functools.partial(_s5_kernel, reverse=reverse, chunk=chunk, batch=batch),
        grid=(np_all + 1,),
        in_specs=in_specs,
        out_specs=out_spec,
        out_shape=out_shape,
        scratch_shapes=[
            pltpu.VMEM((rws, 2 * S5_LANES), F32),
            pltpu.VMEM((rws, 2 * S5_LANES), F32),
            pltpu.VMEM((rws, S5_CHANNELS), BF16),
            pltpu.VMEM((rws, S5_CHANNELS), BF16),
            pltpu.VMEM((batch, 2 * S5_LANES), F32),
        ],
        compiler_params=_cparams(("arbitrary",)),
        name="s5_bwd" if reverse else "s5_fwd",
    )(*args)


def _merge_kernel(x_ref, mod_ref, modt_ref, gpre_ref, gpost_ref, a_ref, b_ref, c_ref, d_ref, wg_ref, wb_ref, wo_ref,
                  *rest, tail, prep_next):
    if prep_next:
        wt_next_ref, o_ref, wp_next_ref, wg_next_ref = rest
        _wprep_kernel(wt_next_ref, wp_next_ref, wg_next_ref)
    else:
        o_ref, = rest
    tm, d = x_ref.shape
    ms = (mod_ref[...], modt_ref[...])
    pieces = _mod_pieces(tm, tail)
    h = jnp.concatenate(
        [(_rms(x_ref[r0:r1], gpre_ref[1:2]) * (1.0 + ms[k][4:5]) + ms[k][3:4]).astype(BF16) for r0, r1, k in pieces],
        axis=0)
    merged = None
    for i, br in enumerate((a_ref, b_ref, c_ref, d_ref)):
        gate = jax.nn.sigmoid(_dot(h, wg_ref[:, i * d:(i + 1) * d]))
        term = gate * _dot(br[...], wb_ref[i])
        merged = term if merged is None else merged + term
    y = _dot(merged.astype(BF16), wo_ref[...])
    for r0, r1, k in pieces:
        o_ref[r0:r1] = x_ref[r0:r1] + ms[k][5:6] * _rms(y[r0:r1], gpost_ref[1:2])


def _merge_call(x3, mods, norm_pre, norm_post, a3, b3, c3, d3, wg_bf, wb_bf, wo_bf, w_next, *, layer, tm, n_blk,
                ctx_rows):
    b, _, d = x3.shape
    bj = _lag_block(b, n_blk, 0)

    def row(q):
        return (*bj(q), 0)

    def lyr3(q):
        return (layer, 0, 0)

    in_specs = _lag_specs(b, tm, d, n_blk, layer, ctx_rows > 0, 0) + [
        _const_spec((None, 3, d), lyr3),
        _const_spec((None, 3, d), lyr3),
        pl.BlockSpec((None, tm, BRANCH_WIDTH), row),
        pl.BlockSpec((None, tm, BRANCH_WIDTH), row),
        pl.BlockSpec((None, tm, BRANCH_WIDTH), row),
        pl.BlockSpec((None, tm, BRANCH_WIDTH), row),
        _const_spec((d, N_BRANCH * d), lambda q: (0, 0)),
        _const_spec((None, N_BRANCH, BRANCH_WIDTH, d), lambda q: (layer, 0, 0, 0)),
        _const_spec((None, d, d), lyr3),
    ]
    args = [x3, mods, mods, norm_pre, norm_post, a3, b3, c3, d3, wg_bf, wb_bf, wo_bf]
    out_specs = [pl.BlockSpec((None, tm, d), row)]
    out_shape = [jax.ShapeDtypeStruct((b, n_blk * tm, d), F32)]
    if w_next is not None:
        wt, l_next = w_next
        n_in, tr = wt.shape[1], LANES
        n_slab = d // tr
        assert b * n_blk >= n_slab

        def slab(q):
            return jnp.minimum(q, n_slab - 1)

        in_specs.append(pl.BlockSpec((None, n_in, tr), lambda q: (l_next, 0, slab(q))))
        args.append(wt)
        out_specs += [pl.BlockSpec((tr, PC_END), lambda q: (slab(q), 0)),
                      pl.BlockSpec((tr, N_BRANCH * d), lambda q: (slab(q), 0))]
        out_shape += [jax.ShapeDtypeStruct((d, PC_END), BF16), jax.ShapeDtypeStruct((d, N_BRANCH * d), BF16)]
    return pl.pallas_call(
        functools.partial(_merge_kernel, tail=ctx_rows, prep_next=w_next is not None),
        grid=(b * n_blk,),
        in_specs=in_specs,
        out_specs=out_specs,
        out_shape=out_shape,
        compiler_params=_cparams(("arbitrary",)),
        name="merge",
    )(*args)


def _rot_cols(w):
    q = w.shape[-1] // 4
    return jnp.concatenate([-w[..., q:2 * q], w[..., 0:q], -w[..., 3 * q:4 * q], w[..., 2 * q:3 * q]], axis=-1)


def _rope_full(rows_n, rot_dim):
    f32 = np.float32
    axis_dim = rot_dim // 2
    inv_freq = (f32(ROPE_THETA) ** (-np.arange(0, axis_dim, 2, dtype=f32) / f32(axis_dim))).astype(f32)
    row = np.repeat(np.arange(rows_n, dtype=f32), GRID_W)
    col = np.tile(np.arange(GRID_W, dtype=f32), rows_n)
    ang_r = row[:, None] * inv_freq[None, :]
    ang_c = col[:, None] * inv_freq[None, :]
    cos = np.concatenate([np.cos(ang_r), np.cos(ang_r), np.cos(ang_c), np.cos(ang_c)], axis=-1)
    sin = np.concatenate([np.sin(ang_r), np.sin(ang_r), np.sin(ang_c), np.sin(ang_c)], axis=-1)
    return cos.astype(f32), sin.astype(f32)


def _rope_table(l, tm):
    f32 = np.float32
    cg, sg = _rope_full(l // GRID_W, HEAD_DIM)
    cm, sm = _rope_full(l // GRID_W, MLA_ROPE)
    scale = f32((MLA_NOPE + MLA_ROPE) ** -0.5)
    pad = MLA_SLOT - MLA_NOPE - MLA_ROPE

    def build(cg, sg, cm, sm):
        n = cg.shape[0]
        ones = np.ones((n, MLA_NOPE), f32)
        zeros = np.zeros((n, MLA_NOPE), f32)
        return np.concatenate([
            cg, cg, sg, sg,
            scale * ones, scale * cm, zeros[:, :pad],
            zeros, scale * sm, zeros[:, :pad],
            cm, sm, zeros,
        ], axis=-1)

    lat = build(cg, sg, cm, sm)
    one_g, zero_g = np.ones((tm, HEAD_DIM), f32), np.zeros((tm, HEAD_DIM), f32)
    ctx = build(one_g, zero_g, one_g[:, :MLA_ROPE], zero_g[:, :MLA_ROPE])
    return jnp.asarray(np.concatenate([lat, ctx], axis=0))


GQA_HEAD_ORDER = (0, 2, 1, 3)


def _proj_cols(w):
    o = 0

    def take(n):
        nonlocal o
        v = w[..., o:o + n]
        o += n
        return v

    kvl, kpe = take(MLA_KV_LORA), take(MLA_ROPE)
    gk, gv = take(GQA_KV_HEADS * HEAD_DIM), take(GQA_KV_HEADS * HEAD_DIM)
    u, ql, gq, z = take(S5_CHANNELS), take(MLA_Q_LORA), take(GQA_Q_HEADS * HEAD_DIM), take(2 * GMLP_WIDTH)
    gate = w[..., o:]
    gqh = [gq[..., i * HEAD_DIM:(i + 1) * HEAD_DIM] for i in range(GQA_Q_HEADS)]
    gkh = [gk[..., i * HEAD_DIM:(i + 1) * HEAD_DIM] for i in range(GQA_KV_HEADS)]
    pad = jnp.zeros(w.shape[:-1] + (PC_U - PC_A - MLA_KV_LORA - 2 * MLA_ROPE,), w.dtype)
    wp = jnp.concatenate([z, kvl, kpe, _rot_cols(kpe), pad, u, ql]
                         + [gqh[i] for i in GQA_HEAD_ORDER] + [_rot_cols(gqh[i]) for i in GQA_HEAD_ORDER]
                         + [gk] + [_rot_cols(h) for h in gkh] + [gv], axis=-1)
    return wp, gate


def _wprep_kernel(wt_ref, wp_ref, wg_ref):
    n = wt_ref.shape[0]
    full = n // LANES * LANES
    pieces = [wt_ref[r0:r0 + LANES, :].T for r0 in range(0, full, LANES)]
    if full < n:
        rest = jnp.concatenate([wt_ref[full:, :], jnp.zeros((LANES - (n - full), wt_ref.shape[1]), F32)], axis=0)
        pieces.append(rest.T[:, :n - full])
    wp, wg = _proj_cols(jnp.concatenate(pieces, axis=1))
    wp_ref[...] = wp.astype(BF16)
    wg_ref[...] = wg.astype(BF16)


def _wprep_call(wt, layer):
    _, n, d = wt.shape
    tr = 256
    n_gate = N_BRANCH * d
    return pl.pallas_call(
        _wprep_kernel,
        grid=(d // tr,),
        in_specs=[pl.BlockSpec((None, n, tr), lambda i: (layer, 0, i))],
        out_specs=[pl.BlockSpec((tr, PC_END), lambda i: (i, 0)), pl.BlockSpec((tr, n_gate), lambda i: (i, 0))],
        out_shape=[jax.ShapeDtypeStruct((d, PC_END), BF16), jax.ShapeDtypeStruct((d, n_gate), BF16)],
        compiler_params=_cparams(("parallel",)),
        name="w_prep",
    )(wt)


def _mla_kv_weight(w_ukv):
    depth = w_ukv.shape[0]
    wh = w_ukv.reshape(depth, MLA_KV_LORA, MLA_HEADS, MLA_NOPE + MLA_V)
    k_nope, v = wh[..., :MLA_NOPE], wh[..., MLA_NOPE:]
    kslot = jnp.concatenate([k_nope, jnp.zeros((depth, MLA_KV_LORA, MLA_HEADS, MLA_SLOT - MLA_NOPE), F32)], axis=-1)
    top = jnp.concatenate([kslot.reshape(depth, MLA_KV_LORA, -1), v.reshape(depth, MLA_KV_LORA, -1)], axis=-1)
    eye = jnp.eye(MLA_ROPE, dtype=F32)
    pe_slot = jnp.concatenate([jnp.zeros((MLA_ROPE, MLA_NOPE), F32), eye,
                               jnp.zeros((MLA_ROPE, MLA_SLOT - MLA_NOPE - MLA_ROPE), F32)], axis=-1)
    pe_rows = jnp.concatenate([jnp.tile(pe_slot, (1, MLA_HEADS)), jnp.zeros((MLA_ROPE, MLA_HEADS * MLA_V), F32)],
                              axis=-1)
    pe_rows = jnp.broadcast_to(pe_rows, (depth,) + pe_rows.shape)
    tail = jnp.zeros((depth, 256 - MLA_KV_LORA - 2 * MLA_ROPE, MLA_KV_COLS), F32)
    return jnp.concatenate([top, pe_rows, pe_rows, tail], axis=1).astype(BF16)


def _mla_q_weight(w_uq):
    depth = w_uq.shape[0]
    wh = w_uq.reshape(depth, MLA_Q_LORA, MLA_HEADS, MLA_NOPE + MLA_ROPE)
    nope, pe = wh[..., :MLA_NOPE], wh[..., MLA_NOPE:]
    pad = jnp.zeros((depth, MLA_Q_LORA, MLA_HEADS, MLA_SLOT - MLA_NOPE - MLA_ROPE), F32)
    full = jnp.concatenate([nope, pe, pad], axis=-1).reshape(depth, MLA_Q_LORA, -1)
    rot = jnp.concatenate([jnp.zeros_like(nope), _rot_cols(pe), pad], axis=-1).reshape(depth, MLA_Q_LORA, -1)
    return jnp.concatenate([full, rot], axis=-1).astype(BF16)


def _block_diag(w):
    g, a, b = w.shape[-3:]
    lead = w.shape[:-3]
    cols = jnp.swapaxes(w, -3, -2).reshape(lead + (a, g * b))
    tiled = jnp.tile(cols, (1,) * len(lead) + (g, 1))
    same = (jnp.arange(g * a)[:, None] // a) == (jnp.arange(g * b)[None, :] // b)
    return jnp.where(same, tiled, jnp.zeros((), w.dtype))


def kernel(x, c, ctx, c_ctx, w_ada, b_ada, norm_pre, norm_post, w_ffn_in, w_ffn_out, w_in, mla_q_norm, mla_w_uq,
           mla_kv_norm, mla_w_ukv, gqa_sink, s5_lam_re, s5_lam_im, s5_log_dt, s5_b_re, s5_b_im, s5_c_re, s5_c_im,
           s5_d, s5_w_glu, s5_b_glu, gmlp_norm, gmlp_w_s, gmlp_b_s, w_branch, w_out):
    b, l, d = x.shape
    cl = ctx.shape[1]
    r = l + cl
    depth = w_ada.shape[0]
    assert b == SUBLANES and b < MOD_ROWS
    assert l % GRID_W == 0 and l >= 3 * W_BLOCK and l % cl == 0
    assert l % GMLP_CHUNK == 0 and WIDE_TILE % GMLP_CHUNK == 0 and l % MLA_Q_TILE == 0 and l % GQA_Q_TILE == 0
    assert l % S5_CHUNK == 0 and cl % S5_CHUNK == 0 and l % LAT_TILE == 0
    assert r % WIDE_TILE == 0 and r % XWIDE_TILE == 0 and XWIDE_TILE % GMLP_CHUNK == 0 and cl <= WIDE_TILE

    wi_bf, wo_bf_ffn = w_ffn_in[0, 0].astype(BF16), w_ffn_out[0, 0].astype(BF16)
    w_in_t = jnp.swapaxes(w_in, 1, 2)
    wp_bf, wg_bf = _wprep_call(w_in_t, 0)
    wkv_bf = _mla_kv_weight(mla_w_ukv)
    wq_bf = _mla_q_weight(mla_w_uq)
    head_order = jnp.array(GQA_HEAD_ORDER)
    wb1 = w_branch[:, 1].reshape(depth, GQA_Q_HEADS, HEAD_DIM, d)[:, head_order].reshape(depth, BRANCH_WIDTH, d)
    wb_bf = jnp.concatenate([w_branch[:, :1], wb1[:, None], w_branch[:, 2:]], axis=1).astype(BF16)
    wo_bf = w_out.astype(BF16)
    sink_rows = jnp.broadcast_to(
        jnp.concatenate([gqa_sink[:, head_order], jnp.zeros((depth, SUBLANES - GQA_Q_HEADS), F32)], axis=1)[:, :, None],
        (depth, SUBLANES, LANES))
    ws_bf = jnp.transpose(gmlp_w_s, (0, 2, 1, 3)).reshape(depth, GMLP_CHUNK, GMLP_GROUPS * GMLP_CHUNK).astype(BF16)
    bs_f = jnp.repeat(jnp.transpose(gmlp_b_s, (0, 2, 1)), GMLP_WIDTH // GMLP_GROUPS, axis=2)
    tab = _rope_table(l, cl)
    mla_kv_norm3 = mla_kv_norm[:, None, :]
    mla_q_norm3 = mla_q_norm[:, None, :]
    gmlp_norm3 = gmlp_norm[:, None, :]
    s5_d3 = s5_d[:, None, :]
    s5_b_glu3 = s5_b_glu[:, None, :]
    wglu_bf = s5_w_glu.astype(BF16)
    tb = jnp.arange(S5_CHUNK * b)
    pm = (tb[None, :] == ((tb % b) * S5_CHUNK + tb // b)[:, None]).astype(BF16)
    pmt = pm.T

    hg, p, g = S5_GROUP, S5_STATE, S5_GROUPS
    n_par = depth * 2 * g
    rep = lambda t: jnp.repeat(t.reshape(n_par, p), hg, axis=1)
    ldt = jnp.broadcast_to(s5_log_dt[..., None], (depth, 2, g, p))
    a_re_x, a_im_x, bb_re, bb_im = _s5_disc_call(
        rep(s5_lam_re), rep(s5_lam_im), rep(ldt), s5_b_re.reshape(n_par, p * hg), s5_b_im.reshape(n_par, p * hg))
    a_ri = jnp.broadcast_to(
        jnp.stack([a_re_x[:, ::hg].reshape(depth, 2, g * p), a_im_x[:, ::hg].reshape(depth, 2, g * p)], axis=2)
        [:, :, :, None, :], (depth, 2, 2, b, g * p))
    bb_re = jnp.swapaxes(bb_re.reshape(depth, 2, g, p, hg), -1, -2)
    bb_im = jnp.swapaxes(bb_im.reshape(depth, 2, g, p, hg), -1, -2)
    s5_b_cat = jnp.concatenate([_block_diag(bb_re), _block_diag(bb_im)], axis=-1).astype(BF16)
    s5_c_cat = jnp.concatenate([_block_diag(jnp.swapaxes(s5_c_re, -1, -2)),
                                _block_diag(jnp.swapaxes(-s5_c_im, -1, -2))], axis=-2).astype(BF16)

    cs = jnp.concatenate([c, c_ctx[None, :], jnp.zeros((MOD_ROWS - b - 1, d), F32)], axis=0)
    mods = _ada_call(cs, w_ada, b_ada[:, None, :]).reshape(depth, MOD_ROWS, N_MOD, d)

    xs = jnp.concatenate([x, ctx], axis=1)
    n_wide = r // WIDE_TILE
    for layer in range(depth):
        last = layer == depth - 1
        xs, wi_bf, wo_bf_ffn = _ffn_call(xs, mods, norm_pre, norm_post, wi_bf, wo_bf_ffn,
                                         (w_ffn_in, w_ffn_out, layer, 1),
                                         layer=layer, s=0, tm=WIDE_TILE, n_blk=n_wide, ctx_rows=cl)
        mq, mkv, gq, gkv, u3, d3 = _proj_call(
            xs, mods, norm_pre, wp_bf, tab, mla_kv_norm3, wkv_bf, mla_q_norm3, wq_bf, gmlp_norm3, ws_bf, bs_f,
            layer=layer, lat_rows=l)
        a_dst, b_dst = (None, None) if last else _ctx_attn_call(mq, mkv, gq, gkv, sink_rows, layer=layer, lat_rows=l)
        a3, b3 = _lat_attn_call(mq, mkv, gq, gkv, sink_rows, a_dst, b_dst, layer=layer, lat_rows=l)
        u4 = u3.reshape(b, r // S5_CHUNK, S5_CHUNK, S5_CHANNELS)
        rb = _s5_call(u4, None, pm, None, a_ri, s5_b_cat, s5_c_cat, None, None, None,
                      layer=layer, reverse=True, lat_rows=l)
        c3 = _s5_call(u4, rb, pm, pmt, a_ri, s5_b_cat, s5_c_cat, s5_d3, wglu_bf, s5_b_glu3,
                      layer=layer, reverse=False, lat_rows=l).reshape(b, r, S5_CHANNELS)
        if last:
            tm, n_blk, ctx_rows = LAT_TILE, l // LAT_TILE, 0
            tm_merge, n_merge = tm, n_blk
        else:
            tm, n_blk, ctx_rows = WIDE_TILE, n_wide, cl
            tm_merge, n_merge = tm, n_blk
        outs = _merge_call(xs, mods, norm_pre, norm_post, a3, b3, c3, d3, wg_bf, wb_bf, wo_bf,
                           None if last else (w_in_t, layer + 1),
                           layer=layer, tm=tm_merge, n_blk=n_merge, ctx_rows=ctx_rows)
        xs = outs[0]
        if not last:
            wp_bf, wg_bf = outs[1], outs[2]
        outs = _ffn_call(xs, mods, norm_pre, norm_post, wi_bf, wo_bf_ffn,
                         None if last else (w_ffn_in, w_ffn_out, layer + 1, 0),
                         layer=layer, s=2, tm=tm, n_blk=n_blk, ctx_rows=ctx_rows)
        xs = outs[0]
        if not last:
            wi_bf, wo_bf_ffn = outs[1], outs[2]
    return xs
```
